```python
import math
import jax, jax.numpy as jnp
from jax import lax
import numpy as np

D_MODEL = 4096
BATCH = 4
SEQ = 2048
DEPTH = 2
DEC_BATCH = 128
DEC_SEQ = 4
PAST_LEN = 16384
PAGE_SIZE = 128

N_MEM = 256
D_BRANCH = D_MODEL
XATT_HEADS = 4
XATT_HEAD_DIM = D_MODEL // 8
D_XATT = XATT_HEADS * XATT_HEAD_DIM
D_GATE = D_BRANCH + D_XATT
POOL_WINDOWS = (2, 4, 8, 16)
POOL_GROUP = D_BRANCH // len(POOL_WINDOWS)
POOL_HIST = max(POOL_WINDOWS) - 1
MLSTM_HEADS = 8
MLSTM_DV = D_BRANCH // MLSTM_HEADS
MLSTM_DQK = MLSTM_DV // 2
MLSTM_CHUNK = 64
GATE_SOFTCAP = 15.0
NORM_EPS = 1e-6
N_POOL_LAYERS = (DEPTH + 1) // 2
N_MLSTM_LAYERS = DEPTH // 2
POOL_IN_SIZES = (D_BRANCH, D_XATT, D_GATE)
MLSTM_IN_SIZES = (MLSTM_HEADS * MLSTM_DQK, MLSTM_HEADS * MLSTM_DQK, D_BRANCH, D_BRANCH,
                  MLSTM_HEADS, MLSTM_HEADS, D_XATT, D_GATE)
D_IN_POOL = sum(POOL_IN_SIZES)
D_IN_MLSTM = sum(MLSTM_IN_SIZES)

kernel_name = 'hybrid_pool_mlstm_memory_decoder_step'


def _split_last(a, sizes):
    offs, acc = [], 0
    for s in sizes[:-1]:
        acc += s
        offs.append(acc)
    return jnp.split(a, offs, axis=-1)


def rmsnorm(x, g):
    xf = x.astype(jnp.float32)
    r = lax.rsqrt(jnp.mean(xf * xf, axis=-1, keepdims=True) + NORM_EPS)
    return (xf * r * g.astype(jnp.float32)).astype(x.dtype)


def softcap(a):
    return GATE_SOFTCAP * jnp.tanh(a / GATE_SOFTCAP)


def mem_kv(mem, g_mem, w_kv):
    B = mem.shape[0]
    kv = jnp.einsum('bsd,de->bse', rmsnorm(mem, g_mem), w_kv)
    k, v = jnp.split(kv, 2, axis=-1)
    return (k.reshape(B, N_MEM, XATT_HEADS, XATT_HEAD_DIM),
            v.reshape(B, N_MEM, XATT_HEADS, XATT_HEAD_DIM))


def cross_attend(xq, k, v):
    B, T = xq.shape[:2]
    q = xq.reshape(B, T, XATT_HEADS, XATT_HEAD_DIM).astype(jnp.float32)
    s = jnp.einsum('bthd,bshd->bhts', q, k.astype(jnp.float32)) * (XATT_HEAD_DIM ** -0.5)
    p = jax.nn.softmax(s, axis=-1)
    o = jnp.einsum('bhts,bshd->bthd', p, v.astype(jnp.float32))
    return o.reshape(B, T, D_XATT).astype(xq.dtype)


def multiscale_pool(u, hist, pos0, w_group, scale):
    B, T, _ = u.shape
    u_ext = jnp.concatenate([hist.astype(u.dtype), u], axis=1)
    cs = lax.cumsum(u_ext.astype(jnp.float32), axis=1)
    cs = jnp.concatenate([jnp.zeros((B, 1, D_BRANCH), jnp.float32), cs], axis=1)
    pos = pos0 + jnp.arange(T)
    groups = []
    for g, w in enumerate(POOL_WINDOWS):
        lo, hi = g * POOL_GROUP, (g + 1) * POOL_GROUP
        win = (cs[:, POOL_HIST + 1:POOL_HIST + 1 + T, lo:hi]
               - cs[:, POOL_HIST + 1 - w:POOL_HIST + 1 - w + T, lo:hi])
        cnt = jnp.minimum(w, pos + 1).astype(jnp.float32)[None, :, None]
        groups.append(win / cnt - u[..., lo:hi].astype(jnp.float32))
    pooled = jnp.stack(groups, axis=2)
    mixed = jnp.einsum('btgc,gcd->btgd', pooled, w_group.astype(jnp.float32)).reshape(B, T, D_BRANCH)
    mixed = mixed * scale.astype(jnp.float32)
    return mixed.astype(u.dtype), u_ext[:, -POOL_HIST:]


def _to_chunks(a, L):
    B, T = a.shape[:2]
    a = a.reshape((B, T // L, L) + a.shape[2:])
    return a.transpose((1, 0, 3, 2) + tuple(range(4, a.ndim)))


def _mlstm_chunk(carry, chunk):
    C, n, m = carry
    q, k, v, ig, lf = chunk
    L = q.shape[2]
    b = jnp.cumsum(lf, axis=-1)
    causal = jnp.tril(jnp.ones((L, L), dtype=bool))
    d = jnp.where(causal, b[..., :, None] - b[..., None, :] + ig[..., None, :], -jnp.inf)
    inter = b + m[..., None]
    m_t = jnp.maximum(inter, jnp.max(d, axis=-1))
    w = jnp.exp(d - m_t[..., None])
    w_inter = jnp.exp(inter - m_t)
    s = jnp.einsum('bhtd,bhsd->bhts', q, k) * w
    num = (jnp.einsum('bhts,bhsv->bhtv', s, v)
           + w_inter[..., None] * jnp.einsum('bhvd,bhtd->bhtv', C, q))
    den = jnp.sum(s, axis=-1) + w_inter * jnp.einsum('bhd,bhtd->bht', n, q)
    h = num / jnp.maximum(jnp.abs(den), jnp.exp(-m_t))[..., None]
    b_last = b[..., -1]
    m_new = m_t[..., -1]
    w_state = jnp.exp(b_last[..., None] - b + ig - m_new[..., None])
    decay = jnp.exp(b_last + m - m_new)
    C_new = decay[..., None, None] * C + jnp.einsum('bhsv,bhsd->bhvd', w_state[..., None] * v, k)
    n_new = decay[..., None] * n + jnp.einsum('bhs,bhsd->bhd', w_state, k)
    return (C_new, n_new, m_new), h


def mlstm_recurrence(q, k, v, ig, lf, C0, n0, m0):
    B, T = q.shape[:2]
    L = math.gcd(T, MLSTM_CHUNK)
    xs = (_to_chunks(q, L), _to_chunks(k, L), _to_chunks(v, L), _to_chunks(ig, L), _to_chunks(lf, L))
    (C1, n1, m1), h = lax.scan(_mlstm_chunk, (C0, n0, m0), xs)
    h = h.transpose(1, 0, 3, 2, 4).reshape(B, T, MLSTM_HEADS, MLSTM_DV)
    return h, C1, n1, m1


def _finish_layer(x, mix, xq, z, mem_k, mem_v, w_out, g_post):
    att = cross_attend(xq, mem_k, mem_v)
    branch = jnp.concatenate([mix.astype(x.dtype), att], axis=-1) * jax.nn.silu(z)
    y = jnp.einsum('bte,ed->btd', branch, w_out)
    return x + rmsnorm(y, g_post)


def pool_layer(x, hist, pos0, mem_k, mem_v, g_pre, g_post, w_in, w_group, scale, w_out):
    h = rmsnorm(x, g_pre)
    proj = jnp.einsum('btd,de->bte', h, w_in)
    u, xq, z = _split_last(proj, POOL_IN_SIZES)
    mix, new_hist = multiscale_pool(u, hist, pos0, w_group, scale)
    return _finish_layer(x, mix, xq, z, mem_k, mem_v, w_out, g_post), new_hist


def mlstm_layer(x, C0, n0, m0, mem_k, mem_v, g_pre, g_post, w_in, b_i, b_f, g_head, w_out):
    B, T, _ = x.shape
    f32 = jnp.float32
    h = rmsnorm(x, g_pre)
    proj = jnp.einsum('btd,de->bte', h, w_in)
    q, k, v, o, ig, fg, xq, z = _split_last(proj, MLSTM_IN_SIZES)
    q = q.reshape(B, T, MLSTM_HEADS, MLSTM_DQK).astype(f32) * (MLSTM_DQK ** -0.5)
    k = k.reshape(B, T, MLSTM_HEADS, MLSTM_DQK).astype(f32)
    v = v.reshape(B, T, MLSTM_HEADS, MLSTM_DV).astype(f32)
    ig = softcap(ig.astype(f32) + b_i.astype(f32))
    lf = jax.nn.log_sigmoid(softcap(fg.astype(f32) + b_f.astype(f32)))
    hh, C1, n1, m1 = mlstm_recurrence(q, k, v, ig, lf, C0.astype(f32), n0.astype(f32), m0.astype(f32))
    hh = rmsnorm(hh, g_head.reshape(MLSTM_HEADS, MLSTM_DV))
    mix = jax.nn.sigmoid(o.astype(f32)) * hh.reshape(B, T, D_BRANCH)
    return _finish_layer(x, mix, xq, z, mem_k, mem_v, w_out, g_post), C1, n1, m1


def setup_inputs(seed: int = 0) -> dict:
    key = jax.random.key(seed)
    ks = jax.random.split(key, 24)
    nrm = jax.random.normal
    f32 = jnp.float32
    NP, NL = N_POOL_LAYERS, N_MLSTM_LAYERS
    return {
        'x_prompt': nrm(ks[0], (BATCH, SEQ, D_MODEL), f32),
        'x_sample': nrm(ks[1], (DEC_BATCH, DEC_SEQ, D_MODEL), f32),
        'mem_prompt': nrm(ks[2], (BATCH, N_MEM, D_MODEL), f32),
        'state_pool': nrm(ks[3], (NP, DEC_BATCH, POOL_HIST, D_BRANCH), f32),
        'state_mlstm_C': 0.1 * nrm(ks[4], (NL, DEC_BATCH, MLSTM_HEADS, MLSTM_DV, MLSTM_DQK), f32),
        'state_mlstm_n': 0.1 * nrm(ks[5], (NL, DEC_BATCH, MLSTM_HEADS, MLSTM_DQK), f32),
        'state_mlstm_m': jax.random.uniform(ks[6], (NL, DEC_BATCH, MLSTM_HEADS), f32, 0.0, 3.0),
        'cache_mem_k': nrm(ks[7], (DEPTH, DEC_BATCH, N_MEM, XATT_HEADS, XATT_HEAD_DIM), f32),
        'cache_mem_v': nrm(ks[8], (DEPTH, DEC_BATCH, N_MEM, XATT_HEADS, XATT_HEAD_DIM), f32),
        'norm_pre': 1.0 + 0.1 * nrm(ks[9], (DEPTH, D_MODEL), f32),
        'norm_post': 1.0 + 0.1 * nrm(ks[10], (DEPTH, D_MODEL), f32),
        'norm_mem': 1.0 + 0.1 * nrm(ks[11], (DEPTH, D_MODEL), f32),
        'w_mem_kv': nrm(ks[12], (DEPTH, D_MODEL, 2 * D_XATT), f32) * D_MODEL ** -0.5,
        'w_out': nrm(ks[13], (DEPTH, D_GATE, D_MODEL), f32) * D_GATE ** -0.5,
        'w_in_pool': nrm(ks[14], (NP, D_MODEL, D_IN_POOL), f32) * D_MODEL ** -0.5,
        'w_pool_group': nrm(ks[15], (NP, len(POOL_WINDOWS), POOL_GROUP, POOL_GROUP), f32) * POOL_GROUP ** -0.5,
        'pool_scale': 1.0 + 0.1 * nrm(ks[16], (NP, D_BRANCH), f32),
        'w_in_mlstm': nrm(ks[17], (NL, D_MODEL, D_IN_MLSTM), f32) * D_MODEL ** -0.5,
        'b_igate': 0.1 * nrm(ks[18], (NL, MLSTM_HEADS), f32),
        'b_fgate': jnp.linspace(3.0, 6.0, MLSTM_HEADS, dtype=f32)[None, :] + 0.1 * nrm(ks[19], (NL, MLSTM_HEADS), f32),
        'mlstm_head_norm': 1.0 + 0.1 * nrm(ks[20], (NL, D_BRANCH), f32),
    }


def reference(x_prompt, x_sample, mem_prompt, state_pool, state_mlstm_C, state_mlstm_n, state_mlstm_m,
              cache_mem_k, cache_mem_v, norm_pre, norm_post, norm_mem, w_mem_kv, w_out,
              w_in_pool, w_pool_group, pool_scale, w_in_mlstm, b_igate, b_fgate, mlstm_head_norm):
    B = x_prompt.shape[0]
    Bs = x_sample.shape[0]
    f32 = jnp.float32
    xp, xs = x_prompt, x_sample
    mem_k_l, mem_v_l = [], []
    pool_p_l, pool_s_l = [], []
    Cp_l, np_l, mp_l, Cs_l, ns_l, ms_l = [], [], [], [], [], []
    for layer in range(DEPTH):
        j = layer // 2
        mk, mv = mem_kv(mem_prompt, norm_mem[layer], w_mem_kv[layer])
        mem_k_l.append(mk)
        mem_v_l.append(mv)
        if layer % 2 == 0:
            pw = (norm_pre[layer], norm_post[layer], w_in_pool[j], w_pool_group[j], pool_scale[j], w_out[layer])
            xp, hp = pool_layer(xp, jnp.zeros((B, POOL_HIST, D_BRANCH), xp.dtype), 0, mk, mv, *pw)
            xs, hs = pool_layer(xs, state_pool[j], PAST_LEN, cache_mem_k[layer], cache_mem_v[layer], *pw)
            pool_p_l.append(hp)
            pool_s_l.append(hs)
        else:
            mw = (norm_pre[layer], norm_post[layer], w_in_mlstm[j], b_igate[j], b_fgate[j],
                  mlstm_head_norm[j], w_out[layer])
            C0 = jnp.zeros((B, MLSTM_HEADS, MLSTM_DV, MLSTM_DQK), f32)
            n0 = jnp.zeros((B, MLSTM_HEADS, MLSTM_DQK), f32)
            m0 = jnp.zeros((B, MLSTM_HEADS), f32)
            xp, Cp, npr, mp = mlstm_layer(xp, C0, n0, m0, mk, mv, *mw)
            xs, Cs, ns, ms = mlstm_layer(xs, state_mlstm_C[j], state_mlstm_n[j], state_mlstm_m[j],
                                         cache_mem_k[layer], cache_mem_v[layer], *mw)
            Cp_l.append(Cp); np_l.append(npr); mp_l.append(mp)
            Cs_l.append(Cs); ns_l.append(ns); ms_l.append(ms)
    pool_prompt = jnp.stack(pool_p_l)
    mem_k_prompt = jnp.stack(mem_k_l)
    mem_v_prompt = jnp.stack(mem_v_l)
    C_prompt = jnp.stack(Cp_l)
    n_prompt = jnp.stack(np_l)
    m_prompt = jnp.stack(mp_l)
    pool_sample = jnp.stack(pool_s_l)
    C_sample = jnp.stack(Cs_l)
    n_sample = jnp.stack(ns_l)
    m_sample = jnp.stack(ms_l)
    return (xp, xs, pool_prompt, mem_k_prompt, mem_v_prompt, C_prompt, n_prompt, m_prompt,
            pool_sample, C_sample, n_sample, m_sample)
```

```python
import functools

import jax
import jax.numpy as jnp
from jax import lax
from jax.experimental import pallas as pl
from jax.experimental.pallas import tpu as pltpu

F32 = jnp.float32
BF16 = jnp.bfloat16

NORM_EPS = 1e-6
GATE_SOFTCAP = 15.0
POOL_WINDOWS = (2, 4, 8, 16)
POOL_HIST = max(POOL_WINDOWS) - 1
HALO = POOL_HIST + 1
XATT_HEADS = 4
MLSTM_HEADS = 8
PAST_LEN = 16384
GATE_LANES = 128
MLSTM_CHUNK = 256
SAMPLE_BATCH_BLOCK = 8

V7X_VMEM_BYTES = 64 * 1024 * 1024
VMEM_CAP = V7X_VMEM_BYTES - 8 * 1024 * 1024


def _params(semantics, vmem_bytes):
    limit = int(min(max(vmem_bytes, 16 * 1024 * 1024), VMEM_CAP))
    return pltpu.CompilerParams(dimension_semantics=semantics, vmem_limit_bytes=limit)


def _sigmoid(x):
    return 1.0 / (1.0 + jnp.exp(-x))


def _silu(x):
    return x * _sigmoid(x)


def _rms(x, g):
    r = lax.rsqrt(jnp.mean(x * x, axis=-1, keepdims=True) + NORM_EPS)
    return x * r * g


def _rmsnorm_kernel(x_ref, g_ref, o_ref):
    o_ref[...] = _rms(x_ref[...], g_ref[...]).astype(o_ref.dtype)


def rmsnorm_cast(x, g, tm=256):
    M, D = x.shape
    tm = min(tm, M)
    return pl.pallas_call(
        _rmsnorm_kernel,
        grid=(M // tm,),
        in_specs=[pl.BlockSpec((tm, D), lambda i: (i, 0)),
                  pl.BlockSpec((1, D), lambda i: (0, 0))],
        out_specs=pl.BlockSpec((tm, D), lambda i: (i, 0)),
        out_shape=jax.ShapeDtypeStruct((M, D), BF16),
        compiler_params=_params(("parallel",), 8 * tm * D * 4),
        name="rmsnorm_cast",
    )(x, g.reshape(1, D))


def _post_kernel(x_ref, y_ref, g_ref, gn_ref, o_ref, h_ref):
    out = x_ref[...] + _rms(y_ref[...], g_ref[...])
    o_ref[...] = out
    h_ref[...] = _rms(out, gn_ref[...]).astype(h_ref.dtype)


def post_norm_residual(x, y, g_post, g_next, tm=256):
    M, D = x.shape
    tm = min(tm, M)
    row = pl.BlockSpec((tm, D), lambda i: (i, 0))
    vec = pl.BlockSpec((1, D), lambda i: (0, 0))
    return pl.pallas_call(
        _post_kernel,
        grid=(M // tm,),
        in_specs=[row, row, vec, vec],
        out_specs=[row, row],
        out_shape=[jax.ShapeDtypeStruct((M, D), F32), jax.ShapeDtypeStruct((M, D), BF16)],
        compiler_params=_params(("parallel",), 12 * tm * D * 4),
        name="post_norm_residual",
    )(x, y, g_post.reshape(1, D), g_next.reshape(1, D))


def _mm_kernel(*refs, ksizes):
    n_a = len(ksizes)
    a_refs, w_ref, o_ref = refs[:n_a], refs[n_a], refs[n_a + 1]
    acc = None
    off = 0
    for a_ref, ks in zip(a_refs, ksizes):
        part = jnp.dot(a_ref[...], w_ref[off:off + ks, :], preferred_element_type=F32)
        acc = part if acc is None else acc + part
        off += ks
    o_ref[...] = acc.astype(o_ref.dtype)


def matmul(a_list, w, out_dtype, tm=1024, tn=1024):
    M = a_list[0].shape[0]
    K, N = w.shape
    ksizes = tuple(a.shape[1] for a in a_list)
    assert sum(ksizes) == K
    tm, tn = min(tm, M), min(tn, N)
    osz = jnp.dtype(out_dtype).itemsize
    vmem = 2 * (tm * K * 2 + K * tn * 2 + tm * tn * osz) + 2 * tm * tn * 4
    return pl.pallas_call(
        functools.partial(_mm_kernel, ksizes=ksizes),
        grid=(N // tn, M // tm),
        in_specs=[pl.BlockSpec((tm, ks), lambda j, i: (i, 0)) for ks in ksizes]
        + [pl.BlockSpec((K, tn), lambda j, i: (0, j))],
        out_specs=pl.BlockSpec((tm, tn), lambda j, i: (i, j)),
        out_shape=jax.ShapeDtypeStruct((M, N), out_dtype),
        compiler_params=_params(("parallel", "parallel"), vmem + (4 << 20)),
        name="matmul",
    )(*a_list, w)


def _gates_kernel(h_ref, w_ref, b_ref, o_ref):
    nh = MLSTM_HEADS
    pre = jnp.dot(h_ref[...], w_ref[...], preferred_element_type=F32) + b_ref[...]
    capped = GATE_SOFTCAP * jnp.tanh(pre / GATE_SOFTCAP)
    log_sig = jnp.minimum(capped, 0.0) - jnp.log(1.0 + jnp.exp(-jnp.abs(capped)))
    lane = lax.broadcasted_iota(jnp.int32, capped.shape, 1)
    o_ref[...] = jnp.where(lane < nh, capped, log_sig)


def mlstm_gates(h, w_gates, bias, tm=512):
    M, D = h.shape
    tm = min(tm, M)
    return pl.pallas_call(
        _gates_kernel,
        grid=(M // tm,),
        in_specs=[pl.BlockSpec((tm, D), lambda i: (i, 0)),
                  pl.BlockSpec((D, GATE_LANES), lambda i: (0, 0)),
                  pl.BlockSpec((1, GATE_LANES), lambda i: (0, 0))],
        out_specs=pl.BlockSpec((tm, GATE_LANES), lambda i: (i, 0)),
        out_shape=jax.ShapeDtypeStruct((M, GATE_LANES), F32),
        compiler_params=_params(("parallel",), 4 * tm * D * 2 + 4 * D * GATE_LANES * 2),
        name="mlstm_gates",
    )(h, w_gates, bias)


def _group_mm_kernel(a_ref, w_ref, s_ref, z_ref, o_ref):
    mixed = jnp.dot(a_ref[...], w_ref[...], preferred_element_type=F32) * s_ref[...]
    o_ref[...] = (mixed * _silu(z_ref[...].astype(F32))).astype(o_ref.dtype)


def group_mix_gate(pooled, w_group, scale, proj, z_off, tm=512):
    M, D = pooled.shape
    G, Gs, _ = w_group.shape
    tm = min(tm, M)
    zb = z_off // Gs
    return pl.pallas_call(
        _group_mm_kernel,
        grid=(G, M // tm),
        in_specs=[pl.BlockSpec((tm, Gs), lambda g, i: (i, g)),
                  pl.BlockSpec((None, Gs, Gs), lambda g, i: (g, 0, 0)),
                  pl.BlockSpec((1, Gs), lambda g, i: (0, g)),
                  pl.BlockSpec((tm, Gs), lambda g, i: (i, zb + g))],
        out_specs=pl.BlockSpec((tm, Gs), lambda g, i: (i, g)),
        out_shape=jax.ShapeDtypeStruct((M, D), BF16),
        compiler_params=_params(("parallel", "parallel"), 6 * tm * Gs * 4 + 4 * Gs * Gs * 2),
        name="group_mix_gate",
    )(pooled, w_group, scale.reshape(1, D), proj)


def _window_sums(ext, levels):
    sums = []
    cur = ext
    for lv in range(levels):
        cur = cur + pltpu.roll(cur, 1 << lv, 0)
        sums.append(cur)
    return sums


def _pool_prompt_kernel(u_ref, halo_ref, hist_ref, o_ref, *, pos0, tt):
    i = pl.program_id(1)
    gs = u_ref.shape[1] // len(POOL_WINDOWS)
    row = lax.broadcasted_iota(jnp.int32, (tt, 1), 0) + i * tt
    pos1 = (row + (pos0 + 1)).astype(F32)
    first = i == 0
    for g, w in enumerate(POOL_WINDOWS):
        lo, hi = g * gs, (g + 1) * gs
        u = u_ref[:, lo:hi].astype(F32)
        halo = jnp.where(first, hist_ref[:, lo:hi], halo_ref[:, lo:hi].astype(F32))
        ext = jnp.concatenate([halo, u], axis=0)
        win = _window_sums(ext, g + 1)[-1][HALO:, :]
        cnt = jnp.minimum(float(w), pos1)
        o_ref[:, lo:hi] = (win / cnt - u).astype(o_ref.dtype)


def pool_prompt(proj, hist, batch, pos0, tt=256):
    D = hist.shape[-1]
    T = proj.shape[0] // batch
    tt = min(tt, T)
    nt = T // tt
    hist16 = jnp.concatenate([jnp.zeros((batch, 1, D), F32), hist.astype(F32)], axis=1)
    return pl.pallas_call(
        functools.partial(_pool_prompt_kernel, pos0=pos0, tt=tt),
        grid=(batch, nt),
        in_specs=[pl.BlockSpec((tt, D), lambda b, i: (b * nt + i, 0)),
                  pl.BlockSpec((HALO, D),
                               lambda b, i: (jnp.maximum((b * nt + i) * (tt // HALO) - 1, 0), 0)),
                  pl.BlockSpec((None, HALO, D), lambda b, i: (b, 0, 0))],
        out_specs=pl.BlockSpec((tt, D), lambda b, i: (b * nt + i, 0)),
        out_shape=jax.ShapeDtypeStruct((batch * T, D), BF16),
        compiler_params=_params(("parallel", "parallel"), 16 * (tt + HALO) * D * 4),
        name="pool_prompt",
    )(proj, proj, hist16)


def _pool_sample_kernel(ext_ref, o_ref, *, pos0):
    steps = o_ref.shape[0]
    gs = ext_ref.shape[2] // len(POOL_WINDOWS)
    for g, w in enumerate(POOL_WINDOWS):
        lo, hi = g * gs, (g + 1) * gs
        for t in range(steps):
            cur = ext_ref[POOL_HIST + t, :, lo:hi]
            win = cur
            for r in range(1, w):
                win = win + ext_ref[POOL_HIST + t - r, :, lo:hi]
            cnt = float(min(w, pos0 + t + 1))
            o_ref[t, :, lo:hi] = (win / cnt - cur).astype(o_ref.dtype)


def pool_sample(ext_t, pos0, bt=32):
    R, B, D = ext_t.shape
    T = R - POOL_HIST
    bt = min(bt, B)
    return pl.pallas_call(
        functools.partial(_pool_sample_kernel, pos0=pos0),
        grid=(B // bt,),
        in_specs=[pl.BlockSpec((R, bt, D), lambda i: (0, i, 0))],
        out_specs=pl.BlockSpec((T, bt, D), lambda i: (0, i, 0)),
        out_shape=jax.ShapeDtypeStruct((T, B, D), BF16),
        compiler_params=_params(("parallel",), 3 * R * bt * D * 4),
        name="pool_sample",
    )(ext_t)


def _softmax_rows(s):
    m = jnp.max(s, axis=-1, keepdims=True)
    p = jnp.exp(s - m)
    return p, jnp.sum(p, axis=-1, keepdims=True)


def _attn_prompt_kernel(q_ref, k_ref, v_ref, z_ref, o_ref):
    hd = q_ref.shape[1] // XATT_HEADS
    scale = hd ** -0.5
    for h in range(XATT_HEADS):
        lo, hi = h * hd, (h + 1) * hd
        s = lax.dot_general(q_ref[:, lo:hi], k_ref[:, lo:hi], (((1,), (1,)), ((), ())),
                            preferred_element_type=F32) * scale
        p, l = _softmax_rows(s)
        o = jnp.dot(p.astype(BF16), v_ref[:, lo:hi], preferred_element_type=F32) / l
        o_ref[:, lo:hi] = (o * _silu(z_ref[:, lo:hi].astype(F32))).astype(o_ref.dtype)


def attend_prompt(proj, k, v, q_off, z_off, tq=512):
    B, S, DX = k.shape
    T = proj.shape[0] // B
    tq = min(tq, T)
    nq = T // tq
    qb, zb = q_off // DX, z_off // DX
    return pl.pallas_call(
        _attn_prompt_kernel,
        grid=(B, nq),
        in_specs=[pl.BlockSpec((tq, DX), lambda b, i: (b * nq + i, qb)),
                  pl.BlockSpec((None, S, DX), lambda b, i: (b, 0, 0)),
                  pl.BlockSpec((None, S, DX), lambda b, i: (b, 0, 0)),
                  pl.BlockSpec((tq, DX), lambda b, i: (b * nq + i, zb))],
        out_specs=pl.BlockSpec((tq, DX), lambda b, i: (b * nq + i, 0)),
        out_shape=jax.ShapeDtypeStruct((B * T, DX), BF16),
        compiler_params=_params(("parallel", "parallel"), 8 * tq * DX * 4 + 8 * S * DX * 2),
        name="attend_prompt",
    )(proj, k, v, proj)


def _attn_sample_kernel(q_ref, k_ref, v_ref, z_ref, o_ref, *, steps):
    rows, dx = q_ref.shape
    hd = dx // XATT_HEADS
    scale = hd ** -0.5
    row_batch = lax.broadcasted_iota(jnp.int32, (rows, 1), 0) // steps
    for h in range(XATT_HEADS):
        lo, hi = h * hd, (h + 1) * hd
        q = q_ref[:, lo:hi]
        acc = jnp.zeros((rows, hd), F32)
        for j in range(rows // steps):
            kj = k_ref[j, :, lo:hi].astype(BF16)
            vj = v_ref[j, :, lo:hi].astype(BF16)
            s = lax.dot_general(q, kj, (((1,), (1,)), ((), ())), preferred_element_type=F32) * scale
            p, l = _softmax_rows(s)
            o = jnp.dot(p.astype(BF16), vj, preferred_element_type=F32) / l
            acc = jnp.where(row_batch == j, o, acc)
        o_ref[:, lo:hi] = (acc * _silu(z_ref[:, lo:hi].astype(F32))).astype(o_ref.dtype)


def attend_sample(proj, k, v, steps, q_off, z_off, bb=4):
    B, S, DX = k.shape
    rows = bb * steps
    qb, zb = q_off // DX, z_off // DX
    return pl.pallas_call(
        functools.partial(_attn_sample_kernel, steps=steps),
        grid=(B // bb,),
        in_specs=[pl.BlockSpec((rows, DX), lambda i: (i, qb)),
                  pl.BlockSpec((bb, S, DX), lambda i: (i, 0, 0)),
                  pl.BlockSpec((bb, S, DX), lambda i: (i, 0, 0)),
                  pl.BlockSpec((rows, DX), lambda i: (i, zb))],
        out_specs=pl.BlockSpec((rows, DX), lambda i: (i, 0)),
        out_shape=jax.ShapeDtypeStruct((B * steps, DX), BF16),
        compiler_params=_params(("parallel",), 4 * bb * S * DX * 4 + (8 << 20)),
        name="attend_sample",
    )(proj, k, v, proj)


def _head_output(h, o, z, g):
    return _sigmoid(o) * _rms(h, g) * _silu(z)


def _mlstm_prompt_kernel(q_ref, k_ref, v_ref, o_ref, z_ref, g_ref, gh_ref,
                         y_ref, c_out, n_out, m_out, ct_scr, n_scr, m_scr):
    c = pl.program_id(1)
    nh = MLSTM_HEADS
    L = q_ref.shape[0]
    dqk = q_ref.shape[1] // nh
    dv = v_ref.shape[1] // nh
    qscale = dqk ** -0.5

    @pl.when(c == 0)
    def _():
        ct_scr[...] = jnp.zeros_like(ct_scr)
        n_scr[...] = jnp.zeros_like(n_scr)
        m_scr[...] = jnp.zeros_like(m_scr)

    r_i = lax.broadcasted_iota(jnp.int32, (L, L), 0)
    c_i = lax.broadcasted_iota(jnp.int32, (L, L), 1)
    causal = c_i <= r_i
    eye = (c_i == r_i).astype(F32)
    tril = causal.astype(F32)
    triu = (r_i <= c_i).astype(F32)
    gates = g_ref[...]

    for h in range(nh):
        q = (q_ref[:, h * dqk:(h + 1) * dqk].astype(F32) * qscale)
        qb = q.astype(BF16)
        k = k_ref[:, h * dqk:(h + 1) * dqk]
        v = v_ref[:, h * dv:(h + 1) * dv]
        ig = gates[:, h:h + 1]
        lf = gates[:, nh + h:nh + h + 1]
        m_prev = m_scr[h:h + 1, 0:1]
        n_prev = n_scr[h:h + 1, :]
        ct = ct_scr[h]

        lf_row = jnp.sum(lf * eye, axis=0, keepdims=True)
        ig_row = jnp.sum(ig * eye, axis=0, keepdims=True)
        b_col = jnp.sum(tril * lf_row, axis=1, keepdims=True)
        b_row = jnp.sum(triu * lf, axis=0, keepdims=True)
        d = jnp.where(causal, b_col - b_row + ig_row, -jnp.inf)
        inter = b_col + m_prev
        m_t = jnp.maximum(inter, jnp.max(d, axis=1, keepdims=True))
        w = jnp.exp(d - m_t)
        w_inter = jnp.exp(inter - m_t)
        s = lax.dot_general(qb, k, (((1,), (1,)), ((), ())), preferred_element_type=F32) * w
        num = (jnp.dot(s.astype(BF16), v, preferred_element_type=F32)
               + w_inter * jnp.dot(qb, ct.astype(BF16), preferred_element_type=F32))
        den = (jnp.sum(s, axis=1, keepdims=True)
               + w_inter * jnp.sum(q * n_prev, axis=1, keepdims=True))
        hh = num / jnp.maximum(jnp.abs(den), jnp.exp(-m_t))

        b_last = b_col[L - 1:L, :]
        m_new = m_t[L - 1:L, :]
        w_state = jnp.exp(b_last - b_col + ig - m_new)
        decay = jnp.exp(b_last + m_prev - m_new)
        kw = k.astype(F32) * w_state
        ct_scr[h] = decay * ct + lax.dot_general(kw.astype(BF16), v, (((0,), (0,)), ((), ())),
                                                 preferred_element_type=F32)
        n_scr[h:h + 1, :] = decay * n_prev + jnp.sum(kw, axis=0, keepdims=True)
        m_scr[h:h + 1, :] = jnp.broadcast_to(m_new, (1, m_scr.shape[1]))

        lo, hi = h * dv, (h + 1) * dv
        y_ref[:, lo:hi] = _head_output(hh, o_ref[:, lo:hi].astype(F32), z_ref[:, lo:hi].astype(F32),
                                       gh_ref[:, lo:hi]).astype(y_ref.dtype)

    @pl.when(c == pl.num_programs(1) - 1)
    def _():
        for h in range(nh):
            c_out[h] = ct_scr[h].T
        n_out[...] = n_scr[...]
        m_out[...] = m_scr[...]


def mlstm_prompt(proj, gates, g_head, batch, dqk, dv):
    nh = MLSTM_HEADS
    T = proj.shape[0] // batch
    L = min(MLSTM_CHUNK, T)
    nc = T // L
    wq, wv = nh * dqk, nh * dv
    assert wv == 2 * wq
    rows = lambda b, c: b * nc + c
    return pl.pallas_call(
        _mlstm_prompt_kernel,
        grid=(batch, nc),
        in_specs=[pl.BlockSpec((L, wq), lambda b, c: (rows(b, c), 0)),
                  pl.BlockSpec((L, wq), lambda b, c: (rows(b, c), 1)),
                  pl.BlockSpec((L, wv), lambda b, c: (rows(b, c), 1)),
                  pl.BlockSpec((L, wv), lambda b, c: (rows(b, c), 2)),
                  pl.BlockSpec((L, wv), lambda b, c: (rows(b, c), 3)),
                  pl.BlockSpec((L, GATE_LANES), lambda b, c: (rows(b, c), 0)),
                  pl.BlockSpec((1, wv), lambda b, c: (0, 0))],
        out_specs=[pl.BlockSpec((L, wv), lambda b, c: (rows(b, c), 0)),
                   pl.BlockSpec((None, nh, dv, dqk), lambda b, c: (b, 0, 0, 0)),
                   pl.BlockSpec((None, nh, dqk), lambda b, c: (b, 0, 0)),
                   pl.BlockSpec((None, nh, GATE_LANES), lambda b, c: (b, 0, 0))],
        out_shape=[jax.ShapeDtypeStruct((batch * T, wv), BF16),
                   jax.ShapeDtypeStruct((batch, nh, dv, dqk), F32),
                   jax.ShapeDtypeStruct((batch, nh, dqk), F32),
                   jax.ShapeDtypeStruct((batch, nh, GATE_LANES), F32)],
        scratch_shapes=[pltpu.VMEM((nh, dqk, dv), F32),
                        pltpu.VMEM((nh, dqk), F32),
                        pltpu.VMEM((nh, GATE_LANES), F32)],
        compiler_params=_params(("parallel", "arbitrary"),
                                24 * L * wv + 5 * nh * dv * dqk * 4 + (16 << 20)),
        name="mlstm_prompt",
    )(proj, proj, proj, proj, proj, gates, g_head.reshape(1, wv))


def _lane_pick(x, lane, idx):
    return jnp.sum(jnp.where(lane == idx, x, 0.0), axis=1, keepdims=True)


def _mlstm_sample_kernel(q_ref, k_ref, v_ref, o_ref, z_ref, g_ref, gh_ref, c_ref, n_ref,
                         y_ref, c_out, n_out, m_out, *, steps):
    head = pl.program_id(1)
    nh = MLSTM_HEADS
    R, dqk = q_ref.shape
    nb = R // steps
    qscale = dqk ** -0.5

    gates = g_ref[...]
    lane = lax.broadcasted_iota(jnp.int32, gates.shape, 1)
    ig = _lane_pick(gates, lane, head)
    lf = _lane_pick(gates, lane, nh + head)
    m_prev = _lane_pick(gates, lane, 2 * nh + head)

    r_i = lax.broadcasted_iota(jnp.int32, (R, R), 0)
    c_i = lax.broadcasted_iota(jnp.int32, (R, R), 1)
    same = (r_i // steps) == (c_i // steps)
    causal = same & (c_i <= r_i)
    eye = (c_i == r_i).astype(F32)
    lower = causal.astype(F32)
    upper = (same & (r_i <= c_i)).astype(F32)
    last_of_row_batch = (c_i == (r_i // steps) * steps + (steps - 1)).astype(F32)
    row_batch = lax.broadcasted_iota(jnp.int32, (R, 1), 0) // steps

    q = q_ref[...].astype(F32) * qscale
    qb = q.astype(BF16)
    k = k_ref[...]
    v = v_ref[...]

    lf_row = jnp.sum(lf * eye, axis=0, keepdims=True)
    ig_row = jnp.sum(ig * eye, axis=0, keepdims=True)
    b_col = jnp.sum(lower * lf_row, axis=1, keepdims=True)
    b_row = jnp.sum(upper * lf, axis=0, keepdims=True)
    d = jnp.where(causal, b_col - b_row + ig_row, -jnp.inf)
    inter = b_col + m_prev
    m_t = jnp.maximum(inter, jnp.max(d, axis=1, keepdims=True))
    w = jnp.exp(d - m_t)
    w_inter = jnp.exp(inter - m_t)
    s = lax.dot_general(qb, k, (((1,), (1,)), ((), ())), preferred_element_type=F32) * w

    qc = jnp.zeros((R, v_ref.shape[1]), F32)
    n_rows = jnp.zeros((R, dqk), F32)
    for j in range(nb):
        mine = row_batch == j
        qcj = lax.dot_general(qb, c_ref[j].astype(BF16), (((1,), (1,)), ((), ())),
                              preferred_element_type=F32)
        qc = jnp.where(mine, qcj, qc)
        n_rows = jnp.where(mine, n_ref[j:j + 1, :], n_rows)

    num = jnp.dot(s.astype(BF16), v, preferred_element_type=F32) + w_inter * qc
    den = jnp.sum(s, axis=1, keepdims=True) + w_inter * jnp.sum(q * n_rows, axis=1, keepdims=True)
    hh = num / jnp.maximum(jnp.abs(den), jnp.exp(-m_t))
    y_ref[...] = _head_output(hh, o_ref[...].astype(F32), z_ref[...].astype(F32),
                              gh_ref[...]).astype(y_ref.dtype)

    m_row = jnp.sum(m_t * eye, axis=0, keepdims=True)
    b_last = jnp.sum(last_of_row_batch * b_row, axis=1, keepdims=True)
    m_new = jnp.sum(last_of_row_batch * m_row, axis=1, keepdims=True)
    w_state = jnp.exp(b_last - b_col + ig - m_new)
    decay = jnp.exp(b_last + m_prev - m_new)
    vw = v.astype(F32) * w_state
    kw = k.astype(F32) * w_state
    for j in range(nb):
        mine = row_batch == j
        r0 = j * steps
        dj = decay[r0:r0 + 1, :]
        upd = lax.dot_general(jnp.where(mine, vw, 0.0).astype(BF16), k, (((0,), (0,)), ((), ())),
                              preferred_element_type=F32)
        c_out[j] = dj * c_ref[j] + upd
        n_out[j:j + 1, :] = dj * n_ref[j:j + 1, :] + jnp.sum(jnp.where(mine, kw, 0.0), axis=0, keepdims=True)
        m_out[j:j + 1, :] = jnp.broadcast_to(m_new[r0:r0 + 1, :], (1, m_out.shape[1]))


def mlstm_sample(proj, gates, g_head, c0, n0, steps, dqk, dv):
    nh = MLSTM_HEADS
    B = c0.shape[0]
    nb = SAMPLE_BATCH_BLOCK
    R = nb * steps
    kq, kv, ko, kz = nh, nh * dqk // dv, 2 * nh * dqk // dv + nh, 2 * nh * dqk // dv + 2 * nh
    kv = (2 * nh * dqk) // dv
    ko = kv + nh
    kz = ko + nh
    return pl.pallas_call(
        functools.partial(_mlstm_sample_kernel, steps=steps),
        grid=(B // nb, nh),
        in_specs=[pl.BlockSpec((R, dqk), lambda i, h: (i, h)),
                  pl.BlockSpec((R, dqk), lambda i, h: (i, kq + h)),
                  pl.BlockSpec((R, dv), lambda i, h: (i, kv + h)),
                  pl.BlockSpec((R, dv), lambda i, h: (i, ko + h)),
                  pl.BlockSpec((R, dv), lambda i, h: (i, kz + h)),
                  pl.BlockSpec((R, GATE_LANES), lambda i, h: (i, 0)),
                  pl.BlockSpec((1, dv), lambda i, h: (0, h)),
                  pl.BlockSpec((nb, None, dv, dqk), lambda i, h: (i, h, 0, 0)),
                  pl.BlockSpec((None, nb, dqk), lambda i, h: (h, i, 0))],
        out_specs=[pl.BlockSpec((R, dv), lambda i, h: (i, h)),
                   pl.BlockSpec((nb, None, dv, dqk), lambda i, h: (i, h, 0, 0)),
                   pl.BlockSpec((None, nb, dqk), lambda i, h: (h, i, 0)),
                   pl.BlockSpec((None, nb, GATE_LANES), lambda i, h: (h, i, 0))],
        out_shape=[jax.ShapeDtypeStruct((B * steps, nh * dv), BF16),
                   jax.ShapeDtypeStruct(c0.shape, F32),
                   jax.ShapeDtypeStruct((nh, B, dqk), F32),
                   jax.ShapeDtypeStruct((nh, B, GATE_LANES), F32)],
        compiler_params=_params(("parallel", "parallel"), 6 * nb * dv * dqk * 4 + (8 << 20)),
        name="mlstm_sample",
    )(proj, proj, proj, proj, proj, gates, g_head.reshape(1, nh * dv), c0, n0)


def _finish(x, h_mix, h_att, w_out, g_post, g_next):
    y = matmul([h_mix, h_att], w_out, F32, tm=512, tn=1024)
    return post_norm_residual(x, y, g_post, g_next)


def _pool_weights(w_in, D):
    dx = D // 2
    u, xq, z = w_in[:, :D], w_in[:, D:D + dx], w_in[:, D + dx:]
    return jnp.concatenate([u, z[:, :D], xq, z[:, D:]], axis=1).astype(BF16)


def _mlstm_weights(w_in, D):
    nh = MLSTM_HEADS
    dx = D // 2
    wq = D // 2
    q, k = w_in[:, :wq], w_in[:, wq:2 * wq]
    v, o = w_in[:, 2 * wq:2 * wq + D], w_in[:, 2 * wq + D:2 * wq + 2 * D]
    g0 = 2 * wq + 2 * D
    gates = w_in[:, g0:g0 + 2 * nh]
    xq = w_in[:, g0 + 2 * nh:g0 + 2 * nh + dx]
    z = w_in[:, g0 + 2 * nh + dx:]
    main = jnp.concatenate([q, k, v, o, z[:, :D], xq, z[:, D:]], axis=1).astype(BF16)
    gates = jnp.pad(gates, ((0, 0), (0, GATE_LANES - 2 * nh))).astype(BF16)
    return main, gates


def kernel(x_prompt, x_sample, mem_prompt, state_pool, state_mlstm_C, state_mlstm_n, state_mlstm_m,
           cache_mem_k, cache_mem_v, norm_pre, norm_post, norm_mem, w_mem_kv, w_out,
           w_in_pool, w_pool_group, pool_scale, w_in_mlstm, b_igate, b_fgate, mlstm_head_norm):
    B, T, D = x_prompt.shape
    Bs, Ts, _ = x_sample.shape
    S = mem_prompt.shape[1]
    depth = w_out.shape[0]
    nh = MLSTM_HEADS
    dx = D // 2
    dv = D // nh
    dqk = dv // 2
    hd = dx // XATT_HEADS

    xp = x_prompt.reshape(B * T, D)
    xs = x_sample.reshape(Bs * Ts, D)
    mem = mem_prompt.reshape(B * S, D)
    hp = rmsnorm_cast(xp, norm_pre[0])
    hs = rmsnorm_cast(xs, norm_pre[0])

    mem_k_l, mem_v_l = [], []
    pool_p_l, pool_s_l = [], []
    cp_l, np_l, mp_l, cs_l, ns_l, ms_l = [], [], [], [], [], []
    for layer in range(depth):
        j = layer // 2
        g_next = norm_pre[layer + 1] if layer + 1 < depth else jnp.ones((D,), F32)
        kv = matmul([rmsnorm_cast(mem, norm_mem[layer])], w_mem_kv[layer].astype(BF16), F32)
        mk, mv = kv[:, :dx].reshape(B, S, dx), kv[:, dx:].reshape(B, S, dx)
        mem_k_l.append(mk.reshape(B, S, XATT_HEADS, hd))
        mem_v_l.append(mv.reshape(B, S, XATT_HEADS, hd))
        mk16, mv16 = mk.astype(BF16), mv.astype(BF16)
        ck = cache_mem_k[layer].reshape(Bs, S, dx)
        cv = cache_mem_v[layer].reshape(Bs, S, dx)
        w_o = w_out[layer].astype(BF16)
        if layer % 2 == 0:
            w_in = _pool_weights(w_in_pool[j], D)
            w_g = w_pool_group[j].astype(BF16)
            z_mix, q_off, z_att = D, 2 * D, 2 * D + dx
            proj = matmul([hp], w_in, BF16)
            pooled = pool_prompt(proj, jnp.zeros((B, POOL_HIST, D), F32), B, 0)
            mix = group_mix_gate(pooled, w_g, pool_scale[j], proj, z_mix)
            att = attend_prompt(proj, mk16, mv16, q_off, z_att)
            xp, hp = _finish(xp, mix, att, w_o, norm_post[layer], g_next)
            pool_p_l.append(proj[:, :D].reshape(B, T, D)[:, T - POOL_HIST:].astype(F32))
            proj = matmul([hs], w_in, BF16)
            u = proj[:, :D].astype(F32).reshape(Bs, Ts, D)
            ext = jnp.concatenate([state_pool[j], u], axis=1)
            pooled = pool_sample(ext.transpose(1, 0, 2), PAST_LEN)
            pooled = pooled.transpose(1, 0, 2).reshape(Bs * Ts, D)
            mix = group_mix_gate(pooled, w_g, pool_scale[j], proj, z_mix)
            att = attend_sample(proj, ck, cv, Ts, q_off, z_att)
            xs, hs = _finish(xs, mix, att, w_o, norm_post[layer], g_next)
            pool_s_l.append(ext[:, Ts:])
        else:
            w_in, w_gates = _mlstm_weights(w_in_mlstm[j], D)
            bias = jnp.pad(jnp.concatenate([b_igate[j], b_fgate[j]]), (0, GATE_LANES - 2 * nh))
            bias = bias.reshape(1, GATE_LANES)
            q_off, z_att = 3 * D + D, 3 * D + D + dx
            proj = matmul([hp], w_in, BF16)
            gates = mlstm_gates(hp, w_gates, bias)
            mix, c1, n1, m1 = mlstm_prompt(proj, gates, mlstm_head_norm[j], B, dqk, dv)
            att = attend_prompt(proj, mk16, mv16, q_off, z_att)
            xp, hp = _finish(xp, mix, att, w_o, norm_post[layer], g_next)
            cp_l.append(c1); np_l.append(n1); mp_l.append(m1[:, :, 0])
            proj = matmul([hs], w_in, BF16)
            gates = mlstm_gates(hs, w_gates, bias)
            m_rows = jnp.repeat(state_mlstm_m[j], Ts, axis=0)
            gates = jnp.concatenate([gates[:, :2 * nh], m_rows,
                                     jnp.zeros((Bs * Ts, GATE_LANES - 3 * nh), F32)], axis=1)
            mix, c1, n1, m1 = mlstm_sample(proj, gates, mlstm_head_norm[j], state_mlstm_C[j],
                                           state_mlstm_n[j].transpose(1, 0, 2), Ts, dqk, dv)
            att = attend_sample(proj, ck, cv, Ts, q_off, z_att)
            xs, hs = _finish(xs, mix, att, w_o, norm_post[layer], g_next)
            cs_l.append(c1); ns_l.append(n1.transpose(1, 0, 2)); ms_l.append(m1[:, :, 0].T)

    return (xp.reshape(B, T, D), xs.reshape(Bs, Ts, D), jnp.stack(pool_p_l),
            jnp.stack(mem_k_l), jnp.stack(mem_v_l),
            jnp.stack(cp_l), jnp.stack(np_l), jnp.stack(mp_l),
            jnp.stack(pool_s_l), jnp.stack(cs_l), jnp.stack(ns_l), jnp.stack(ms_l))
```

```python
import functools

import jax
import jax.numpy as jnp
from jax import lax
from jax.experimental import pallas as pl
from jax.experimental.pallas import tpu as pltpu

F32 = jnp.float32
BF16 = jnp.bfloat16

NORM_EPS = 1e-6
GATE_SOFTCAP = 15.0
POOL_WINDOWS = (2, 4, 8, 16)
POOL_HIST = max(POOL_WINDOWS) - 1
HALO = POOL_HIST + 1
XATT_HEADS = 4
MLSTM_HEADS = 8
PAST_LEN = 16384
GATE_LANES = 128
MLSTM_CHUNK = 256
SAMPLE_BATCH_BLOCK = 8

V7X_VMEM_BYTES = 64 * 1024 * 1024
VMEM_CAP = V7X_VMEM_BYTES - 8 * 1024 * 1024


def _params(semantics, vmem_bytes):
    limit = int(min(max(vmem_bytes, 16 * 1024 * 1024), VMEM_CAP))
    return pltpu.CompilerParams(dimension_semantics=semantics, vmem_limit_bytes=limit)


def _sigmoid(x):
    return 1.0 / (1.0 + jnp.exp(-x))


def _silu(x):
    return x * _sigmoid(x)


def _rms(x, g):
    r = lax.rsqrt(jnp.mean(x * x, axis=-1, keepdims=True) + NORM_EPS)
    return x * r * g


def _rmsnorm_kernel(x_ref, g_ref, o_ref):
    o_ref[...] = _rms(x_ref[...], g_ref[...]).astype(o_ref.dtype)


def rmsnorm_cast(x, g, tm=256):
    M, D = x.shape
    tm = min(tm, M)
    return pl.pallas_call(
        _rmsnorm_kernel,
        grid=(M // tm,),
        in_specs=[pl.BlockSpec((tm, D), lambda i: (i, 0)),
                  pl.BlockSpec((1, D), lambda i: (0, 0))],
        out_specs=pl.BlockSpec((tm, D), lambda i: (i, 0)),
        out_shape=jax.ShapeDtypeStruct((M, D), BF16),
        compiler_params=_params(("parallel",), 8 * tm * D * 4),
        name="rmsnorm_cast",
    )(x, g.reshape(1, D))


def _post_kernel(x_ref, y_ref, g_ref, gn_ref, o_ref, h_ref):
    out = x_ref[...] + _rms(y_ref[...], g_ref[...])
    o_ref[...] = out
    h_ref[...] = _rms(out, gn_ref[...]).astype(h_ref.dtype)


def post_norm_residual(x, y, g_post, g_next, tm=256):
    M, D = x.shape
    tm = min(tm, M)
    row = pl.BlockSpec((tm, D), lambda i: (i, 0))
    vec = pl.BlockSpec((1, D), lambda i: (0, 0))
    return pl.pallas_call(
        _post_kernel,
        grid=(M // tm,),
        in_specs=[row, row, vec, vec],
        out_specs=[row, row],
        out_shape=[jax.ShapeDtypeStruct((M, D), F32), jax.ShapeDtypeStruct((M, D), BF16)],
        compiler_params=_params(("parallel",), 12 * tm * D * 4),
        name="post_norm_residual",
    )(x, y, g_post.reshape(1, D), g_next.reshape(1, D))


def _mm_kernel(*refs, ksizes):
    n_a = len(ksizes)
    a_refs, w_ref, o_ref = refs[:n_a], refs[n_a], refs[n_a + 1]
    acc = None
    off = 0
    for a_ref, ks in zip(a_refs, ksizes):
        part = jnp.dot(a_ref[...], w_ref[off:off + ks, :], preferred_element_type=F32)
        acc = part if acc is None else acc + part
        off += ks
    o_ref[...] = acc.astype(o_ref.dtype)


def matmul(a_list, w, out_dtype, layer=0, tm=1024, tn=1024):
    M = a_list[0].shape[0]
    _, K, N = w.shape
    ksizes = tuple(a.shape[1] for a in a_list)
    assert sum(ksizes) == K
    tm, tn = min(tm, M), min(tn, N)
    osz = jnp.dtype(out_dtype).itemsize
    vmem = 2 * (tm * K * 2 + K * tn * 2 + tm * tn * osz) + 2 * tm * tn * 4
    return pl.pallas_call(
        functools.partial(_mm_kernel, ksizes=ksizes),
        grid=(N // tn, M // tm),
        in_specs=[pl.BlockSpec((tm, ks), lambda j, i: (i, 0)) for ks in ksizes]
        + [pl.BlockSpec((None, K, tn), lambda j, i: (layer, 0, j))],
        out_specs=pl.BlockSpec((tm, tn), lambda j, i: (i, j)),
        out_shape=jax.ShapeDtypeStruct((M, N), out_dtype),
        compiler_params=_params(("parallel", "parallel"), vmem + (4 << 20)),
        name="matmul",
    )(*a_list, w)


def _mem_kv_kernel(x_ref, g_ref, w_ref, k_ref, v_ref, h_scr):
    @pl.when(pl.program_id(2) == 0)
    def _():
        h_scr[...] = _rms(x_ref[...], g_ref[...]).astype(h_scr.dtype)

    kv = jnp.dot(h_scr[...], w_ref[...], preferred_element_type=F32)
    half = pl.num_programs(2) // 2

    @pl.when(pl.program_id(2) < half)
    def _():
        k_ref[...] = kv

    @pl.when(pl.program_id(2) >= half)
    def _():
        v_ref[...] = kv


def mem_kv(mem, g_mem, w_kv, tm=512, tn=512):
    M, D = mem.shape
    depth, _, N = w_kv.shape
    dx = N // 2
    tm, tn = min(tm, M), min(tn, dx)
    nj = N // tn
    half = nj // 2
    vmem = 2 * (tm * D * 4 + D * tn * 2 + 2 * tm * tn * 4) + tm * D * 2 + 2 * tm * tn * 4 + 2 * tm * D * 4
    out = jax.ShapeDtypeStruct((depth, M, dx), F32)
    return pl.pallas_call(
        _mem_kv_kernel,
        grid=(depth, M // tm, nj),
        in_specs=[pl.BlockSpec((tm, D), lambda l, i, j: (i, 0)),
                  pl.BlockSpec((None, 1, D), lambda l, i, j: (l, 0, 0)),
                  pl.BlockSpec((None, D, tn), lambda l, i, j: (l, 0, j))],
        out_specs=[pl.BlockSpec((None, tm, tn), lambda l, i, j: (l, i, jnp.minimum(j, half - 1))),
                   pl.BlockSpec((None, tm, tn), lambda l, i, j: (l, i, jnp.maximum(j - half, 0)))],
        out_shape=[out, out],
        scratch_shapes=[pltpu.VMEM((tm, D), BF16)],
        compiler_params=_params(("parallel", "parallel", "arbitrary"), vmem + (4 << 20)),
        name="mem_kv",
    )(mem, g_mem.reshape(depth, 1, D), w_kv)


def _gates_kernel(h_ref, w_ref, b_ref, o_ref):
    nh = MLSTM_HEADS
    pre = jnp.dot(h_ref[...], w_ref[...], preferred_element_type=F32) + b_ref[...]
    capped = GATE_SOFTCAP * jnp.tanh(pre / GATE_SOFTCAP)
    log_sig = jnp.minimum(capped, 0.0) - jnp.log(1.0 + jnp.exp(-jnp.abs(capped)))
    lane = lax.broadcasted_iota(jnp.int32, capped.shape, 1)
    o_ref[...] = jnp.where(lane < nh, capped, log_sig)


def mlstm_gates(h, w_gates, bias, tm=512):
    M, D = h.shape
    tm = min(tm, M)
    return pl.pallas_call(
        _gates_kernel,
        grid=(M // tm,),
        in_specs=[pl.BlockSpec((tm, D), lambda i: (i, 0)),
                  pl.BlockSpec((D, GATE_LANES), lambda i: (0, 0)),
                  pl.BlockSpec((1, GATE_LANES), lambda i: (0, 0))],
        out_specs=pl.BlockSpec((tm, GATE_LANES), lambda i: (i, 0)),
        out_shape=jax.ShapeDtypeStruct((M, GATE_LANES), F32),
        compiler_params=_params(("parallel",), 4 * tm * D * 2 + 4 * D * GATE_LANES * 2),
        name="mlstm_gates",
    )(h, w_gates, bias)


def _group_mm_kernel(a_ref, w_ref, s_ref, z_ref, o_ref):
    mixed = jnp.dot(a_ref[...], w_ref[...], preferred_element_type=F32) * s_ref[...]
    o_ref[...] = (mixed * _silu(z_ref[...].astype(F32))).astype(o_ref.dtype)


def group_mix_gate(pooled, w_group, scale, proj, z_off, tm=512):
    M, D = pooled.shape
    G, Gs, _ = w_group.shape
    tm = min(tm, M)
    zb = z_off // Gs
    return pl.pallas_call(
        _group_mm_kernel,
        grid=(G, M // tm),
        in_specs=[pl.BlockSpec((tm, Gs), lambda g, i: (i, g)),
                  pl.BlockSpec((None, Gs, Gs), lambda g, i: (g, 0, 0)),
                  pl.BlockSpec((1, Gs), lambda g, i: (0, g)),
                  pl.BlockSpec((tm, Gs), lambda g, i: (i, zb + g))],
        out_specs=pl.BlockSpec((tm, Gs), lambda g, i: (i, g)),
        out_shape=jax.ShapeDtypeStruct((M, D), BF16),
        compiler_params=_params(("parallel", "parallel"), 6 * tm * Gs * 4 + 4 * Gs * Gs * 2),
        name="group_mix_gate",
    )(pooled, w_group, scale.reshape(1, D), proj)


def _window_sums(ext, levels):
    sums = []
    cur = ext
    for lv in range(levels):
        cur = cur + pltpu.roll(cur, 1 << lv, 0)
        sums.append(cur)
    return sums


def _pool_prompt_kernel(u_ref, halo_ref, hist_ref, o_ref, *, pos0, tt):
    i = pl.program_id(1)
    gs = u_ref.shape[1] // len(POOL_WINDOWS)
    row = lax.broadcasted_iota(jnp.int32, (tt, 1), 0) + i * tt
    pos1 = (row + (pos0 + 1)).astype(F32)
    first = i == 0
    for g, w in enumerate(POOL_WINDOWS):
        lo, hi = g * gs, (g + 1) * gs
        u = u_ref[:, lo:hi].astype(F32)
        halo = jnp.where(first, hist_ref[:, lo:hi], halo_ref[:, lo:hi].astype(F32))
        ext = jnp.concatenate([halo, u], axis=0)
        win = _window_sums(ext, g + 1)[-1][HALO:, :]
        cnt = jnp.minimum(float(w), pos1)
        o_ref[:, lo:hi] = (win / cnt - u).astype(o_ref.dtype)


def pool_prompt(proj, hist, batch, pos0, tt=256):
    D = hist.shape[-1]
    T = proj.shape[0] // batch
    tt = min(tt, T)
    nt = T // tt
    hist16 = jnp.concatenate([jnp.zeros((batch, 1, D), F32), hist.astype(F32)], axis=1)
    return pl.pallas_call(
        functools.partial(_pool_prompt_kernel, pos0=pos0, tt=tt),
        grid=(batch, nt),
        in_specs=[pl.BlockSpec((tt, D), lambda b, i: (b * nt + i, 0)),
                  pl.BlockSpec((HALO, D),
                               lambda b, i: (jnp.maximum((b * nt + i) * (tt // HALO) - 1, 0), 0)),
                  pl.BlockSpec((None, HALO, D), lambda b, i: (b, 0, 0))],
        out_specs=pl.BlockSpec((tt, D), lambda b, i: (b * nt + i, 0)),
        out_shape=jax.ShapeDtypeStruct((batch * T, D), BF16),
        compiler_params=_params(("parallel", "parallel"), 16 * (tt + HALO) * D * 4),
        name="pool_prompt",
    )(proj, proj, hist16)


def _pool_sample_kernel(ext_ref, o_ref, *, pos0):
    steps = o_ref.shape[0]
    gs = ext_ref.shape[2] // len(POOL_WINDOWS)
    for g, w in enumerate(POOL_WINDOWS):
        lo, hi = g * gs, (g + 1) * gs
        for t in range(steps):
            cur = ext_ref[POOL_HIST + t, :, lo:hi]
            win = cur
            for r in range(1, w):
                win = win + ext_ref[POOL_HIST + t - r, :, lo:hi]
            cnt = float(min(w, pos0 + t + 1))
            o_ref[t, :, lo:hi] = (win / cnt - cur).astype(o_ref.dtype)


def pool_sample(ext_t, pos0, bt=32):
    R, B, D = ext_t.shape
    T = R - POOL_HIST
    bt = min(bt, B)
    return pl.pallas_call(
        functools.partial(_pool_sample_kernel, pos0=pos0),
        grid=(B // bt,),
        in_specs=[pl.BlockSpec((R, bt, D), lambda i: (0, i, 0))],
        out_specs=pl.BlockSpec((T, bt, D), lambda i: (0, i, 0)),
        out_shape=jax.ShapeDtypeStruct((T, B, D), BF16),
        compiler_params=_params(("parallel",), 3 * R * bt * D * 4),
        name="pool_sample",
    )(ext_t)


def _softmax_rows(s):
    m = jnp.max(s, axis=-1, keepdims=True)
    p = jnp.exp(s - m)
    return p, jnp.sum(p, axis=-1, keepdims=True)


def _attn_prompt_kernel(q_ref, k_ref, v_ref, z_ref, o_ref):
    hd = q_ref.shape[1] // XATT_HEADS
    scale = hd ** -0.5
    for h in range(XATT_HEADS):
        lo, hi = h * hd, (h + 1) * hd
        s = lax.dot_general(q_ref[:, lo:hi], k_ref[:, lo:hi], (((1,), (1,)), ((), ())),
                            preferred_element_type=F32) * scale
        p, l = _softmax_rows(s)
        o = jnp.dot(p.astype(BF16), v_ref[:, lo:hi], preferred_element_type=F32) / l
        o_ref[:, lo:hi] = (o * _silu(z_ref[:, lo:hi].astype(F32))).astype(o_ref.dtype)


def attend_prompt(q_arr, qb, z_arr, zb, k, v, tq=512):
    B, S, DX = k.shape
    T = q_arr.shape[0] // B
    tq = min(tq, T)
    nq = T // tq
    return pl.pallas_call(
        _attn_prompt_kernel,
        grid=(B, nq),
        in_specs=[pl.BlockSpec((tq, DX), lambda b, i: (b * nq + i, qb)),
                  pl.BlockSpec((None, S, DX), lambda b, i: (b, 0, 0)),
                  pl.BlockSpec((None, S, DX), lambda b, i: (b, 0, 0)),
                  pl.BlockSpec((tq, DX), lambda b, i: (b * nq + i, zb))],
        out_specs=pl.BlockSpec((tq, DX), lambda b, i: (b * nq + i, 0)),
        out_shape=jax.ShapeDtypeStruct((B * T, DX), BF16),
        compiler_params=_params(("parallel", "parallel"), 8 * tq * DX * 4 + 8 * S * DX * 2),
        name="attend_prompt",
    )(q_arr, k, v, z_arr)


def _attn_sample_kernel(q_ref, k_hbm, v_hbm, z_ref, o_ref, kbuf, vbuf, sem, *, steps, layer, bb):
    i = pl.program_id(0)
    rows, dx = q_ref.shape
    hd = dx // XATT_HEADS
    scale = hd ** -0.5

    def head_copies(step, slot):
        out = []
        for t, (src, dst) in enumerate(((k_hbm, kbuf), (v_hbm, vbuf))):
            for h in range(XATT_HEADS):
                out.append(pltpu.make_async_copy(src.at[layer, pl.ds(step * bb, bb), :, h, :],
                                                 dst.at[slot, h], sem.at[t, slot, h]))
        return out

    @pl.when(i == 0)
    def _():
        for cp in head_copies(0, 0):
            cp.start()

    @pl.when(i + 1 < pl.num_programs(0))
    def _():
        for cp in head_copies(i + 1, (i + 1) % 2):
            cp.start()

    slot = i % 2
    for cp in head_copies(i, slot):
        cp.wait()

    row_batch = lax.broadcasted_iota(jnp.int32, (rows, 1), 0) // steps
    for h in range(XATT_HEADS):
        lo, hi = h * hd, (h + 1) * hd
        q = q_ref[:, lo:hi]
        acc = jnp.zeros((rows, hd), F32)
        for j in range(bb):
            kj = kbuf[slot, h, j].astype(BF16)
            vj = vbuf[slot, h, j].astype(BF16)
            s = lax.dot_general(q, kj, (((1,), (1,)), ((), ())), preferred_element_type=F32) * scale
            p, l = _softmax_rows(s)
            o = jnp.dot(p.astype(BF16), vj, preferred_element_type=F32) / l
            acc = jnp.where(row_batch == j, o, acc)
        o_ref[:, lo:hi] = (acc * _silu(z_ref[:, lo:hi].astype(F32))).astype(o_ref.dtype)


def attend_sample(q_arr, qb, z_arr, zb, k, v, layer, steps, bb=4):
    _, B, S, nh, hd = k.shape
    assert nh == XATT_HEADS
    DX = nh * hd
    rows = bb * steps
    slot_bytes = nh * bb * S * hd * 4
    return pl.pallas_call(
        functools.partial(_attn_sample_kernel, steps=steps, layer=layer, bb=bb),
        grid=(B // bb,),
        in_specs=[pl.BlockSpec((rows, DX), lambda i: (i, qb)),
                  pl.BlockSpec(memory_space=pl.ANY), pl.BlockSpec(memory_space=pl.ANY),
                  pl.BlockSpec((rows, DX), lambda i: (i, zb))],
        out_specs=pl.BlockSpec((rows, DX), lambda i: (i, 0)),
        out_shape=jax.ShapeDtypeStruct((B * steps, DX), BF16),
        scratch_shapes=[pltpu.VMEM((2, nh, bb, S, hd), F32), pltpu.VMEM((2, nh, bb, S, hd), F32),
                        pltpu.SemaphoreType.DMA((2, 2, nh))],
        compiler_params=_params(("arbitrary",), 4 * slot_bytes + (12 << 20)),
        name="attend_sample",
    )(q_arr, k, v, z_arr)


def _head_output(h, o, z, g):
    return _sigmoid(o) * _rms(h, g) * _silu(z)


def _mlstm_prompt_kernel(q_ref, k_ref, v_ref, o_ref, zlo_ref, zhi_ref, g_ref, gh_ref,
                         y_ref, c_out, n_out, m_out, ct_scr, n_scr, m_scr):
    c = pl.program_id(1)
    nh = MLSTM_HEADS
    L = q_ref.shape[0]
    dqk = q_ref.shape[1] // nh
    dv = v_ref.shape[1] // nh
    qscale = dqk ** -0.5

    @pl.when(c == 0)
    def _():
        ct_scr[...] = jnp.zeros_like(ct_scr)
        n_scr[...] = jnp.zeros_like(n_scr)
        m_scr[...] = jnp.zeros_like(m_scr)

    r_i = lax.broadcasted_iota(jnp.int32, (L, L), 0)
    c_i = lax.broadcasted_iota(jnp.int32, (L, L), 1)
    causal = c_i <= r_i
    eye = (c_i == r_i).astype(F32)
    tril = causal.astype(F32)
    triu = (r_i <= c_i).astype(F32)
    gates = g_ref[...]

    for h in range(nh):
        q = (q_ref[:, h * dqk:(h + 1) * dqk].astype(F32) * qscale)
        qb = q.astype(BF16)
        k = k_ref[:, h * dqk:(h + 1) * dqk]
        v = v_ref[:, h * dv:(h + 1) * dv]
        ig = gates[:, h:h + 1]
        lf = gates[:, nh + h:nh + h + 1]
        m_prev = m_scr[h:h + 1, 0:1]
        n_prev = n_scr[h:h + 1, :]
        ct = ct_scr[h]

        lf_row = jnp.sum(lf * eye, axis=0, keepdims=True)
        ig_row = jnp.sum(ig * eye, axis=0, keepdims=True)
        b_col = jnp.sum(tril * lf_row, axis=1, keepdims=True)
        b_row = jnp.sum(triu * lf, axis=0, keepdims=True)
        d = jnp.where(causal, b_col - b_row + ig_row, -jnp.inf)
        inter = b_col + m_prev
        m_t = jnp.maximum(inter, jnp.max(d, axis=1, keepdims=True))
        w = jnp.exp(d - m_t)
        w_inter = jnp.exp(inter - m_t)
        s = lax.dot_general(qb, k, (((1,), (1,)), ((), ())), preferred_element_type=F32) * w
        num = (jnp.dot(s.astype(BF16), v, preferred_element_type=F32)
               + w_inter * jnp.dot(qb, ct.astype(BF16), preferred_element_type=F32))
        den = (jnp.sum(s, axis=1, keepdims=True)
               + w_inter * jnp.sum(q * n_prev, axis=1, keepdims=True))
        hh = num / jnp.maximum(jnp.abs(den), jnp.exp(-m_t))

        b_last = b_col[L - 1:L, :]
        m_new = m_t[L - 1:L, :]
        w_state = jnp.exp(b_last - b_col + ig - m_new)
        decay = jnp.exp(b_last + m_prev - m_new)
        kw = k.astype(F32) * w_state
        ct_scr[h] = decay * ct + lax.dot_general(kw.astype(BF16), v, (((0,), (0,)), ((), ())),
                                                 preferred_element_type=F32)
        n_scr[h:h + 1, :] = decay * n_prev + jnp.sum(kw, axis=0, keepdims=True)
        m_scr[h:h + 1, :] = jnp.broadcast_to(m_new, (1, m_scr.shape[1]))

        lo, hi = h * dv, (h + 1) * dv
        z_ref, zh = (zlo_ref, h) if h < nh // 2 else (zhi_ref, h - nh // 2)
        y_ref[:, lo:hi] = _head_output(hh, o_ref[:, lo:hi].astype(F32),
                                       z_ref[:, zh * dv:(zh + 1) * dv].astype(F32),
                                       gh_ref[:, lo:hi]).astype(y_ref.dtype)

    @pl.when(c == pl.num_programs(1) - 1)
    def _():
        for h in range(nh):
            c_out[h] = ct_scr[h].T
        n_out[...] = n_scr[...]
        m_out[...] = m_scr[...]


def mlstm_prompt(proj, proj_z, gates, g_head, batch, dqk, dv):
    nh = MLSTM_HEADS
    T = proj.shape[0] // batch
    L = min(MLSTM_CHUNK, T)
    nc = T // L
    wq, wv = nh * dqk, nh * dv
    assert wv == 2 * wq
    rows = lambda b, c: b * nc + c
    return pl.pallas_call(
        _mlstm_prompt_kernel,
        grid=(batch, nc),
        in_specs=[pl.BlockSpec((L, wq), lambda b, c: (rows(b, c), 0)),
                  pl.BlockSpec((L, wq), lambda b, c: (rows(b, c), 1)),
                  pl.BlockSpec((L, wv), lambda b, c: (rows(b, c), 1)),
                  pl.BlockSpec((L, wv), lambda b, c: (rows(b, c), 2)),
                  pl.BlockSpec((L, wq), lambda b, c: (rows(b, c), 1)),
                  pl.BlockSpec((L, wq), lambda b, c: (rows(b, c), 2)),
                  pl.BlockSpec((L, GATE_LANES), lambda b, c: (rows(b, c), 0)),
                  pl.BlockSpec((1, wv), lambda b, c: (0, 0))],
        out_specs=[pl.BlockSpec((L, wv), lambda b, c: (rows(b, c), 0)),
                   pl.BlockSpec((None, nh, dv, dqk), lambda b, c: (b, 0, 0, 0)),
                   pl.BlockSpec((None, nh, dqk), lambda b, c: (b, 0, 0)),
                   pl.BlockSpec((None, nh, GATE_LANES), lambda b, c: (b, 0, 0))],
        out_shape=[jax.ShapeDtypeStruct((batch * T, wv), BF16),
                   jax.ShapeDtypeStruct((batch, nh, dv, dqk), F32),
                   jax.ShapeDtypeStruct((batch, nh, dqk), F32),
                   jax.ShapeDtypeStruct((batch, nh, GATE_LANES), F32)],
        scratch_shapes=[pltpu.VMEM((nh, dqk, dv), F32),
                        pltpu.VMEM((nh, dqk), F32),
                        pltpu.VMEM((nh, GATE_LANES), F32)],
        compiler_params=_params(("parallel", "arbitrary"),
                                24 * L * wv + 5 * nh * dv * dqk * 4 + (16 << 20)),
        name="mlstm_prompt",
    )(proj, proj, proj, proj, proj_z, proj_z, gates, g_head.reshape(1, wv))


def _lane_pick(x, lane, idx):
    return jnp.sum(jnp.where(lane == idx, x, 0.0), axis=1, keepdims=True)


def _mlstm_sample_kernel(q_ref, k_ref, v_ref, o_ref, z_ref, g_ref, gh_ref, c_ref, n_ref,
                         y_ref, c_out, n_out, m_out, *, steps):
    head = pl.program_id(1)
    nh = MLSTM_HEADS
    R, dqk = q_ref.shape
    nb = R // steps
    qscale = dqk ** -0.5

    gates = g_ref[...]
    lane = lax.broadcasted_iota(jnp.int32, gates.shape, 1)
    ig = _lane_pick(gates, lane, head)
    lf = _lane_pick(gates, lane, nh + head)
    m_prev = _lane_pick(gates, lane, 2 * nh + head)

    r_i = lax.broadcasted_iota(jnp.int32, (R, R), 0)
    c_i = lax.broadcasted_iota(jnp.int32, (R, R), 1)
    same = (r_i // steps) == (c_i // steps)
    causal = same & (c_i <= r_i)
    eye = (c_i == r_i).astype(F32)
    lower = causal.astype(F32)
    upper = (same & (r_i <= c_i)).astype(F32)
    last_of_row_batch = (c_i == (r_i // steps) * steps + (steps - 1)).astype(F32)
    row_batch = lax.broadcasted_iota(jnp.int32, (R, 1), 0) // steps

    q = q_ref[...].astype(F32) * qscale
    qb = q.astype(BF16)
    k = k_ref[...]
    v = v_ref[...]

    lf_row = jnp.sum(lf * eye, axis=0, keepdims=True)
    ig_row = jnp.sum(ig * eye, axis=0, keepdims=True)
    b_col = jnp.sum(lower * lf_row, axis=1, keepdims=True)
    b_row = jnp.sum(upper * lf, axis=0, keepdims=True)
    d = jnp.where(causal, b_col - b_row + ig_row, -jnp.inf)
    inter = b_col + m_prev
    m_t = jnp.maximum(inter, jnp.max(d, axis=1, keepdims=True))
    w = jnp.exp(d - m_t)
    w_inter = jnp.exp(inter - m_t)
    s = lax.dot_general(qb, k, (((1,), (1,)), ((), ())), preferred_element_type=F32) * w

    qc = jnp.zeros((R, v_ref.shape[1]), F32)
    n_rows = jnp.zeros((R, dqk), F32)
    for j in range(nb):
        mine = row_batch == j
        qcj = lax.dot_general(qb, c_ref[j].astype(BF16), (((1,), (1,)), ((), ())),
                              preferred_element_type=F32)
        qc = jnp.where(mine, qcj, qc)
        n_rows = jnp.where(mine, n_ref[j:j + 1, :], n_rows)

    num = jnp.dot(s.astype(BF16), v, preferred_element_type=F32) + w_inter * qc
    den = jnp.sum(s, axis=1, keepdims=True) + w_inter * jnp.sum(q * n_rows, axis=1, keepdims=True)
    hh = num / jnp.maximum(jnp.abs(den), jnp.exp(-m_t))
    y_ref[...] = _head_output(hh, o_ref[...].astype(F32), z_ref[...].astype(F32),
                              gh_ref[...]).astype(y_ref.dtype)

    m_row = jnp.sum(m_t * eye, axis=0, keepdims=True)
    b_last = jnp.sum(last_of_row_batch * b_row, axis=1, keepdims=True)
    m_new = jnp.sum(last_of_row_batch * m_row, axis=1, keepdims=True)
    w_state = jnp.exp(b_last - b_col + ig - m_new)
    decay = jnp.exp(b_last + m_prev - m_new)
    vw = v.astype(F32) * w_state
    kw = k.astype(F32) * w_state
    for j in range(nb):
        mine = row_batch == j
        r0 = j * steps
        dj = decay[r0:r0 + 1, :]
        upd = lax.dot_general(jnp.where(mine, vw, 0.0).astype(BF16), k, (((0,), (0,)), ((), ())),
                              preferred_element_type=F32)
        c_out[j] = dj * c_ref[j] + upd
        n_out[j:j + 1, :] = dj * n_ref[j:j + 1, :] + jnp.sum(jnp.where(mine, kw, 0.0), axis=0, keepdims=True)
        m_out[j:j + 1, :] = jnp.broadcast_to(m_new[r0:r0 + 1, :], (1, m_out.shape[1]))


def mlstm_sample(proj, proj_z, gates, g_head, c0, n0, steps, dqk, dv):
    nh = MLSTM_HEADS
    B = c0.shape[0]
    nb = SAMPLE_BATCH_BLOCK
    R = nb * steps
    kq = nh
    kv = (2 * nh * dqk) // dv
    ko = kv + nh
    kz = (nh * dqk) // dv
    return pl.pallas_call(
        functools.partial(_mlstm_sample_kernel, steps=steps),
        grid=(B // nb, nh),
        in_specs=[pl.BlockSpec((R, dqk), lambda i, h: (i, h)),
                  pl.BlockSpec((R, dqk), lambda i, h: (i, kq + h)),
                  pl.BlockSpec((R, dv), lambda i, h: (i, kv + h)),
                  pl.BlockSpec((R, dv), lambda i, h: (i, ko + h)),
                  pl.BlockSpec((R, dv), lambda i, h: (i, kz + h)),
                  pl.BlockSpec((R, GATE_LANES), lambda i, h: (i, 0)),
                  pl.BlockSpec((1, dv), lambda i, h: (0, h)),
                  pl.BlockSpec((nb, None, dv, dqk), lambda i, h: (i, h, 0, 0)),
                  pl.BlockSpec((None, nb, dqk), lambda i, h: (h, i, 0))],
        out_specs=[pl.BlockSpec((R, dv), lambda i, h: (i, h)),
                   pl.BlockSpec((nb, None, dv, dqk), lambda i, h: (i, h, 0, 0)),
                   pl.BlockSpec((None, nb, dqk), lambda i, h: (h, i, 0)),
                   pl.BlockSpec((None, nb, GATE_LANES), lambda i, h: (h, i, 0))],
        out_shape=[jax.ShapeDtypeStruct((B * steps, nh * dv), BF16),
                   jax.ShapeDtypeStruct(c0.shape, F32),
                   jax.ShapeDtypeStruct((nh, B, dqk), F32),
                   jax.ShapeDtypeStruct((nh, B, GATE_LANES), F32)],
        compiler_params=_params(("parallel", "parallel"), 6 * nb * dv * dqk * 4 + (8 << 20)),
        name="mlstm_sample",
    )(proj, proj, proj, proj, proj_z, gates, g_head.reshape(1, nh * dv), c0, n0)


def _finish(x, h_mix, h_att, w_out, layer, g_post, g_next):
    y = matmul([h_mix, h_att], w_out, F32, layer=layer, tm=512, tn=1024)
    return post_norm_residual(x, y, g_post, g_next)


def _mlstm_weights(w_in, D):
    nh = MLSTM_HEADS
    g0 = 3 * D
    w_main = w_in[:, :, :g0].astype(BF16)
    w_z = w_in[:, :, g0 + 2 * nh:].astype(BF16)
    w_gates = jnp.pad(w_in[:, :, g0:g0 + 2 * nh], ((0, 0), (0, 0), (0, GATE_LANES - 2 * nh))).astype(BF16)
    return w_main, w_z, w_gates


def kernel(x_prompt, x_sample, mem_prompt, state_pool, state_mlstm_C, state_mlstm_n, state_mlstm_m,
           cache_mem_k, cache_mem_v, norm_pre, norm_post, norm_mem, w_mem_kv, w_out,
           w_in_pool, w_pool_group, pool_scale, w_in_mlstm, b_igate, b_fgate, mlstm_head_norm):
    B, T, D = x_prompt.shape
    Bs, Ts, _ = x_sample.shape
    S = mem_prompt.shape[1]
    depth = w_out.shape[0]
    nh = MLSTM_HEADS
    dx = D // 2
    dv = D // nh
    dqk = dv // 2
    hd = dx // XATT_HEADS

    xp = x_prompt.reshape(B * T, D)
    xs = x_sample.reshape(Bs * Ts, D)
    mem = mem_prompt.reshape(B * S, D)
    hp = rmsnorm_cast(xp, norm_pre[0])
    hs = rmsnorm_cast(xs, norm_pre[0])

    mem_k, mem_v = mem_kv(mem, norm_mem, w_mem_kv.astype(BF16))
    w_o = w_out.astype(BF16)
    w_pool = w_in_pool.astype(BF16)
    w_main, w_z, w_gates = _mlstm_weights(w_in_mlstm, D)

    pool_p_l, pool_s_l = [], []
    cp_l, np_l, mp_l, cs_l, ns_l, ms_l = [], [], [], [], [], []
    for layer in range(depth):
        j = layer // 2
        g_next = norm_pre[layer + 1] if layer + 1 < depth else jnp.ones((D,), F32)
        mk16 = mem_k[layer].astype(BF16).reshape(B, S, dx)
        mv16 = mem_v[layer].astype(BF16).reshape(B, S, dx)
        if layer % 2 == 0:
            w_g = w_pool_group[j].astype(BF16)
            z_mix, qb, zb = D + dx, D // dx, (2 * D + dx) // dx
            proj = matmul([hp], w_pool, BF16, layer=j)
            pooled = pool_prompt(proj, jnp.zeros((B, POOL_HIST, D), F32), B, 0)
            mix = group_mix_gate(pooled, w_g, pool_scale[j], proj, z_mix)
            att = attend_prompt(proj, qb, proj, zb, mk16, mv16)
            xp, hp = _finish(xp, mix, att, w_o, layer, norm_post[layer], g_next)
            pool_p_l.append(proj.reshape(B, T, -1)[:, T - POOL_HIST:, :D].astype(F32))
            proj = matmul([hs], w_pool, BF16, layer=j)
            u = proj[:, :D].astype(F32).reshape(Bs, Ts, D)
            ext = jnp.concatenate([state_pool[j], u], axis=1)
            pooled = pool_sample(ext.transpose(1, 0, 2), PAST_LEN)
            pooled = pooled.transpose(1, 0, 2).reshape(Bs * Ts, D)
            mix = group_mix_gate(pooled, w_g, pool_scale[j], proj, z_mix)
            att = attend_sample(proj, qb, proj, zb, cache_mem_k, cache_mem_v, layer, Ts)
            xs, hs = _finish(xs, mix, att, w_o, layer, norm_post[layer], g_next)
            pool_s_l.append(ext[:, Ts:])
        else:
            bias = jnp.pad(jnp.concatenate([b_igate[j], b_fgate[j]]), (0, GATE_LANES - 2 * nh))
            bias = bias.reshape(1, GATE_LANES)
            qb, zb = 0, (dx + D) // dx
            proj = matmul([hp], w_main, BF16, layer=j)
            proj_z = matmul([hp], w_z, BF16, layer=j)
            gates = mlstm_gates(hp, w_gates[j], bias)
            mix, c1, n1, m1 = mlstm_prompt(proj, proj_z, gates, mlstm_head_norm[j], B, dqk, dv)
            att = attend_prompt(proj_z, qb, proj_z, zb, mk16, mv16)
            xp, hp = _finish(xp, mix, att, w_o, layer, norm_post[layer], g_next)
            cp_l.append(c1); np_l.append(n1); mp_l.append(m1[:, :, 0])
            proj = matmul([hs], w_main, BF16, layer=j)
            proj_z = matmul([hs], w_z, BF16, layer=j)
            gates = mlstm_gates(hs, w_gates[j], bias)
            m_rows = jnp.repeat(state_mlstm_m[j], Ts, axis=0)
            gates = jnp.concatenate([gates[:, :2 * nh], m_rows,
                                     jnp.zeros((Bs * Ts, GATE_LANES - 3 * nh), F32)], axis=1)
            mix, c1, n1, m1 = mlstm_sample(proj, proj_z, gates, mlstm_head_norm[j], state_mlstm_C[j],
                                           state_mlstm_n[j].transpose(1, 0, 2), Ts, dqk, dv)
            att = attend_sample(proj_z, qb, proj_z, zb, cache_mem_k, cache_mem_v, layer, Ts)
            xs, hs = _finish(xs, mix, att, w_o, layer, norm_post[layer], g_next)
            cs_l.append(c1); ns_l.append(n1.transpose(1, 0, 2)); ms_l.append(m1[:, :, 0].T)

    return (xp.reshape(B, T, D), xs.reshape(Bs, Ts, D), jnp.stack(pool_p_l),
            mem_k.reshape(depth, B, S, XATT_HEADS, hd), mem_v.reshape(depth, B, S, XATT_HEADS, hd),
            jnp.stack(cp_l), jnp.stack(np_l), jnp.stack(mp_l),
            jnp.stack(pool_s_l), jnp.stack(cs_l), jnp.stack(ns_l), jnp.stack(ms_l))
```

```python
import functools

import jax
import jax.numpy as jnp
from jax import lax
from jax.experimental import pallas as pl
from jax.experimental.pallas import tpu as pltpu

F32 = jnp.float32
BF16 = jnp.bfloat16

NORM_EPS = 1e-6
GATE_SOFTCAP = 15.0
POOL_WINDOWS = (2, 4, 8, 16)
POOL_HIST = max(POOL_WINDOWS) - 1
HALO = POOL_HIST + 1
XATT_HEADS = 4
MLSTM_HEADS = 8
PAST_LEN = 16384
GATE_LANES = 128
MLSTM_CHUNK = 256
SAMPLE_BATCH_BLOCK = 8

V7X_VMEM_BYTES = 64 * 1024 * 1024
VMEM_CAP = V7X_VMEM_BYTES - 8 * 1024 * 1024


def _params(semantics, vmem_bytes):
    limit = int(min(max(vmem_bytes, 16 * 1024 * 1024), VMEM_CAP))
    return pltpu.CompilerParams(dimension_semantics=semantics, vmem_limit_bytes=limit)


def _sigmoid(x):
    return 1.0 / (1.0 + jnp.exp(-x))


def _silu(x):
    return x * _sigmoid(x)


def _rms(x, g):
    r = lax.rsqrt(jnp.mean(x * x, axis=-1, keepdims=True) + NORM_EPS)
    return x * r * g


def _rmsnorm_kernel(x_ref, g_ref, o_ref):
    o_ref[...] = _rms(x_ref[...], g_ref[...]).astype(o_ref.dtype)


def rmsnorm_cast(x, g, tm=256):
    M, D = x.shape
    tm = min(tm, M)
    return pl.pallas_call(
        _rmsnorm_kernel,
        grid=(M // tm,),
        in_specs=[pl.BlockSpec((tm, D), lambda i: (i, 0)),
                  pl.BlockSpec((1, D), lambda i: (0, 0))],
        out_specs=pl.BlockSpec((tm, D), lambda i: (i, 0)),
        out_shape=jax.ShapeDtypeStruct((M, D), BF16),
        compiler_params=_params(("parallel",), 8 * tm * D * 4),
        name="rmsnorm_cast",
    )(x, g.reshape(1, D))


def _post_kernel(*refs, with_next):
    x_ref, y_ref, g_ref = refs[:3]
    out = x_ref[...] + _rms(y_ref[...].astype(F32), g_ref[...])
    if with_next:
        gn_ref, o_ref, h_ref = refs[3:]
        h_ref[...] = _rms(out, gn_ref[...]).astype(h_ref.dtype)
    else:
        o_ref = refs[3]
    o_ref[...] = out


def post_norm_residual(x, y, g_post, g_next=None, tm=256):
    M, D = x.shape
    tm = min(tm, M)
    row = pl.BlockSpec((tm, D), lambda i: (i, 0))
    vec = pl.BlockSpec((1, D), lambda i: (0, 0))
    with_next = g_next is not None
    args = [x, y, g_post.reshape(1, D)] + ([g_next.reshape(1, D)] if with_next else [])
    out_shape = [jax.ShapeDtypeStruct((M, D), F32)] + ([jax.ShapeDtypeStruct((M, D), BF16)] if with_next else [])
    res = pl.pallas_call(
        functools.partial(_post_kernel, with_next=with_next),
        grid=(M // tm,),
        in_specs=[row, row, vec] + ([vec] if with_next else []),
        out_specs=[row] * len(out_shape),
        out_shape=out_shape,
        compiler_params=_params(("parallel",), 12 * tm * D * 4),
        name="post_norm_residual",
    )(*args)
    return (res[0], res[1]) if with_next else (res[0], None)


LANES = 128


def _cast_columns_kernel(*refs, shift):
    o_ref = refs[-1]
    if shift:
        m_ref, e_ref = refs[:2]
        o_ref[...] = jnp.concatenate([m_ref[:, shift:], e_ref[:, :shift]], axis=1).astype(o_ref.dtype)
    else:
        o_ref[...] = refs[0][...].astype(o_ref.dtype)


def cast_columns(w, layer, col0, ncols, tk=1024, tn=1024):
    _, K, _ = w.shape
    tk, tn = min(tk, K), min(tn, ncols)
    shift = col0 % LANES
    base = col0 - shift
    assert base % tn == 0 and ncols % tn == 0 and K % tk == 0
    in_specs = [pl.BlockSpec((None, tk, tn), lambda r, c: (layer, r, base // tn + c))]
    if shift:
        in_specs.append(pl.BlockSpec((None, tk, LANES), lambda r, c: (layer, r, (base + (c + 1) * tn) // LANES)))
    return pl.pallas_call(
        functools.partial(_cast_columns_kernel, shift=shift),
        grid=(K // tk, ncols // tn),
        in_specs=in_specs,
        out_specs=pl.BlockSpec((tk, tn), lambda r, c: (r, c)),
        out_shape=jax.ShapeDtypeStruct((K, ncols), BF16),
        compiler_params=_params(("parallel", "parallel"), 6 * tk * tn * 4),
        name="cast_columns",
    )(*([w] * len(in_specs)))


def _mm_kernel(*refs, ksizes):
    n_a = len(ksizes)
    a_refs, w_ref, o_ref = refs[:n_a], refs[n_a], refs[n_a + 1]
    acc = None
    off = 0
    for a_ref, ks in zip(a_refs, ksizes):
        part = jnp.dot(a_ref[...], w_ref[off:off + ks, :], preferred_element_type=F32)
        acc = part if acc is None else acc + part
        off += ks
    o_ref[...] = acc.astype(o_ref.dtype)


def matmul(a_list, w, out_dtype, layer=0, tm=1024, tn=1024):
    M = a_list[0].shape[0]
    _, K, N = w.shape
    ksizes = tuple(a.shape[1] for a in a_list)
    assert sum(ksizes) == K
    tm, tn = min(tm, M), min(tn, N)
    osz = jnp.dtype(out_dtype).itemsize
    vmem = 2 * (tm * K * 2 + K * tn * 2 + tm * tn * osz) + 2 * tm * tn * 4
    return pl.pallas_call(
        functools.partial(_mm_kernel, ksizes=ksizes),
        grid=(N // tn, M // tm),
        in_specs=[pl.BlockSpec((tm, ks), lambda j, i: (i, 0)) for ks in ksizes]
        + [pl.BlockSpec((None, K, tn), lambda j, i: (layer, 0, j))],
        out_specs=pl.BlockSpec((tm, tn), lambda j, i: (i, j)),
        out_shape=jax.ShapeDtypeStruct((M, N), out_dtype),
        compiler_params=_params(("parallel", "parallel"), vmem + (4 << 20)),
        name="matmul",
    )(*a_list, w)


def _mem_kv_kernel(x_ref, g_ref, w_ref, k_ref, v_ref, k16_ref, v16_ref, h_scr):
    @pl.when(pl.program_id(2) == 0)
    def _():
        h_scr[...] = _rms(x_ref[...], g_ref[...]).astype(h_scr.dtype)

    kv = jnp.dot(h_scr[...], w_ref[...], preferred_element_type=F32)
    half = pl.num_programs(2) // 2

    @pl.when(pl.program_id(2) < half)
    def _():
        k_ref[...] = kv
        k16_ref[...] = kv.astype(k16_ref.dtype)

    @pl.when(pl.program_id(2) >= half)
    def _():
        v_ref[...] = kv
        v16_ref[...] = kv.astype(v16_ref.dtype)


def mem_kv(mem, g_mem, w_kv, tm=512, tn=512):
    M, D = mem.shape
    depth, _, N = w_kv.shape
    dx = N // 2
    tm, tn = min(tm, M), min(tn, dx)
    nj = N // tn
    half = nj // 2
    vmem = 2 * (tm * D * 4 + D * tn * 2 + 3 * tm * tn * 4) + tm * D * 2 + 2 * tm * tn * 4 + 2 * tm * D * 4
    out = jax.ShapeDtypeStruct((depth, M, dx), F32)
    out16 = jax.ShapeDtypeStruct((depth, M, dx), BF16)
    k_spec = pl.BlockSpec((None, tm, tn), lambda l, i, j: (l, i, jnp.minimum(j, half - 1)))
    v_spec = pl.BlockSpec((None, tm, tn), lambda l, i, j: (l, i, jnp.maximum(j - half, 0)))
    return pl.pallas_call(
        _mem_kv_kernel,
        grid=(depth, M // tm, nj),
        in_specs=[pl.BlockSpec((tm, D), lambda l, i, j: (i, 0)),
                  pl.BlockSpec((None, 1, D), lambda l, i, j: (l, 0, 0)),
                  pl.BlockSpec((None, D, tn), lambda l, i, j: (l, 0, j))],
        out_specs=[k_spec, v_spec, k_spec, v_spec],
        out_shape=[out, out, out16, out16],
        scratch_shapes=[pltpu.VMEM((tm, D), BF16)],
        compiler_params=_params(("parallel", "parallel", "arbitrary"), vmem + (4 << 20)),
        name="mem_kv",
    )(mem, g_mem.reshape(depth, 1, D), w_kv)


def _gates_kernel(h_ref, w_ref, b_ref, o_ref):
    nh = MLSTM_HEADS
    pre = jnp.dot(h_ref[...], w_ref[...], preferred_element_type=F32) + b_ref[...]
    capped = GATE_SOFTCAP * jnp.tanh(pre / GATE_SOFTCAP)
    log_sig = jnp.minimum(capped, 0.0) - jnp.log(1.0 + jnp.exp(-jnp.abs(capped)))
    lane = lax.broadcasted_iota(jnp.int32, capped.shape, 1)
    o_ref[...] = jnp.where(lane < nh, capped, log_sig)


def mlstm_gates(h, w_gates, bias, tm=512):
    M, D = h.shape
    tm = min(tm, M)
    return pl.pallas_call(
        _gates_kernel,
        grid=(M // tm,),
        in_specs=[pl.BlockSpec((tm, D), lambda i: (i, 0)),
                  pl.BlockSpec((D, GATE_LANES), lambda i: (0, 0)),
                  pl.BlockSpec((1, GATE_LANES), lambda i: (0, 0))],
        out_specs=pl.BlockSpec((tm, GATE_LANES), lambda i: (i, 0)),
        out_shape=jax.ShapeDtypeStruct((M, GATE_LANES), F32),
        compiler_params=_params(("parallel",), 4 * tm * D * 2 + 4 * D * GATE_LANES * 2),
        name="mlstm_gates",
    )(h, w_gates, bias)


def _group_mm_kernel(a_ref, w_ref, s_ref, z_ref, o_ref):
    mixed = jnp.dot(a_ref[...], w_ref[...], preferred_element_type=F32) * s_ref[...]
    o_ref[...] = (mixed * _silu(z_ref[...].astype(F32))).astype(o_ref.dtype)


def group_mix_gate(pooled, w_group, scale, proj, z_off, tm=512):
    M, D = pooled.shape
    G, Gs, _ = w_group.shape
    tm = min(tm, M)
    zb = z_off // Gs
    return pl.pallas_call(
        _group_mm_kernel,
        grid=(G, M // tm),
        in_specs=[pl.BlockSpec((tm, Gs), lambda g, i: (i, g)),
                  pl.BlockSpec((None, Gs, Gs), lambda g, i: (g, 0, 0)),
                  pl.BlockSpec((1, Gs), lambda g, i: (0, g)),
                  pl.BlockSpec((tm, Gs), lambda g, i: (i, zb + g))],
        out_specs=pl.BlockSpec((tm, Gs), lambda g, i: (i, g)),
        out_shape=jax.ShapeDtypeStruct((M, D), BF16),
        compiler_params=_params(("parallel", "parallel"), 6 * tm * Gs * 4 + 4 * Gs * Gs * 2),
        name="group_mix_gate",
    )(pooled, w_group, scale.reshape(1, D), proj)


def _window_sums(ext, levels):
    sums = []
    cur = ext
    for lv in range(levels):
        cur = cur + pltpu.roll(cur, 1 << lv, 0)
        sums.append(cur)
    return sums


def _pool_mix_prompt_kernel(u_ref, halo_ref, hist_ref, zlo_ref, zhi_ref, w_ref, s_ref, o_ref, *, pos0, tt):
    i = pl.program_id(1)
    ng = len(POOL_WINDOWS)
    gs = u_ref.shape[1] // ng
    row = lax.broadcasted_iota(jnp.int32, (tt, 1), 0) + i * tt
    pos1 = (row + (pos0 + 1)).astype(F32)
    first = i == 0
    for g, w in enumerate(POOL_WINDOWS):
        lo, hi = g * gs, (g + 1) * gs
        u = u_ref[:, lo:hi].astype(F32)
        halo = jnp.where(first, hist_ref[:, lo:hi], halo_ref[:, lo:hi].astype(F32))
        ext = jnp.concatenate([halo, u], axis=0)
        win = _window_sums(ext, g + 1)[-1][HALO:, :]
        inv_cnt = 1.0 / jnp.minimum(float(w), pos1)
        pooled = (win * inv_cnt - u).astype(BF16)
        mixed = jnp.dot(pooled, w_ref[g], preferred_element_type=F32) * s_ref[:, lo:hi]
        z_ref, zg = (zlo_ref, g) if g < ng // 2 else (zhi_ref, g - ng // 2)
        z = z_ref[:, zg * gs:(zg + 1) * gs].astype(F32)
        o_ref[:, lo:hi] = (mixed * _silu(z)).astype(o_ref.dtype)


def pool_mix_prompt(proj, hist, w_group, scale, batch, pos0, z_off, tt=512):
    D = hist.shape[-1]
    G, Gs, _ = w_group.shape
    T = proj.shape[0] // batch
    tt = min(tt, T)
    nt = T // tt
    dh = D // 2
    assert z_off % dh == 0 and G == len(POOL_WINDOWS)
    zb = z_off // dh
    hist16 = jnp.concatenate([jnp.zeros((batch, 1, D), F32), hist.astype(F32)], axis=1)
    vmem = 2 * (3 * tt * D * 2 + G * Gs * Gs * 2) + 12 * (tt + HALO) * Gs * 4
    return pl.pallas_call(
        functools.partial(_pool_mix_prompt_kernel, pos0=pos0, tt=tt),
        grid=(batch, nt),
        in_specs=[pl.BlockSpec((tt, D), lambda b, i: (b * nt + i, 0)),
                  pl.BlockSpec((HALO, D),
                               lambda b, i: (jnp.maximum((b * nt + i) * (tt // HALO) - 1, 0), 0)),
                  pl.BlockSpec((None, HALO, D), lambda b, i: (b, 0, 0)),
                  pl.BlockSpec((tt, dh), lambda b, i: (b * nt + i, zb)),
                  pl.BlockSpec((tt, dh), lambda b, i: (b * nt + i, zb + 1)),
                  pl.BlockSpec((G, Gs, Gs), lambda b, i: (0, 0, 0)),
                  pl.BlockSpec((1, D), lambda b, i: (0, 0))],
        out_specs=pl.BlockSpec((tt, D), lambda b, i: (b * nt + i, 0)),
        out_shape=jax.ShapeDtypeStruct((batch * T, D), BF16),
        compiler_params=_params(("parallel", "parallel"), vmem + (4 << 20)),
        name="pool_mix_prompt",
    )(proj, proj, hist16, proj, proj, w_group, scale.reshape(1, D))


def _pool_sample_kernel(ext_ref, o_ref, *, pos0):
    steps = o_ref.shape[0]
    gs = ext_ref.shape[2] // len(POOL_WINDOWS)
    for g, w in enumerate(POOL_WINDOWS):
        lo, hi = g * gs, (g + 1) * gs
        for t in range(steps):
            cur = ext_ref[POOL_HIST + t, :, lo:hi]
            win = cur
            for r in range(1, w):
                win = win + ext_ref[POOL_HIST + t - r, :, lo:hi]
            cnt = float(min(w, pos0 + t + 1))
            o_ref[t, :, lo:hi] = (win / cnt - cur).astype(o_ref.dtype)


def pool_sample(ext_t, pos0, bt=32):
    R, B, D = ext_t.shape
    T = R - POOL_HIST
    bt = min(bt, B)
    return pl.pallas_call(
        functools.partial(_pool_sample_kernel, pos0=pos0),
        grid=(B // bt,),
        in_specs=[pl.BlockSpec((R, bt, D), lambda i: (0, i, 0))],
        out_specs=pl.BlockSpec((T, bt, D), lambda i: (0, i, 0)),
        out_shape=jax.ShapeDtypeStruct((T, B, D), BF16),
        compiler_params=_params(("parallel",), 3 * R * bt * D * 4),
        name="pool_sample",
    )(ext_t)


def _softmax_rows(s):
    m = jnp.max(s, axis=-1, keepdims=True)
    p = jnp.exp(s - m)
    return p, jnp.sum(p, axis=-1, keepdims=True)


def _attn_prompt_kernel(q_ref, k_ref, v_ref, z_ref, o_ref):
    hd = q_ref.shape[1] // XATT_HEADS
    scale = hd ** -0.5
    for h in range(XATT_HEADS):
        lo, hi = h * hd, (h + 1) * hd
        s = lax.dot_general(q_ref[:, lo:hi], k_ref[:, lo:hi], (((1,), (1,)), ((), ())),
                            preferred_element_type=F32) * scale
        p, l = _softmax_rows(s)
        o = jnp.dot(p.astype(BF16), v_ref[:, lo:hi], preferred_element_type=F32) / l
        o_ref[:, lo:hi] = (o * _silu(z_ref[:, lo:hi].astype(F32))).astype(o_ref.dtype)


def attend_prompt(q_arr, qb, z_arr, zb, k, v, tq=512):
    B, S, DX = k.shape
    T = q_arr.shape[0] // B
    tq = min(tq, T)
    nq = T // tq
    return pl.pallas_call(
        _attn_prompt_kernel,
        grid=(B, nq),
        in_specs=[pl.BlockSpec((tq, DX), lambda b, i: (b * nq + i, qb)),
                  pl.BlockSpec((None, S, DX), lambda b, i: (b, 0, 0)),
                  pl.BlockSpec((None, S, DX), lambda b, i: (b, 0, 0)),
                  pl.BlockSpec((tq, DX), lambda b, i: (b * nq + i, zb))],
        out_specs=pl.BlockSpec((tq, DX), lambda b, i: (b * nq + i, 0)),
        out_shape=jax.ShapeDtypeStruct((B * T, DX), BF16),
        compiler_params=_params(("parallel", "parallel"), 8 * tq * DX * 4 + 8 * S * DX * 2),
        name="attend_prompt",
    )(q_arr, k, v, z_arr)


def _attn_sample_kernel(q_ref, k_hbm, v_hbm, z_ref, o_ref, kbuf, vbuf, sem, *, steps, layer, bb):
    i = pl.program_id(0)
    rows, dx = q_ref.shape
    hd = dx // XATT_HEADS
    scale = hd ** -0.5

    def head_copies(step, slot):
        out = []
        for t, (src, dst) in enumerate(((k_hbm, kbuf), (v_hbm, vbuf))):
            for h in range(XATT_HEADS):
                out.append(pltpu.make_async_copy(src.at[layer, pl.ds(step * bb, bb), :, h, :],
                                                 dst.at[slot, h], sem.at[t, slot, h]))
        return out

    @pl.when(i == 0)
    def _():
        for cp in head_copies(0, 0):
            cp.start()

    @pl.when(i + 1 < pl.num_programs(0))
    def _():
        for cp in head_copies(i + 1, (i + 1) % 2):
            cp.start()

    slot = i % 2
    for cp in head_copies(i, slot):
        cp.wait()

    row_batch = lax.broadcasted_iota(jnp.int32, (rows, 1), 0) // steps
    for h in range(XATT_HEADS):
        lo, hi = h * hd, (h + 1) * hd
        q = q_ref[:, lo:hi]
        acc = jnp.zeros((rows, hd), F32)
        for j in range(bb):
            kj = kbuf[slot, h, j].astype(BF16)
            vj = vbuf[slot, h, j].astype(BF16)
            s = lax.dot_general(q, kj, (((1,), (1,)), ((), ())), preferred_element_type=F32) * scale
            p, l = _softmax_rows(s)
            o = jnp.dot(p.astype(BF16), vj, preferred_element_type=F32) / l
            acc = jnp.where(row_batch == j, o, acc)
        o_ref[:, lo:hi] = (acc * _silu(z_ref[:, lo:hi].astype(F32))).astype(o_ref.dtype)


def attend_sample(q_arr, qb, z_arr, zb, k, v, layer, steps, bb=4):
    _, B, S, nh, hd = k.shape
    assert nh == XATT_HEADS
    DX = nh * hd
    rows = bb * steps
    slot_bytes = nh * bb * S * hd * 4
    return pl.pallas_call(
        functools.partial(_attn_sample_kernel, steps=steps, layer=layer, bb=bb),
        grid=(B // bb,),
        in_specs=[pl.BlockSpec((rows, DX), lambda i: (i, qb)),
                  pl.BlockSpec(memory_space=pl.ANY), pl.BlockSpec(memory_space=pl.ANY),
                  pl.BlockSpec((rows, DX), lambda i: (i, zb))],
        out_specs=pl.BlockSpec((rows, DX), lambda i: (i, 0)),
        out_shape=jax.ShapeDtypeStruct((B * steps, DX), BF16),
        scratch_shapes=[pltpu.VMEM((2, nh, bb, S, hd), F32), pltpu.VMEM((2, nh, bb, S, hd), F32),
                        pltpu.SemaphoreType.DMA((2, 2, nh))],
        compiler_params=_params(("arbitrary",), 4 * slot_bytes + (12 << 20)),
        name="attend_sample",
    )(q_arr, k, v, z_arr)


def _head_output(h, o, z, g):
    return _sigmoid(o) * _rms(h, g) * _silu(z)


def _mlstm_prompt_kernel(q_ref, k_ref, v_ref, o_ref, zlo_ref, zhi_ref, g_ref, gh_ref,
                         y_ref, c_out, n_out, m_out, ct_scr, n_scr, m_scr):
    c = pl.program_id(1)
    nh = MLSTM_HEADS
    L = q_ref.shape[0]
    dqk = q_ref.shape[1] // nh
    dv = v_ref.shape[1] // nh
    qscale = dqk ** -0.5

    @pl.when(c == 0)
    def _():
        ct_scr[...] = jnp.zeros_like(ct_scr)
        n_scr[...] = jnp.zeros_like(n_scr)
        m_scr[...] = jnp.zeros_like(m_scr)

    r_i = lax.broadcasted_iota(jnp.int32, (L, L), 0)
    c_i = lax.broadcasted_iota(jnp.int32, (L, L), 1)
    causal = c_i <= r_i
    eye = (c_i == r_i).astype(F32)
    tril = causal.astype(F32)
    triu = (r_i <= c_i).astype(F32)
    gates = g_ref[...]

    for h in range(nh):
        q = (q_ref[:, h * dqk:(h + 1) * dqk].astype(F32) * qscale)
        qb = q.astype(BF16)
        k = k_ref[:, h * dqk:(h + 1) * dqk]
        v = v_ref[:, h * dv:(h + 1) * dv]
        ig = gates[:, h:h + 1]
        lf = gates[:, nh + h:nh + h + 1]
        m_prev = m_scr[h:h + 1, 0:1]
        n_prev = n_scr[h:h + 1, :]
        ct = ct_scr[h]

        lf_row = jnp.sum(lf * eye, axis=0, keepdims=True)
        ig_row = jnp.sum(ig * eye, axis=0, keepdims=True)
        b_col = jnp.sum(tril * lf_row, axis=1, keepdims=True)
        b_row = jnp.sum(triu * lf, axis=0, keepdims=True)
        d = jnp.where(causal, b_col - b_row + ig_row, -jnp.inf)
        inter = b_col + m_prev
        m_t = jnp.maximum(inter, jnp.max(d, axis=1, keepdims=True))
        w = jnp.exp(d - m_t)
        w_inter = jnp.exp(inter - m_t)
        s = lax.dot_general(qb, k, (((1,), (1,)), ((), ())), preferred_element_type=F32) * w
        num = (jnp.dot(s.astype(BF16), v, preferred_element_type=F32)
               + w_inter * jnp.dot(qb, ct.astype(BF16), preferred_element_type=F32))
        den = (jnp.sum(s, axis=1, keepdims=True)
               + w_inter * jnp.sum(q * n_prev, axis=1, keepdims=True))
        hh = num / jnp.maximum(jnp.abs(den), jnp.exp(-m_t))

        b_last = b_col[L - 1:L, :]
        m_new = m_t[L - 1:L, :]
        w_state = jnp.exp(b_last - b_col + ig - m_new)
        decay = jnp.exp(b_last + m_prev - m_new)
        kw = k.astype(F32) * w_state
        ct_scr[h] = decay * ct + lax.dot_general(kw.astype(BF16), v, (((0,), (0,)), ((), ())),
                                                 preferred_element_type=F32)
        n_scr[h:h + 1, :] = decay * n_prev + jnp.sum(kw, axis=0, keepdims=True)
        m_scr[h:h + 1, :] = jnp.broadcast_to(m_new, (1, m_scr.shape[1]))

        lo, hi = h * dv, (h + 1) * dv
        z_ref, zh = (zlo_ref, h) if h < nh // 2 else (zhi_ref, h - nh // 2)
        y_ref[:, lo:hi] = _head_output(hh, o_ref[:, lo:hi].astype(F32),
                                       z_ref[:, zh * dv:(zh + 1) * dv].astype(F32),
                                       gh_ref[:, lo:hi]).astype(y_ref.dtype)

    @pl.when(c == pl.num_programs(1) - 1)
    def _():
        for h in range(nh):
            c_out[h] = ct_scr[h].T
        n_out[...] = n_scr[...]
        m_out[...] = m_scr[...]


def mlstm_prompt(proj, proj_z, gates, g_head, batch, dqk, dv):
    nh = MLSTM_HEADS
    T = proj.shape[0] // batch
    L = min(MLSTM_CHUNK, T)
    nc = T // L
    wq, wv = nh * dqk, nh * dv
    assert wv == 2 * wq
    rows = lambda b, c: b * nc + c
    return pl.pallas_call(
        _mlstm_prompt_kernel,
        grid=(batch, nc),
        in_specs=[pl.BlockSpec((L, wq), lambda b, c: (rows(b, c), 0)),
                  pl.BlockSpec((L, wq), lambda b, c: (rows(b, c), 1)),
                  pl.BlockSpec((L, wv), lambda b, c: (rows(b, c), 1)),
                  pl.BlockSpec((L, wv), lambda b, c: (rows(b, c), 2)),
                  pl.BlockSpec((L, wq), lambda b, c: (rows(b, c), 1)),
                  pl.BlockSpec((L, wq), lambda b, c: (rows(b, c), 2)),
                  pl.BlockSpec((L, GATE_LANES), lambda b, c: (rows(b, c), 0)),
                  pl.BlockSpec((1, wv), lambda b, c: (0, 0))],
        out_specs=[pl.BlockSpec((L, wv), lambda b, c: (rows(b, c), 0)),
                   pl.BlockSpec((None, nh, dv, dqk), lambda b, c: (b, 0, 0, 0)),
                   pl.BlockSpec((None, nh, dqk), lambda b, c: (b, 0, 0)),
                   pl.BlockSpec((None, nh, GATE_LANES), lambda b, c: (b, 0, 0))],
        out_shape=[jax.ShapeDtypeStruct((batch * T, wv), BF16),
                   jax.ShapeDtypeStruct((batch, nh, dv, dqk), F32),
                   jax.ShapeDtypeStruct((batch, nh, dqk), F32),
                   jax.ShapeDtypeStruct((batch, nh, GATE_LANES), F32)],
        scratch_shapes=[pltpu.VMEM((nh, dqk, dv), F32),
                        pltpu.VMEM((nh, dqk), F32),
                        pltpu.VMEM((nh, GATE_LANES), F32)],
        compiler_params=_params(("parallel", "arbitrary"),
                                24 * L * wv + 5 * nh * dv * dqk * 4 + (16 << 20)),
        name="mlstm_prompt",
    )(proj, proj, proj, proj, proj_z, proj_z, gates, g_head.reshape(1, wv))


def _lane_pick(x, lane, idx):
    return jnp.sum(jnp.where(lane == idx, x, 0.0), axis=1, keepdims=True)


def _mlstm_sample_kernel(q_ref, k_ref, v_ref, o_ref, z_ref, g_ref, gh_ref, c_ref, n_ref,
                         y_ref, c_out, n_out, m_out, *, steps):
    head = pl.program_id(1)
    nh = MLSTM_HEADS
    R, dqk = q_ref.shape
    nb = R // steps
    qscale = dqk ** -0.5

    gates = g_ref[...]
    lane = lax.broadcasted_iota(jnp.int32, gates.shape, 1)
    ig = _lane_pick(gates, lane, head)
    lf = _lane_pick(gates, lane, nh + head)
    m_prev = _lane_pick(gates, lane, 2 * nh + head)

    r_i = lax.broadcasted_iota(jnp.int32, (R, R), 0)
    c_i = lax.broadcasted_iota(jnp.int32, (R, R), 1)
    same = (r_i // steps) == (c_i // steps)
    causal = same & (c_i <= r_i)
    eye = (c_i == r_i).astype(F32)
    lower = causal.astype(F32)
    upper = (same & (r_i <= c_i)).astype(F32)
    last_of_row_batch = (c_i == (r_i // steps) * steps + (steps - 1)).astype(F32)
    row_batch = lax.broadcasted_iota(jnp.int32, (R, 1), 0) // steps

    q = q_ref[...].astype(F32) * qscale
    qb = q.astype(BF16)
    k = k_ref[...]
    v = v_ref[...]

    lf_row = jnp.sum(lf * eye, axis=0, keepdims=True)
    ig_row = jnp.sum(ig * eye, axis=0, keepdims=True)
    b_col = jnp.sum(lower * lf_row, axis=1, keepdims=True)
    b_row = jnp.sum(upper * lf, axis=0, keepdims=True)
    d = jnp.where(causal, b_col - b_row + ig_row, -jnp.inf)
    inter = b_col + m_prev
    m_t = jnp.maximum(inter, jnp.max(d, axis=1, keepdims=True))
    w = jnp.exp(d - m_t)
    w_inter = jnp.exp(inter - m_t)
    s = lax.dot_general(qb, k, (((1,), (1,)), ((), ())), preferred_element_type=F32) * w

    qc = jnp.zeros((R, v_ref.shape[1]), F32)
    n_rows = jnp.zeros((R, dqk), F32)
    for j in range(nb):
        mine = row_batch == j
        qcj = lax.dot_general(qb, c_ref[j].astype(BF16), (((1,), (1,)), ((), ())),
                              preferred_element_type=F32)
        qc = jnp.where(mine, qcj, qc)
        n_rows = jnp.where(mine, n_ref[j:j + 1, :], n_rows)

    num = jnp.dot(s.astype(BF16), v, preferred_element_type=F32) + w_inter * qc
    den = jnp.sum(s, axis=1, keepdims=True) + w_inter * jnp.sum(q * n_rows, axis=1, keepdims=True)
    hh = num / jnp.maximum(jnp.abs(den), jnp.exp(-m_t))
    y_ref[...] = _head_output(hh, o_ref[...].astype(F32), z_ref[...].astype(F32),
                              gh_ref[...]).astype(y_ref.dtype)

    m_row = jnp.sum(m_t * eye, axis=0, keepdims=True)
    b_last = jnp.sum(last_of_row_batch * b_row, axis=1, keepdims=True)
    m_new = jnp.sum(last_of_row_batch * m_row, axis=1, keepdims=True)
    w_state = jnp.exp(b_last - b_col + ig - m_new)
    decay = jnp.exp(b_last + m_prev - m_new)
    vw = v.astype(F32) * w_state
    kw = k.astype(F32) * w_state
    for j in range(nb):
        mine = row_batch == j
        r0 = j * steps
        dj = decay[r0:r0 + 1, :]
        upd = lax.dot_general(jnp.where(mine, vw, 0.0).astype(BF16), k, (((0,), (0,)), ((), ())),
                              preferred_element_type=F32)
        c_out[j] = dj * c_ref[j] + upd
        n_out[j:j + 1, :] = dj * n_ref[j:j + 1, :] + jnp.sum(jnp.where(mine, kw, 0.0), axis=0, keepdims=True)
        m_out[j:j + 1, :] = jnp.broadcast_to(m_new[r0:r0 + 1, :], (1, m_out.shape[1]))


def mlstm_sample(proj, proj_z, gates, g_head, c0, n0, steps, dqk, dv):
    nh = MLSTM_HEADS
    B = c0.shape[0]
    nb = SAMPLE_BATCH_BLOCK
    R = nb * steps
    kq = nh
    kv = (2 * nh * dqk) // dv
    ko = kv + nh
    kz = (nh * dqk) // dv
    return pl.pallas_call(
        functools.partial(_mlstm_sample_kernel, steps=steps),
        grid=(B // nb, nh),
        in_specs=[pl.BlockSpec((R, dqk), lambda i, h: (i, h)),
                  pl.BlockSpec((R, dqk), lambda i, h: (i, kq + h)),
                  pl.BlockSpec((R, dv), lambda i, h: (i, kv + h)),
                  pl.BlockSpec((R, dv), lambda i, h: (i, ko + h)),
                  pl.BlockSpec((R, dv), lambda i, h: (i, kz + h)),
                  pl.BlockSpec((R, GATE_LANES), lambda i, h: (i, 0)),
                  pl.BlockSpec((1, dv), lambda i, h: (0, h)),
                  pl.BlockSpec((nb, None, dv, dqk), lambda i, h: (i, h, 0, 0)),
                  pl.BlockSpec((None, nb, dqk), lambda i, h: (h, i, 0))],
        out_specs=[pl.BlockSpec((R, dv), lambda i, h: (i, h)),
                   pl.BlockSpec((nb, None, dv, dqk), lambda i, h: (i, h, 0, 0)),
                   pl.BlockSpec((None, nb, dqk), lambda i, h: (h, i, 0)),
                   pl.BlockSpec((None, nb, GATE_LANES), lambda i, h: (h, i, 0))],
        out_shape=[jax.ShapeDtypeStruct((B * steps, nh * dv), BF16),
                   jax.ShapeDtypeStruct(c0.shape, F32),
                   jax.ShapeDtypeStruct((nh, B, dqk), F32),
                   jax.ShapeDtypeStruct((nh, B, GATE_LANES), F32)],
        compiler_params=_params(("parallel", "parallel"), 6 * nb * dv * dqk * 4 + (8 << 20)),
        name="mlstm_sample",
    )(proj, proj, proj, proj, proj_z, gates, g_head.reshape(1, nh * dv), c0, n0)


def _finish(x, h_mix, h_att, w_out, layer, g_post, g_next):
    y = matmul([h_mix, h_att], w_out, BF16, layer=layer, tm=512, tn=1024)
    return post_norm_residual(x, y, g_post, g_next)


def _mlstm_weights(w_in, j, D):
    nh = MLSTM_HEADS
    g0 = 3 * D
    w_main = cast_columns(w_in, j, 0, g0)[None]
    w_z = cast_columns(w_in, j, g0 + 2 * nh, w_in.shape[2] - g0 - 2 * nh)[None]
    w_gates = jnp.pad(w_in[j, :, g0:g0 + 2 * nh], ((0, 0), (0, GATE_LANES - 2 * nh))).astype(BF16)
    return w_main, w_z, w_gates


def kernel(x_prompt, x_sample, mem_prompt, state_pool, state_mlstm_C, state_mlstm_n, state_mlstm_m,
           cache_mem_k, cache_mem_v, norm_pre, norm_post, norm_mem, w_mem_kv, w_out,
           w_in_pool, w_pool_group, pool_scale, w_in_mlstm, b_igate, b_fgate, mlstm_head_norm):
    B, T, D = x_prompt.shape
    Bs, Ts, _ = x_sample.shape
    S = mem_prompt.shape[1]
    depth = w_out.shape[0]
    nh = MLSTM_HEADS
    dx = D // 2
    dv = D // nh
    dqk = dv // 2
    hd = dx // XATT_HEADS

    xp = x_prompt.reshape(B * T, D)
    xs = x_sample.reshape(Bs * Ts, D)
    mem = mem_prompt.reshape(B * S, D)
    hp = rmsnorm_cast(xp, norm_pre[0])
    hs = rmsnorm_cast(xs, norm_pre[0])

    mem_k, mem_v, mem_k16, mem_v16 = mem_kv(mem, norm_mem, w_mem_kv.astype(BF16))
    w_o = w_out.astype(BF16)
    w_pool = w_in_pool.astype(BF16)

    pool_p_l, pool_s_l = [], []
    cp_l, np_l, mp_l, cs_l, ns_l, ms_l = [], [], [], [], [], []
    for layer in range(depth):
        j = layer // 2
        g_next = norm_pre[layer + 1] if layer + 1 < depth else None
        mk16 = mem_k16[layer].reshape(B, S, dx)
        mv16 = mem_v16[layer].reshape(B, S, dx)
        if layer % 2 == 0:
            w_g = w_pool_group[j].astype(BF16)
            z_mix, qb, zb = D + dx, D // dx, (2 * D + dx) // dx
            proj = matmul([hp], w_pool, BF16, layer=j)
            mix = pool_mix_prompt(proj, jnp.zeros((B, POOL_HIST, D), F32), w_g, pool_scale[j], B, 0, z_mix)
            att = attend_prompt(proj, qb, proj, zb, mk16, mv16)
            xp, hp = _finish(xp, mix, att, w_o, layer, norm_post[layer], g_next)
            pool_p_l.append(proj.reshape(B, T, -1)[:, T - POOL_HIST:, :D].astype(F32))
            proj = matmul([hs], w_pool, BF16, layer=j)
            u = proj[:, :D].astype(F32).reshape(Bs, Ts, D)
            ext = jnp.concatenate([state_pool[j], u], axis=1)
            pooled = pool_sample(ext.transpose(1, 0, 2), PAST_LEN)
            pooled = pooled.transpose(1, 0, 2).reshape(Bs * Ts, D)
            mix = group_mix_gate(pooled, w_g, pool_scale[j], proj, z_mix)
            att = attend_sample(proj, qb, proj, zb, cache_mem_k, cache_mem_v, layer, Ts)
            xs, hs = _finish(xs, mix, att, w_o, layer, norm_post[layer], g_next)
            pool_s_l.append(ext[:, Ts:])
        else:
            bias = jnp.pad(jnp.concatenate([b_igate[j], b_fgate[j]]), (0, GATE_LANES - 2 * nh))
            bias = bias.reshape(1, GATE_LANES)
            qb, zb = 0, (dx + D) // dx
            w_main, w_z, w_gates = _mlstm_weights(w_in_mlstm, j, D)
            proj = matmul([hp], w_main, BF16)
            proj_z = matmul([hp], w_z, BF16)
            gates = mlstm_gates(hp, w_gates, bias)
            mix, c1, n1, m1 = mlstm_prompt(proj, proj_z, gates, mlstm_head_norm[j], B, dqk, dv)
            att = attend_prompt(proj_z, qb, proj_z, zb, mk16, mv16)
            xp, hp = _finish(xp, mix, att, w_o, layer, norm_post[layer], g_next)
            cp_l.append(c1); np_l.append(n1); mp_l.append(m1[:, :, 0])
            proj = matmul([hs], w_main, BF16)
            proj_z = matmul([hs], w_z, BF16)
            gates = mlstm_gates(hs, w_gates, bias)
            m_rows = jnp.repeat(state_mlstm_m[j], Ts, axis=0)
            gates = jnp.concatenate([gates[:, :2 * nh], m_rows,
                                     jnp.zeros((Bs * Ts, GATE_LANES - 3 * nh), F32)], axis=1)
            mix, c1, n1, m1 = mlstm_sample(proj, proj_z, gates, mlstm_head_norm[j], state_mlstm_C[j],
                                           state_mlstm_n[j].transpose(1, 0, 2), Ts, dqk, dv)
            att = attend_sample(proj_z, qb, proj_z, zb, cache_mem_k, cache_mem_v, layer, Ts)
            xs, hs = _finish(xs, mix, att, w_o, layer, norm_post[layer], g_next)
            cs_l.append(c1); ns_l.append(n1.transpose(1, 0, 2)); ms_l.append(m1[:, :, 0].T)

    return (xp.reshape(B, T, D), xs.reshape(Bs, Ts, D), jnp.stack(pool_p_l),
            mem_k.reshape(depth, B, S, XATT_HEADS, hd), mem_v.reshape(depth, B, S, XATT_HEADS, hd),
            jnp.stack(cp_l), jnp.stack(np_l), jnp.stack(mp_l),
            jnp.stack(pool_s_l), jnp.stack(cs_l), jnp.stack(ns_l), jnp.stack(ms_l))
```

```python
import functools

import jax
import jax.numpy as jnp
from jax import lax
from jax.experimental import pallas as pl
from jax.experimental.pallas import tpu as pltpu

F32 = jnp.float32
BF16 = jnp.bfloat16

NORM_EPS = 1e-6
GATE_SOFTCAP = 15.0
POOL_WINDOWS = (2, 4, 8, 16)
POOL_HIST = max(POOL_WINDOWS) - 1
HALO = POOL_HIST + 1
XATT_HEADS = 4
MLSTM_HEADS = 8
PAST_LEN = 16384
GATE_LANES = 128
MLSTM_CHUNK = 256
SAMPLE_BATCH_BLOCK = 8

V7X_VMEM_BYTES = 64 * 1024 * 1024
VMEM_CAP = V7X_VMEM_BYTES - 8 * 1024 * 1024


def _params(semantics, vmem_bytes):
    limit = int(min(max(vmem_bytes, 16 * 1024 * 1024), VMEM_CAP))
    return pltpu.CompilerParams(dimension_semantics=semantics, vmem_limit_bytes=limit)


def _sigmoid(x):
    return 1.0 / (1.0 + jnp.exp(-x))


def _silu(x):
    return x * _sigmoid(x)


def _rms(x, g):
    r = lax.rsqrt(jnp.mean(x * x, axis=-1, keepdims=True) + NORM_EPS)
    return x * r * g


def _rmsnorm_kernel(x_ref, g_ref, o_ref):
    o_ref[...] = _rms(x_ref[...], g_ref[...]).astype(o_ref.dtype)


def rmsnorm_cast(x, g, tm=256):
    M, D = x.shape
    tm = min(tm, M)
    return pl.pallas_call(
        _rmsnorm_kernel,
        grid=(M // tm,),
        in_specs=[pl.BlockSpec((tm, D), lambda i: (i, 0)),
                  pl.BlockSpec((1, D), lambda i: (0, 0))],
        out_specs=pl.BlockSpec((tm, D), lambda i: (i, 0)),
        out_shape=jax.ShapeDtypeStruct((M, D), BF16),
        compiler_params=_params(("parallel",), 8 * tm * D * 4),
        name="rmsnorm_cast",
    )(x, g.reshape(1, D))


def _post_kernel(*refs, with_next):
    x_ref, y_ref, g_ref = refs[:3]
    out = x_ref[...] + _rms(y_ref[...].astype(F32), g_ref[...])
    if with_next:
        gn_ref, o_ref, h_ref = refs[3:]
        h_ref[...] = _rms(out, gn_ref[...]).astype(h_ref.dtype)
    else:
        o_ref = refs[3]
    o_ref[...] = out


def post_norm_residual(x, y, g_post, g_next=None, tm=256):
    M, D = x.shape
    tm = min(tm, M)
    row = pl.BlockSpec((tm, D), lambda i: (i, 0))
    vec = pl.BlockSpec((1, D), lambda i: (0, 0))
    with_next = g_next is not None
    args = [x, y, g_post.reshape(1, D)] + ([g_next.reshape(1, D)] if with_next else [])
    out_shape = [jax.ShapeDtypeStruct((M, D), F32)] + ([jax.ShapeDtypeStruct((M, D), BF16)] if with_next else [])
    res = pl.pallas_call(
        functools.partial(_post_kernel, with_next=with_next),
        grid=(M // tm,),
        in_specs=[row, row, vec] + ([vec] if with_next else []),
        out_specs=[row] * len(out_shape),
        out_shape=out_shape,
        compiler_params=_params(("parallel",), 12 * tm * D * 4),
        name="post_norm_residual",
    )(*args)
    return (res[0], res[1]) if with_next else (res[0], None)


BF16_ROW_TILE = 16


def _cast_rows_kernel(*refs, shift):
    o_ref = refs[-1]
    if shift:
        m_ref, e_ref = refs[:2]
        o_ref[...] = jnp.concatenate([m_ref[shift:, :], e_ref[...]], axis=0).astype(o_ref.dtype)
    else:
        o_ref[...] = refs[0][...].astype(o_ref.dtype)


def cast_rows(wt, layer, row0, nrows, tn=1024, tk=1024):
    _, _, K = wt.shape
    tn, tk = min(tn, nrows), min(tk, K)
    shift = row0 % tn
    base = row0 - shift
    assert nrows % tn == 0 and K % tk == 0 and shift % BF16_ROW_TILE == 0
    in_specs = [pl.BlockSpec((None, tn, tk), lambda r, c: (layer, base // tn + r, c))]
    if shift:
        assert tn % shift == 0 and base % shift == 0
        in_specs.append(pl.BlockSpec((None, shift, tk), lambda r, c: (layer, (base + (r + 1) * tn) // shift, c)))
    return pl.pallas_call(
        functools.partial(_cast_rows_kernel, shift=shift),
        grid=(nrows // tn, K // tk),
        in_specs=in_specs,
        out_specs=pl.BlockSpec((tn, tk), lambda r, c: (r, c)),
        out_shape=jax.ShapeDtypeStruct((nrows, K), BF16),
        compiler_params=_params(("parallel", "parallel"), 6 * tn * tk * 4),
        name="cast_rows",
    )(*([wt] * len(in_specs)))


def _mm_kernel(*refs, ksizes):
    n_a = len(ksizes)
    a_refs, w_ref, o_ref = refs[:n_a], refs[n_a], refs[n_a + 1]
    acc = None
    off = 0
    for a_ref, ks in zip(a_refs, ksizes):
        part = jnp.dot(a_ref[...], w_ref[off:off + ks, :], preferred_element_type=F32)
        acc = part if acc is None else acc + part
        off += ks
    o_ref[...] = acc.astype(o_ref.dtype)


def matmul(a_list, w, out_dtype, layer=0, tm=1024, tn=1024):
    M = a_list[0].shape[0]
    _, K, N = w.shape
    ksizes = tuple(a.shape[1] for a in a_list)
    assert sum(ksizes) == K
    tm, tn = min(tm, M), min(tn, N)
    osz = jnp.dtype(out_dtype).itemsize
    vmem = 2 * (tm * K * 2 + K * tn * 2 + tm * tn * osz) + 2 * tm * tn * 4
    return pl.pallas_call(
        functools.partial(_mm_kernel, ksizes=ksizes),
        grid=(N // tn, M // tm),
        in_specs=[pl.BlockSpec((tm, ks), lambda j, i: (i, 0)) for ks in ksizes]
        + [pl.BlockSpec((None, K, tn), lambda j, i: (layer, 0, j))],
        out_specs=pl.BlockSpec((tm, tn), lambda j, i: (i, j)),
        out_shape=jax.ShapeDtypeStruct((M, N), out_dtype),
        compiler_params=_params(("parallel", "parallel"), vmem + (4 << 20)),
        name="matmul",
    )(*a_list, w)


def _mm_nt_kernel(a_ref, wt_ref, o_ref):
    o_ref[...] = lax.dot_general(a_ref[...], wt_ref[...], (((1,), (1,)), ((), ())),
                                 preferred_element_type=F32).astype(o_ref.dtype)


def matmul_nt(a, wt, out_dtype, tm=1024, tn=1024):
    M, K = a.shape
    N, _ = wt.shape
    tm, tn = min(tm, M), min(tn, N)
    osz = jnp.dtype(out_dtype).itemsize
    vmem = 2 * (tm * K * 2 + K * tn * 2 + tm * tn * osz) + 2 * tm * tn * 4
    return pl.pallas_call(
        _mm_nt_kernel,
        grid=(N // tn, M // tm),
        in_specs=[pl.BlockSpec((tm, K), lambda j, i: (i, 0)),
                  pl.BlockSpec((tn, K), lambda j, i: (j, 0))],
        out_specs=pl.BlockSpec((tm, tn), lambda j, i: (i, j)),
        out_shape=jax.ShapeDtypeStruct((M, N), out_dtype),
        compiler_params=_params(("parallel", "parallel"), vmem + (4 << 20)),
        name="matmul_nt",
    )(a, wt)


def _mem_kv_kernel(x_ref, g_ref, w_ref, k_ref, v_ref, k16_ref, v16_ref, h_scr):
    @pl.when(pl.program_id(2) == 0)
    def _():
        h_scr[...] = _rms(x_ref[...], g_ref[...]).astype(h_scr.dtype)

    kv = jnp.dot(h_scr[...], w_ref[...], preferred_element_type=F32)
    half = pl.num_programs(2) // 2

    @pl.when(pl.program_id(2) < half)
    def _():
        k_ref[...] = kv
        k16_ref[...] = kv.astype(k16_ref.dtype)

    @pl.when(pl.program_id(2) >= half)
    def _():
        v_ref[...] = kv
        v16_ref[...] = kv.astype(v16_ref.dtype)


def mem_kv(mem, g_mem, w_kv, tm=512, tn=512):
    M, D = mem.shape
    depth, _, N = w_kv.shape
    dx = N // 2
    tm, tn = min(tm, M), min(tn, dx)
    nj = N // tn
    half = nj // 2
    vmem = 2 * (tm * D * 4 + D * tn * 2 + 3 * tm * tn * 4) + tm * D * 2 + 2 * tm * tn * 4 + 2 * tm * D * 4
    out = jax.ShapeDtypeStruct((depth, M, dx), F32)
    out16 = jax.ShapeDtypeStruct((depth, M, dx), BF16)
    k_spec = pl.BlockSpec((None, tm, tn), lambda l, i, j: (l, i, jnp.minimum(j, half - 1)))
    v_spec = pl.BlockSpec((None, tm, tn), lambda l, i, j: (l, i, jnp.maximum(j - half, 0)))
    return pl.pallas_call(
        _mem_kv_kernel,
        grid=(depth, M // tm, nj),
        in_specs=[pl.BlockSpec((tm, D), lambda l, i, j: (i, 0)),
                  pl.BlockSpec((None, 1, D), lambda l, i, j: (l, 0, 0)),
                  pl.BlockSpec((None, D, tn), lambda l, i, j: (l, 0, j))],
        out_specs=[k_spec, v_spec, k_spec, v_spec],
        out_shape=[out, out, out16, out16],
        scratch_shapes=[pltpu.VMEM((tm, D), BF16)],
        compiler_params=_params(("parallel", "parallel", "arbitrary"), vmem + (4 << 20)),
        name="mem_kv",
    )(mem, g_mem.reshape(depth, 1, D), w_kv)


def _gates_kernel(h_ref, w_ref, b_ref, o_ref):
    nh = MLSTM_HEADS
    pre = lax.dot_general(h_ref[...], w_ref[...], (((1,), (1,)), ((), ())),
                          preferred_element_type=F32) + b_ref[...]
    capped = GATE_SOFTCAP * jnp.tanh(pre / GATE_SOFTCAP)
    log_sig = jnp.minimum(capped, 0.0) - jnp.log(1.0 + jnp.exp(-jnp.abs(capped)))
    lane = lax.broadcasted_iota(jnp.int32, capped.shape, 1)
    o_ref[...] = jnp.where(lane < nh, capped, log_sig)


def mlstm_gates(h, w_gates, bias, tm=512):
    M, D = h.shape
    tm = min(tm, M)
    return pl.pallas_call(
        _gates_kernel,
        grid=(M // tm,),
        in_specs=[pl.BlockSpec((tm, D), lambda i: (i, 0)),
                  pl.BlockSpec((GATE_LANES, D), lambda i: (0, 0)),
                  pl.BlockSpec((1, GATE_LANES), lambda i: (0, 0))],
        out_specs=pl.BlockSpec((tm, GATE_LANES), lambda i: (i, 0)),
        out_shape=jax.ShapeDtypeStruct((M, GATE_LANES), F32),
        compiler_params=_params(("parallel",), 4 * tm * D * 2 + 4 * D * GATE_LANES * 2),
        name="mlstm_gates",
    )(h, w_gates, bias)


def _group_mm_kernel(a_ref, w_ref, s_ref, z_ref, o_ref):
    mixed = jnp.dot(a_ref[...], w_ref[...], preferred_element_type=F32) * s_ref[...]
    o_ref[...] = (mixed * _silu(z_ref[...].astype(F32))).astype(o_ref.dtype)


def group_mix_gate(pooled, w_group, scale, proj, z_off, tm=512):
    M, D = pooled.shape
    G, Gs, _ = w_group.shape
    tm = min(tm, M)
    zb = z_off // Gs
    return pl.pallas_call(
        _group_mm_kernel,
        grid=(G, M // tm),
        in_specs=[pl.BlockSpec((tm, Gs), lambda g, i: (i, g)),
                  pl.BlockSpec((None, Gs, Gs), lambda g, i: (g, 0, 0)),
                  pl.BlockSpec((1, Gs), lambda g, i: (0, g)),
                  pl.BlockSpec((tm, Gs), lambda g, i: (i, zb + g))],
        out_specs=pl.BlockSpec((tm, Gs), lambda g, i: (i, g)),
        out_shape=jax.ShapeDtypeStruct((M, D), BF16),
        compiler_params=_params(("parallel", "parallel"), 6 * tm * Gs * 4 + 4 * Gs * Gs * 2),
        name="group_mix_gate",
    )(pooled, w_group, scale.reshape(1, D), proj)


def _window_sums(ext, levels):
    sums = []
    cur = ext
    for lv in range(levels):
        cur = cur + pltpu.roll(cur, 1 << lv, 0)
        sums.append(cur)
    return sums


def _pool_mix_prompt_kernel(u_ref, halo_ref, hist_ref, zlo_ref, zhi_ref, w_ref, s_ref, o_ref, *, pos0, tt):
    i = pl.program_id(1)
    ng = len(POOL_WINDOWS)
    gs = u_ref.shape[1] // ng
    row = lax.broadcasted_iota(jnp.int32, (tt, 1), 0) + i * tt
    pos1 = (row + (pos0 + 1)).astype(F32)
    first = i == 0
    for g, w in enumerate(POOL_WINDOWS):
        lo, hi = g * gs, (g + 1) * gs
        u = u_ref[:, lo:hi].astype(F32)
        halo = jnp.where(first, hist_ref[:, lo:hi], halo_ref[:, lo:hi].astype(F32))
        ext = jnp.concatenate([halo, u], axis=0)
        win = _window_sums(ext, g + 1)[-1][HALO:, :]
        inv_cnt = 1.0 / jnp.minimum(float(w), pos1)
        pooled = (win * inv_cnt - u).astype(BF16)
        mixed = jnp.dot(pooled, w_ref[g], preferred_element_type=F32) * s_ref[:, lo:hi]
        z_ref, zg = (zlo_ref, g) if g < ng // 2 else (zhi_ref, g - ng // 2)
        z = z_ref[:, zg * gs:(zg + 1) * gs].astype(F32)
        o_ref[:, lo:hi] = (mixed * _silu(z)).astype(o_ref.dtype)


def pool_mix_prompt(proj, hist, w_group, scale, batch, pos0, z_off, tt=512):
    D = hist.shape[-1]
    G, Gs, _ = w_group.shape
    T = proj.shape[0] // batch
    tt = min(tt, T)
    nt = T // tt
    dh = D // 2
    assert z_off % dh == 0 and G == len(POOL_WINDOWS)
    zb = z_off // dh
    hist16 = jnp.concatenate([jnp.zeros((batch, 1, D), F32), hist.astype(F32)], axis=1)
    vmem = 2 * (3 * tt * D * 2 + G * Gs * Gs * 2) + 12 * (tt + HALO) * Gs * 4
    return pl.pallas_call(
        functools.partial(_pool_mix_prompt_kernel, pos0=pos0, tt=tt),
        grid=(batch, nt),
        in_specs=[pl.BlockSpec((tt, D), lambda b, i: (b * nt + i, 0)),
                  pl.BlockSpec((HALO, D),
                               lambda b, i: (jnp.maximum((b * nt + i) * (tt // HALO) - 1, 0), 0)),
                  pl.BlockSpec((None, HALO, D), lambda b, i: (b, 0, 0)),
                  pl.BlockSpec((tt, dh), lambda b, i: (b * nt + i, zb)),
                  pl.BlockSpec((tt, dh), lambda b, i: (b * nt + i, zb + 1)),
                  pl.BlockSpec((G, Gs, Gs), lambda b, i: (0, 0, 0)),
                  pl.BlockSpec((1, D), lambda b, i: (0, 0))],
        out_specs=pl.BlockSpec((tt, D), lambda b, i: (b * nt + i, 0)),
        out_shape=jax.ShapeDtypeStruct((batch * T, D), BF16),
        compiler_params=_params(("parallel", "parallel"), vmem + (4 << 20)),
        name="pool_mix_prompt",
    )(proj, proj, hist16, proj, proj, w_group, scale.reshape(1, D))


def _pool_sample_kernel(ext_ref, o_ref, *, pos0):
    steps = o_ref.shape[0]
    gs = ext_ref.shape[2] // len(POOL_WINDOWS)
    for g, w in enumerate(POOL_WINDOWS):
        lo, hi = g * gs, (g + 1) * gs
        for t in range(steps):
            cur = ext_ref[POOL_HIST + t, :, lo:hi]
            win = cur
            for r in range(1, w):
                win = win + ext_ref[POOL_HIST + t - r, :, lo:hi]
            cnt = float(min(w, pos0 + t + 1))
            o_ref[t, :, lo:hi] = (win / cnt - cur).astype(o_ref.dtype)


def pool_sample(ext_t, pos0, bt=32):
    R, B, D = ext_t.shape
    T = R - POOL_HIST
    bt = min(bt, B)
    return pl.pallas_call(
        functools.partial(_pool_sample_kernel, pos0=pos0),
        grid=(B // bt,),
        in_specs=[pl.BlockSpec((R, bt, D), lambda i: (0, i, 0))],
        out_specs=pl.BlockSpec((T, bt, D), lambda i: (0, i, 0)),
        out_shape=jax.ShapeDtypeStruct((T, B, D), BF16),
        compiler_params=_params(("parallel",), 3 * R * bt * D * 4),
        name="pool_sample",
    )(ext_t)


def _softmax_rows(s):
    m = jnp.max(s, axis=-1, keepdims=True)
    p = jnp.exp(s - m)
    return p, jnp.sum(p, axis=-1, keepdims=True)


def _attn_prompt_kernel(q_ref, k_ref, v_ref, z_ref, o_ref):
    hd = q_ref.shape[1] // XATT_HEADS
    scale = hd ** -0.5
    for h in range(XATT_HEADS):
        lo, hi = h * hd, (h + 1) * hd
        s = lax.dot_general(q_ref[:, lo:hi], k_ref[:, lo:hi], (((1,), (1,)), ((), ())),
                            preferred_element_type=F32) * scale
        p, l = _softmax_rows(s)
        o = jnp.dot(p.astype(BF16), v_ref[:, lo:hi], preferred_element_type=F32) / l
        o_ref[:, lo:hi] = (o * _silu(z_ref[:, lo:hi].astype(F32))).astype(o_ref.dtype)


def attend_prompt(q_arr, qb, z_arr, zb, k, v, tq=512):
    B, S, DX = k.shape
    T = q_arr.shape[0] // B
    tq = min(tq, T)
    nq = T // tq
    return pl.pallas_call(
        _attn_prompt_kernel,
        grid=(B, nq),
        in_specs=[pl.BlockSpec((tq, DX), lambda b, i: (b * nq + i, qb)),
                  pl.BlockSpec((None, S, DX), lambda b, i: (b, 0, 0)),
                  pl.BlockSpec((None, S, DX), lambda b, i: (b, 0, 0)),
                  pl.BlockSpec((tq, DX), lambda b, i: (b * nq + i, zb))],
        out_specs=pl.BlockSpec((tq, DX), lambda b, i: (b * nq + i, 0)),
        out_shape=jax.ShapeDtypeStruct((B * T, DX), BF16),
        compiler_params=_params(("parallel", "parallel"), 8 * tq * DX * 4 + 8 * S * DX * 2),
        name="attend_prompt",
    )(q_arr, k, v, z_arr)


def _attn_sample_kernel(q_ref, k_hbm, v_hbm, z_ref, o_ref, kbuf, vbuf, sem, *, steps, layer, bb):
    i = pl.program_id(0)
    rows, dx = q_ref.shape
    hd = dx // XATT_HEADS
    scale = hd ** -0.5

    def head_copies(step, slot):
        out = []
        for t, (src, dst) in enumerate(((k_hbm, kbuf), (v_hbm, vbuf))):
            for h in range(XATT_HEADS):
                out.append(pltpu.make_async_copy(src.at[layer, pl.ds(step * bb, bb), :, h, :],
                                                 dst.at[slot, h], sem.at[t, slot, h]))
        return out

    @pl.when(i == 0)
    def _():
        for cp in head_copies(0, 0):
            cp.start()

    @pl.when(i + 1 < pl.num_programs(0))
    def _():
        for cp in head_copies(i + 1, (i + 1) % 2):
            cp.start()

    slot = i % 2
    for cp in head_copies(i, slot):
        cp.wait()

    row_batch = lax.broadcasted_iota(jnp.int32, (rows, 1), 0) // steps
    pairs = [(h, j) for h in range(XATT_HEADS) for j in range(bb)]
    scores = [lax.dot_general(q_ref[:, h * hd:(h + 1) * hd], kbuf[slot, h, j].astype(BF16),
                              (((1,), (1,)), ((), ())), preferred_element_type=F32) * scale
              for h, j in pairs]
    probs = [_softmax_rows(s) for s in scores]
    outs = [jnp.dot(p.astype(BF16), vbuf[slot, h, j].astype(BF16), preferred_element_type=F32) / l
            for (h, j), (p, l) in zip(pairs, probs)]
    for h in range(XATT_HEADS):
        lo, hi = h * hd, (h + 1) * hd
        acc = jnp.zeros((rows, hd), F32)
        for j in range(bb):
            acc = jnp.where(row_batch == j, outs[h * bb + j], acc)
        o_ref[:, lo:hi] = (acc * _silu(z_ref[:, lo:hi].astype(F32))).astype(o_ref.dtype)


def attend_sample(q_arr, qb, z_arr, zb, k, v, layer, steps, bb=4):
    _, B, S, nh, hd = k.shape
    assert nh == XATT_HEADS
    DX = nh * hd
    rows = bb * steps
    slot_bytes = nh * bb * S * hd * 4
    return pl.pallas_call(
        functools.partial(_attn_sample_kernel, steps=steps, layer=layer, bb=bb),
        grid=(B // bb,),
        in_specs=[pl.BlockSpec((rows, DX), lambda i: (i, qb)),
                  pl.BlockSpec(memory_space=pl.ANY), pl.BlockSpec(memory_space=pl.ANY),
                  pl.BlockSpec((rows, DX), lambda i: (i, zb))],
        out_specs=pl.BlockSpec((rows, DX), lambda i: (i, 0)),
        out_shape=jax.ShapeDtypeStruct((B * steps, DX), BF16),
        scratch_shapes=[pltpu.VMEM((2, nh, bb, S, hd), F32), pltpu.VMEM((2, nh, bb, S, hd), F32),
                        pltpu.SemaphoreType.DMA((2, 2, nh))],
        compiler_params=_params(("arbitrary",), 4 * slot_bytes + (12 << 20)),
        name="attend_sample",
    )(q_arr, k, v, z_arr)


def _head_output(h, o, z, g):
    return _sigmoid(o) * _rms(h, g) * _silu(z)


def _mlstm_prompt_kernel(q_ref, k_ref, v_ref, o_ref, zlo_ref, zhi_ref, g_ref, gh_ref,
                         y_ref, c_out, n_out, m_out, ct_scr, n_scr, m_scr):
    c = pl.program_id(1)
    nh = MLSTM_HEADS
    L = q_ref.shape[0]
    dqk = q_ref.shape[1] // nh
    dv = v_ref.shape[1] // nh
    qscale = dqk ** -0.5

    @pl.when(c == 0)
    def _():
        ct_scr[...] = jnp.zeros_like(ct_scr)
        n_scr[...] = jnp.zeros_like(n_scr)
        m_scr[...] = jnp.zeros_like(m_scr)

    r_i = lax.broadcasted_iota(jnp.int32, (L, L), 0)
    c_i = lax.broadcasted_iota(jnp.int32, (L, L), 1)
    causal = c_i <= r_i
    eye = (c_i == r_i).astype(F32)
    tril = causal.astype(F32)
    triu = (r_i <= c_i).astype(F32)
    gates = g_ref[...]

    for h in range(nh):
        q = (q_ref[:, h * dqk:(h + 1) * dqk].astype(F32) * qscale)
        qb = q.astype(BF16)
        k = k_ref[:, h * dqk:(h + 1) * dqk]
        v = v_ref[:, h * dv:(h + 1) * dv]
        ig = gates[:, h:h + 1]
        lf = gates[:, nh + h:nh + h + 1]
        m_prev = m_scr[h:h + 1, 0:1]
        n_prev = n_scr[h:h + 1, :]
        ct = ct_scr[h]

        lf_row = jnp.sum(lf * eye, axis=0, keepdims=True)
        ig_row = jnp.sum(ig * eye, axis=0, keepdims=True)
        b_col = jnp.sum(tril * lf_row, axis=1, keepdims=True)
        b_row = jnp.sum(triu * lf, axis=0, keepdims=True)
        d = jnp.where(causal, b_col - b_row + ig_row, -jnp.inf)
        inter = b_col + m_prev
        m_t = jnp.maximum(inter, jnp.max(d, axis=1, keepdims=True))
        w = jnp.exp(d - m_t)
        w_inter = jnp.exp(inter - m_t)
        s = lax.dot_general(qb, k, (((1,), (1,)), ((), ())), preferred_element_type=F32) * w
        num = (jnp.dot(s.astype(BF16), v, preferred_element_type=F32)
               + w_inter * jnp.dot(qb, ct.astype(BF16), preferred_element_type=F32))
        den = (jnp.sum(s, axis=1, keepdims=True)
               + w_inter * jnp.sum(q * n_prev, axis=1, keepdims=True))
        hh = num / jnp.maximum(jnp.abs(den), jnp.exp(-m_t))

        b_last = b_col[L - 1:L, :]
        m_new = m_t[L - 1:L, :]
        w_state = jnp.exp(b_last - b_col + ig - m_new)
        decay = jnp.exp(b_last + m_prev - m_new)
        kw = k.astype(F32) * w_state
        ct_scr[h] = decay * ct + lax.dot_general(kw.astype(BF16), v, (((0,), (0,)), ((), ())),
                                                 preferred_element_type=F32)
        n_scr[h:h + 1, :] = decay * n_prev + jnp.sum(kw, axis=0, keepdims=True)
        m_scr[h:h + 1, :] = jnp.broadcast_to(m_new, (1, m_scr.shape[1]))

        lo, hi = h * dv, (h + 1) * dv
        z_ref, zh = (zlo_ref, h) if h < nh // 2 else (zhi_ref, h - nh // 2)
        y_ref[:, lo:hi] = _head_output(hh, o_ref[:, lo:hi].astype(F32),
                                       z_ref[:, zh * dv:(zh + 1) * dv].astype(F32),
                                       gh_ref[:, lo:hi]).astype(y_ref.dtype)

    @pl.when(c == pl.num_programs(1) - 1)
    def _():
        for h in range(nh):
            c_out[h] = ct_scr[h].T
        n_out[...] = n_scr[...]
        m_out[...] = m_scr[...]


def mlstm_prompt(proj, proj_z, gates, g_head, batch, dqk, dv):
    nh = MLSTM_HEADS
    T = proj.shape[0] // batch
    L = min(MLSTM_CHUNK, T)
    nc = T // L
    wq, wv = nh * dqk, nh * dv
    assert wv == 2 * wq
    rows = lambda b, c: b * nc + c
    return pl.pallas_call(
        _mlstm_prompt_kernel,
        grid=(batch, nc),
        in_specs=[pl.BlockSpec((L, wq), lambda b, c: (rows(b, c), 0)),
                  pl.BlockSpec((L, wq), lambda b, c: (rows(b, c), 1)),
                  pl.BlockSpec((L, wv), lambda b, c: (rows(b, c), 1)),
                  pl.BlockSpec((L, wv), lambda b, c: (rows(b, c), 2)),
                  pl.BlockSpec((L, wq), lambda b, c: (rows(b, c), 1)),
                  pl.BlockSpec((L, wq), lambda b, c: (rows(b, c), 2)),
                  pl.BlockSpec((L, GATE_LANES), lambda b, c: (rows(b, c), 0)),
                  pl.BlockSpec((1, wv), lambda b, c: (0, 0))],
        out_specs=[pl.BlockSpec((L, wv), lambda b, c: (rows(b, c), 0)),
                   pl.BlockSpec((None, nh, dv, dqk), lambda b, c: (b, 0, 0, 0)),
                   pl.BlockSpec((None, nh, dqk), lambda b, c: (b, 0, 0)),
                   pl.BlockSpec((None, nh, GATE_LANES), lambda b, c: (b, 0, 0))],
        out_shape=[jax.ShapeDtypeStruct((batch * T, wv), BF16),
                   jax.ShapeDtypeStruct((batch, nh, dv, dqk), F32),
                   jax.ShapeDtypeStruct((batch, nh, dqk), F32),
                   jax.ShapeDtypeStruct((batch, nh, GATE_LANES), F32)],
        scratch_shapes=[pltpu.VMEM((nh, dqk, dv), F32),
                        pltpu.VMEM((nh, dqk), F32),
                        pltpu.VMEM((nh, GATE_LANES), F32)],
        compiler_params=_params(("parallel", "arbitrary"),
                                24 * L * wv + 5 * nh * dv * dqk * 4 + (16 << 20)),
        name="mlstm_prompt",
    )(proj, proj, proj, proj, proj_z, proj_z, gates, g_head.reshape(1, wv))


def _lane_pick(x, lane, idx):
    return jnp.sum(jnp.where(lane == idx, x, 0.0), axis=1, keepdims=True)


def _mlstm_sample_kernel(q_ref, k_ref, v_ref, o_ref, z_ref, g_ref, gh_ref, c_ref, n_ref,
                         y_ref, c_out, n_out, m_out, *, steps):
    head = pl.program_id(1)
    nh = MLSTM_HEADS
    R, dqk = q_ref.shape
    nb = R // steps
    qscale = dqk ** -0.5

    gates = g_ref[...]
    lane = lax.broadcasted_iota(jnp.int32, gates.shape, 1)
    ig = _lane_pick(gates, lane, head)
    lf = _lane_pick(gates, lane, nh + head)
    m_prev = _lane_pick(gates, lane, 2 * nh + head)

    r_i = lax.broadcasted_iota(jnp.int32, (R, R), 0)
    c_i = lax.broadcasted_iota(jnp.int32, (R, R), 1)
    same = (r_i // steps) == (c_i // steps)
    causal = same & (c_i <= r_i)
    eye = (c_i == r_i).astype(F32)
    lower = causal.astype(F32)
    upper = (same & (r_i <= c_i)).astype(F32)
    last_of_row_batch = (c_i == (r_i // steps) * steps + (steps - 1)).astype(F32)
    row_batch = lax.broadcasted_iota(jnp.int32, (R, 1), 0) // steps

    q = q_ref[...].astype(F32) * qscale
    qb = q.astype(BF16)
    k = k_ref[...]
    v = v_ref[...]

    lf_row = jnp.sum(lf * eye, axis=0, keepdims=True)
    ig_row = jnp.sum(ig * eye, axis=0, keepdims=True)
    b_col = jnp.sum(lower * lf_row, axis=1, keepdims=True)
    b_row = jnp.sum(upper * lf, axis=0, keepdims=True)
    d = jnp.where(causal, b_col - b_row + ig_row, -jnp.inf)
    inter = b_col + m_prev
    m_t = jnp.maximum(inter, jnp.max(d, axis=1, keepdims=True))
    w = jnp.exp(d - m_t)
    w_inter = jnp.exp(inter - m_t)
    s = lax.dot_general(qb, k, (((1,), (1,)), ((), ())), preferred_element_type=F32) * w

    qc = jnp.zeros((R, v_ref.shape[1]), F32)
    n_rows = jnp.zeros((R, dqk), F32)
    for j in range(nb):
        mine = row_batch == j
        qcj = lax.dot_general(qb, c_ref[j].astype(BF16), (((1,), (1,)), ((), ())),
                              preferred_element_type=F32)
        qc = jnp.where(mine, qcj, qc)
        n_rows = jnp.where(mine, n_ref[j:j + 1, :], n_rows)

    num = jnp.dot(s.astype(BF16), v, preferred_element_type=F32) + w_inter * qc
    den = jnp.sum(s, axis=1, keepdims=True) + w_inter * jnp.sum(q * n_rows, axis=1, keepdims=True)
    hh = num / jnp.maximum(jnp.abs(den), jnp.exp(-m_t))
    y_ref[...] = _head_output(hh, o_ref[...].astype(F32), z_ref[...].astype(F32),
                              gh_ref[...]).astype(y_ref.dtype)

    m_row = jnp.sum(m_t * eye, axis=0, keepdims=True)
    b_last = jnp.sum(last_of_row_batch * b_row, axis=1, keepdims=True)
    m_new = jnp.sum(last_of_row_batch * m_row, axis=1, keepdims=True)
    w_state = jnp.exp(b_last - b_col + ig - m_new)
    decay = jnp.exp(b_last + m_prev - m_new)
    vw = v.astype(F32) * w_state
    kw = k.astype(F32) * w_state
    for j in range(nb):
        mine = row_batch == j
        r0 = j * steps
        dj = decay[r0:r0 + 1, :]
        upd = lax.dot_general(jnp.where(mine, vw, 0.0).astype(BF16), k, (((0,), (0,)), ((), ())),
                              preferred_element_type=F32)
        c_out[j] = dj * c_ref[j] + upd
        n_out[j:j + 1, :] = dj * n_ref[j:j + 1, :] + jnp.sum(jnp.where(mine, kw, 0.0), axis=0, keepdims=True)
        m_out[j:j + 1, :] = jnp.broadcast_to(m_new[r0:r0 + 1, :], (1, m_out.shape[1]))


def mlstm_sample(proj, proj_z, gates, g_head, c0, n0, steps, dqk, dv):
    nh = MLSTM_HEADS
    B = c0.shape[0]
    nb = SAMPLE_BATCH_BLOCK
    R = nb * steps
    kq = nh
    kv = (2 * nh * dqk) // dv
    ko = kv + nh
    kz = (nh * dqk) // dv
    return pl.pallas_call(
        functools.partial(_mlstm_sample_kernel, steps=steps),
        grid=(B // nb, nh),
        in_specs=[pl.BlockSpec((R, dqk), lambda i, h: (i, h)),
                  pl.BlockSpec((R, dqk), lambda i, h: (i, kq + h)),
                  pl.BlockSpec((R, dv), lambda i, h: (i, kv + h)),
                  pl.BlockSpec((R, dv), lambda i, h: (i, ko + h)),
                  pl.BlockSpec((R, dv), lambda i, h: (i, kz + h)),
                  pl.BlockSpec((R, GATE_LANES), lambda i, h: (i, 0)),
                  pl.BlockSpec((1, dv), lambda i, h: (0, h)),
                  pl.BlockSpec((nb, None, dv, dqk), lambda i, h: (i, h, 0, 0)),
                  pl.BlockSpec((None, nb, dqk), lambda i, h: (h, i, 0))],
        out_specs=[pl.BlockSpec((R, dv), lambda i, h: (i, h)),
                   pl.BlockSpec((nb, None, dv, dqk), lambda i, h: (i, h, 0, 0)),
                   pl.BlockSpec((None, nb, dqk), lambda i, h: (h, i, 0)),
                   pl.BlockSpec((None, nb, GATE_LANES), lambda i, h: (h, i, 0))],
        out_shape=[jax.ShapeDtypeStruct((B * steps, nh * dv), BF16),
                   jax.ShapeDtypeStruct(c0.shape, F32),
                   jax.ShapeDtypeStruct((nh, B, dqk), F32),
                   jax.ShapeDtypeStruct((nh, B, GATE_LANES), F32)],
        compiler_params=_params(("parallel", "parallel"), 6 * nb * dv * dqk * 4 + (8 << 20)),
        name="mlstm_sample",
    )(proj, proj, proj, proj, proj_z, gates, g_head.reshape(1, nh * dv), c0, n0)


def _finish(x, h_mix, h_att, w_out, layer, g_post, g_next):
    y = matmul([h_mix, h_att], w_out, BF16, layer=layer, tm=512, tn=1024)
    return post_norm_residual(x, y, g_post, g_next)


def _mlstm_weights(w_in, j, D):
    nh = MLSTM_HEADS
    g0 = 3 * D
    wt = jnp.swapaxes(w_in, 1, 2)
    w_main = cast_rows(wt, j, 0, g0)
    w_z = cast_rows(wt, j, g0 + 2 * nh, wt.shape[1] - g0 - 2 * nh)
    w_gates = jnp.pad(wt[j, g0:g0 + 2 * nh, :], ((0, GATE_LANES - 2 * nh), (0, 0))).astype(BF16)
    return w_main, w_z, w_gates


def kernel(x_prompt, x_sample, mem_prompt, state_pool, state_mlstm_C, state_mlstm_n, state_mlstm_m,
           cache_mem_k, cache_mem_v, norm_pre, norm_post, norm_mem, w_mem_kv, w_out,
           w_in_pool, w_pool_group, pool_scale, w_in_mlstm, b_igate, b_fgate, mlstm_head_norm):
    B, T, D = x_prompt.shape
    Bs, Ts, _ = x_sample.shape
    S = mem_prompt.shape[1]
    depth = w_out.shape[0]
    nh = MLSTM_HEADS
    dx = D // 2
    dv = D // nh
    dqk = dv // 2
    hd = dx // XATT_HEADS

    xp = x_prompt.reshape(B * T, D)
    xs = x_sample.reshape(Bs * Ts, D)
    mem = mem_prompt.reshape(B * S, D)
    hp = rmsnorm_cast(xp, norm_pre[0])
    hs = rmsnorm_cast(xs, norm_pre[0])

    mem_k, mem_v, mem_k16, mem_v16 = mem_kv(mem, norm_mem, w_mem_kv.astype(BF16))
    w_o = w_out.astype(BF16)
    w_pool = w_in_pool.astype(BF16)

    pool_p_l, pool_s_l = [], []
    cp_l, np_l, mp_l, cs_l, ns_l, ms_l = [], [], [], [], [], []
    for layer in range(depth):
        j = layer // 2
        g_next = norm_pre[layer + 1] if layer + 1 < depth else None
        mk16 = mem_k16[layer].reshape(B, S, dx)
        mv16 = mem_v16[layer].reshape(B, S, dx)
        if layer % 2 == 0:
            w_g = w_pool_group[j].astype(BF16)
            z_mix, qb, zb = D + dx, D // dx, (2 * D + dx) // dx
            proj = matmul([hp], w_pool, BF16, layer=j)
            mix = pool_mix_prompt(proj, jnp.zeros((B, POOL_HIST, D), F32), w_g, pool_scale[j], B, 0, z_mix)
            att = attend_prompt(proj, qb, proj, zb, mk16, mv16)
            xp, hp = _finish(xp, mix, att, w_o, layer, norm_post[layer], g_next)
            pool_p_l.append(proj.reshape(B, T, -1)[:, T - POOL_HIST:, :D].astype(F32))
            proj = matmul([hs], w_pool, BF16, layer=j)
            u = proj[:, :D].astype(F32).reshape(Bs, Ts, D)
            ext = jnp.concatenate([state_pool[j], u], axis=1)
            pooled = pool_sample(ext.transpose(1, 0, 2), PAST_LEN)
            pooled = pooled.transpose(1, 0, 2).reshape(Bs * Ts, D)
            mix = group_mix_gate(pooled, w_g, pool_scale[j], proj, z_mix)
            att = attend_sample(proj, qb, proj, zb, cache_mem_k, cache_mem_v, layer, Ts)
            xs, hs = _finish(xs, mix, att, w_o, layer, norm_post[layer], g_next)
            pool_s_l.append(ext[:, Ts:])
        else:
            bias = jnp.pad(jnp.concatenate([b_igate[j], b_fgate[j]]), (0, GATE_LANES - 2 * nh))
            bias = bias.reshape(1, GATE_LANES)
            qb, zb = 0, (dx + D) // dx
            w_main, w_z, w_gates = _mlstm_weights(w_in_mlstm, j, D)
            proj = matmul_nt(hp, w_main, BF16)
            proj_z = matmul_nt(hp, w_z, BF16)
            gates = mlstm_gates(hp, w_gates, bias)
            mix, c1, n1, m1 = mlstm_prompt(proj, proj_z, gates, mlstm_head_norm[j], B, dqk, dv)
            att = attend_prompt(proj_z, qb, proj_z, zb, mk16, mv16)
            xp, hp = _finish(xp, mix, att, w_o, layer, norm_post[layer], g_next)
            cp_l.append(c1); np_l.append(n1); mp_l.append(m1[:, :, 0])
            proj = matmul_nt(hs, w_main, BF16)
            proj_z = matmul_nt(hs, w_z, BF16)
            gates = mlstm_gates(hs, w_gates, bias)
            m_rows = jnp.repeat(state_mlstm_m[j], Ts, axis=0)
            gates = jnp.concatenate([gates[:, :2 * nh], m_rows,
                                     jnp.zeros((Bs * Ts, GATE_LANES - 3 * nh), F32)], axis=1)
            mix, c1, n1, m1 = mlstm_sample(proj, proj_z, gates, mlstm_head_norm[j], state_mlstm_C[j],
                                           state_mlstm_n[j].transpose(1, 0, 2), Ts, dqk, dv)
            att = attend_sample(proj_z, qb, proj_z, zb, cache_mem_k, cache_mem_v, layer, Ts)
            xs, hs = _finish(xs, mix, att, w_o, layer, norm_post[layer], g_next)
            cs_l.append(c1); ns_l.append(n1.transpose(1, 0, 2)); ms_l.append(m1[:, :, 0].T)

    return (xp.reshape(B, T, D), xs.reshape(Bs, Ts, D), jnp.stack(pool_p_l),
            mem_k.reshape(depth, B, S, XATT_HEADS, hd), mem_v.reshape(depth, B, S, XATT_HEADS, hd),
            jnp.stack(cp_l), jnp.stack(np_l), jnp.stack(mp_l),
            jnp.stack(pool_s_l), jnp.stack(cs_l), jnp.stack(ns_l), jnp.stack(ms_l))
```

```python
import functools

import jax
import jax.numpy as jnp
from jax import lax
from jax.experimental import pallas as pl
from jax.experimental.pallas import tpu as pltpu

F32 = jnp.float32
BF16 = jnp.bfloat16

NORM_EPS = 1e-6
GATE_SOFTCAP = 15.0
POOL_WINDOWS = (2, 4, 8, 16)
POOL_HIST = max(POOL_WINDOWS) - 1
HALO = POOL_HIST + 1
XATT_HEADS = 4
MLSTM_HEADS = 8
PAST_LEN = 16384
GATE_LANES = 128
MLSTM_CHUNK = 256
SAMPLE_BATCH_BLOCK = 8

V7X_VMEM_BYTES = 64 * 1024 * 1024
VMEM_CAP = V7X_VMEM_BYTES - 8 * 1024 * 1024


def _params(semantics, vmem_bytes):
    limit = int(min(max(vmem_bytes, 16 * 1024 * 1024), VMEM_CAP))
    return pltpu.CompilerParams(dimension_semantics=semantics, vmem_limit_bytes=limit)


NEG_LOG2E = -1.4426950408889634


def _sigmoid(x):
    return 1.0 / (1.0 + jnp.exp2(x * NEG_LOG2E))


def _silu(x):
    return x * _sigmoid(x)


def _rms(x, g):
    r = lax.rsqrt(jnp.mean(x * x, axis=-1, keepdims=True) + NORM_EPS)
    return x * r * g


def _rmsnorm_kernel(x_ref, g_ref, o_ref):
    o_ref[...] = _rms(x_ref[...], g_ref[...]).astype(o_ref.dtype)


def rmsnorm_cast(x, g, tm=256):
    M, D = x.shape
    tm = min(tm, M)
    return pl.pallas_call(
        _rmsnorm_kernel,
        grid=(M // tm,),
        in_specs=[pl.BlockSpec((tm, D), lambda i: (i, 0)),
                  pl.BlockSpec((1, D), lambda i: (0, 0))],
        out_specs=pl.BlockSpec((tm, D), lambda i: (i, 0)),
        out_shape=jax.ShapeDtypeStruct((M, D), BF16),
        compiler_params=_params(("parallel",), 8 * tm * D * 4),
        name="rmsnorm_cast",
    )(x, g.reshape(1, D))


def _post_kernel(*refs, with_next):
    x_ref, y_ref, g_ref = refs[:3]
    out = x_ref[...] + _rms(y_ref[...].astype(F32), g_ref[...])
    if with_next:
        gn_ref, o_ref, h_ref = refs[3:]
        h_ref[...] = _rms(out, gn_ref[...]).astype(h_ref.dtype)
    else:
        o_ref = refs[3]
    o_ref[...] = out


def post_norm_residual(x, y, g_post, g_next=None, tm=256):
    M, D = x.shape
    tm = min(tm, M)
    row = pl.BlockSpec((tm, D), lambda i: (i, 0))
    vec = pl.BlockSpec((1, D), lambda i: (0, 0))
    with_next = g_next is not None
    args = [x, y, g_post.reshape(1, D)] + ([g_next.reshape(1, D)] if with_next else [])
    out_shape = [jax.ShapeDtypeStruct((M, D), F32)] + ([jax.ShapeDtypeStruct((M, D), BF16)] if with_next else [])
    res = pl.pallas_call(
        functools.partial(_post_kernel, with_next=with_next),
        grid=(M // tm,),
        in_specs=[row, row, vec] + ([vec] if with_next else []),
        out_specs=[row] * len(out_shape),
        out_shape=out_shape,
        compiler_params=_params(("parallel",), 12 * tm * D * 4),
        name="post_norm_residual",
    )(*args)
    return (res[0], res[1]) if with_next else (res[0], None)


BF16_ROW_TILE = 16


def _mm_kernel(*refs, ksizes):
    n_a = len(ksizes)
    a_refs, w_ref, o_ref = refs[:n_a], refs[n_a], refs[n_a + 1]
    acc = None
    off = 0
    for a_ref, ks in zip(a_refs, ksizes):
        part = jnp.dot(a_ref[...], w_ref[off:off + ks, :], preferred_element_type=F32)
        acc = part if acc is None else acc + part
        off += ks
    o_ref[...] = acc.astype(o_ref.dtype)


def matmul(a_list, w, out_dtype, layer=0, tm=1024, tn=1024):
    M = a_list[0].shape[0]
    _, K, N = w.shape
    ksizes = tuple(a.shape[1] for a in a_list)
    assert sum(ksizes) == K
    tm, tn = min(tm, M), min(tn, N)
    osz = jnp.dtype(out_dtype).itemsize
    vmem = 2 * (tm * K * 2 + K * tn * 2 + tm * tn * osz) + 2 * tm * tn * 4
    return pl.pallas_call(
        functools.partial(_mm_kernel, ksizes=ksizes),
        grid=(N // tn, M // tm),
        in_specs=[pl.BlockSpec((tm, ks), lambda j, i: (i, 0)) for ks in ksizes]
        + [pl.BlockSpec((None, K, tn), lambda j, i: (layer, 0, j))],
        out_specs=pl.BlockSpec((tm, tn), lambda j, i: (i, j)),
        out_shape=jax.ShapeDtypeStruct((M, N), out_dtype),
        compiler_params=_params(("parallel", "parallel"), vmem + (4 << 20)),
        name="matmul",
    )(*a_list, w)


def _mm_nt_kernel(a_ref, wt_ref, o_ref):
    o_ref[...] = lax.dot_general(a_ref[...], wt_ref[...], (((1,), (1,)), ((), ())),
                                 preferred_element_type=F32).astype(o_ref.dtype)


def matmul_nt(a, wt, out_dtype, tm=1024, tn=1024):
    M, K = a.shape
    N, _ = wt.shape
    tm, tn = min(tm, M), min(tn, N)
    osz = jnp.dtype(out_dtype).itemsize
    vmem = 2 * (tm * K * 2 + K * tn * 2 + tm * tn * osz) + 2 * tm * tn * 4
    return pl.pallas_call(
        _mm_nt_kernel,
        grid=(N // tn, M // tm),
        in_specs=[pl.BlockSpec((tm, K), lambda j, i: (i, 0)),
                  pl.BlockSpec((tn, K), lambda j, i: (j, 0))],
        out_specs=pl.BlockSpec((tm, tn), lambda j, i: (i, j)),
        out_shape=jax.ShapeDtypeStruct((M, N), out_dtype),
        compiler_params=_params(("parallel", "parallel"), vmem + (4 << 20)),
        name="matmul_nt",
    )(a, wt)


def _mm_castw_kernel(*refs, shift, transposed):
    if shift:
        a_ref, w_ref, e_ref, o_ref, wbf_ref = refs
    else:
        a_ref, w_ref, o_ref, wbf_ref = refs

    @pl.when(pl.program_id(1) == 0)
    def _():
        if shift:
            keep = w_ref.shape[0] - shift
            wbf_ref[0:keep, :] = w_ref[shift:, :].astype(wbf_ref.dtype)
            wbf_ref[keep:, :] = e_ref[...].astype(wbf_ref.dtype)
        else:
            wbf_ref[...] = w_ref[...].astype(wbf_ref.dtype)

    dims = (((1,), (1,)), ((), ())) if transposed else (((1,), (0,)), ((), ()))
    o_ref[...] = lax.dot_general(a_ref[...], wbf_ref[...], dims,
                                 preferred_element_type=F32).astype(o_ref.dtype)


def matmul_cast(a, w, layer, out_dtype, n0=0, n=None, transposed=False, tm=1024, tn=512):
    M, K = a.shape
    n_all = w.shape[1] if transposed else w.shape[2]
    n = n_all - n0 if n is None else n
    tm, tn = min(tm, M), min(tn, n)
    shift = n0 % tn
    base = n0 - shift
    assert n % tn == 0 and M % tm == 0
    osz = jnp.dtype(out_dtype).itemsize
    if transposed:
        w_specs = [pl.BlockSpec((None, tn, K), lambda j, i: (layer, base // tn + j, 0))]
        if shift:
            assert shift % BF16_ROW_TILE == 0 and tn % shift == 0 and base % shift == 0
            w_specs.append(pl.BlockSpec((None, shift, K), lambda j, i: (layer, (base + (j + 1) * tn) // shift, 0)))
        wbf_spec = pl.BlockSpec((tn, K), lambda j, i: (j, 0))
        wbf_shape = (n, K)
    else:
        assert shift == 0
        w_specs = [pl.BlockSpec((None, K, tn), lambda j, i: (layer, 0, base // tn + j))]
        wbf_spec = pl.BlockSpec((K, tn), lambda j, i: (0, j))
        wbf_shape = (K, n)
    vmem = 2 * (tm * K * 2 + K * tn * 4 + K * tn * 2 + tm * tn * osz) + 2 * tm * tn * 4 + 2 * K * tn * 2
    return pl.pallas_call(
        functools.partial(_mm_castw_kernel, shift=shift, transposed=transposed),
        grid=(n // tn, M // tm),
        in_specs=[pl.BlockSpec((tm, K), lambda j, i: (i, 0))] + w_specs,
        out_specs=[pl.BlockSpec((tm, tn), lambda j, i: (i, j)), wbf_spec],
        out_shape=[jax.ShapeDtypeStruct((M, n), out_dtype), jax.ShapeDtypeStruct(wbf_shape, BF16)],
        compiler_params=_params(("parallel", "arbitrary"), vmem + (2 << 20)),
        name="matmul_cast",
    )(a, *([w] * len(w_specs)))


def _mem_kv_kernel(x_ref, g_ref, w_ref, k_ref, v_ref, k16_ref, v16_ref, h_scr):
    @pl.when(pl.program_id(2) == 0)
    def _():
        h_scr[...] = _rms(x_ref[...], g_ref[...]).astype(h_scr.dtype)

    kv = jnp.dot(h_scr[...], w_ref[...], preferred_element_type=F32)
    half = pl.num_programs(2) // 2

    @pl.when(pl.program_id(2) < half)
    def _():
        k_ref[...] = kv
        k16_ref[...] = kv.astype(k16_ref.dtype)

    @pl.when(pl.program_id(2) >= half)
    def _():
        v_ref[...] = kv
        v16_ref[...] = kv.astype(v16_ref.dtype)


def mem_kv(mem, g_mem, w_kv, tm=512, tn=512):
    M, D = mem.shape
    depth, _, N = w_kv.shape
    dx = N // 2
    tm, tn = min(tm, M), min(tn, dx)
    nj = N // tn
    half = nj // 2
    vmem = 2 * (tm * D * 4 + D * tn * 2 + 3 * tm * tn * 4) + tm * D * 2 + 2 * tm * tn * 4 + 2 * tm * D * 4
    out = jax.ShapeDtypeStruct((depth, M, dx), F32)
    out16 = jax.ShapeDtypeStruct((depth, M, dx), BF16)
    k_spec = pl.BlockSpec((None, tm, tn), lambda l, i, j: (l, i, jnp.minimum(j, half - 1)))
    v_spec = pl.BlockSpec((None, tm, tn), lambda l, i, j: (l, i, jnp.maximum(j - half, 0)))
    return pl.pallas_call(
        _mem_kv_kernel,
        grid=(depth, M // tm, nj),
        in_specs=[pl.BlockSpec((tm, D), lambda l, i, j: (i, 0)),
                  pl.BlockSpec((None, 1, D), lambda l, i, j: (l, 0, 0)),
                  pl.BlockSpec((None, D, tn), lambda l, i, j: (l, 0, j))],
        out_specs=[k_spec, v_spec, k_spec, v_spec],
        out_shape=[out, out, out16, out16],
        scratch_shapes=[pltpu.VMEM((tm, D), BF16)],
        compiler_params=_params(("parallel", "parallel", "arbitrary"), vmem + (4 << 20)),
        name="mem_kv",
    )(mem, g_mem.reshape(depth, 1, D), w_kv)


def _gates_kernel(h_ref, w_ref, b_ref, o_ref):
    nh = MLSTM_HEADS
    pre = lax.dot_general(h_ref[...], w_ref[...], (((1,), (1,)), ((), ())),
                          preferred_element_type=F32) + b_ref[...]
    capped = GATE_SOFTCAP * jnp.tanh(pre / GATE_SOFTCAP)
    log_sig = jnp.minimum(capped, 0.0) - jnp.log(1.0 + jnp.exp(-jnp.abs(capped)))
    lane = lax.broadcasted_iota(jnp.int32, capped.shape, 1)
    o_ref[...] = jnp.where(lane < nh, capped, log_sig)


def mlstm_gates(h, w_gates, bias, tm=512):
    M, D = h.shape
    tm = min(tm, M)
    return pl.pallas_call(
        _gates_kernel,
        grid=(M // tm,),
        in_specs=[pl.BlockSpec((tm, D), lambda i: (i, 0)),
                  pl.BlockSpec((GATE_LANES, D), lambda i: (0, 0)),
                  pl.BlockSpec((1, GATE_LANES), lambda i: (0, 0))],
        out_specs=pl.BlockSpec((tm, GATE_LANES), lambda i: (i, 0)),
        out_shape=jax.ShapeDtypeStruct((M, GATE_LANES), F32),
        compiler_params=_params(("parallel",), 4 * tm * D * 2 + 4 * D * GATE_LANES * 2),
        name="mlstm_gates",
    )(h, w_gates, bias)


def _group_mm_kernel(a_ref, w_ref, s_ref, z_ref, o_ref):
    mixed = jnp.dot(a_ref[...], w_ref[...], preferred_element_type=F32) * s_ref[...]
    o_ref[...] = (mixed * _silu(z_ref[...].astype(F32))).astype(o_ref.dtype)


def group_mix_gate(pooled, w_group, scale, proj, z_off, tm=512):
    M, D = pooled.shape
    G, Gs, _ = w_group.shape
    tm = min(tm, M)
    zb = z_off // Gs
    return pl.pallas_call(
        _group_mm_kernel,
        grid=(G, M // tm),
        in_specs=[pl.BlockSpec((tm, Gs), lambda g, i: (i, g)),
                  pl.BlockSpec((None, Gs, Gs), lambda g, i: (g, 0, 0)),
                  pl.BlockSpec((1, Gs), lambda g, i: (0, g)),
                  pl.BlockSpec((tm, Gs), lambda g, i: (i, zb + g))],
        out_specs=pl.BlockSpec((tm, Gs), lambda g, i: (i, g)),
        out_shape=jax.ShapeDtypeStruct((M, D), BF16),
        compiler_params=_params(("parallel", "parallel"), 6 * tm * Gs * 4 + 4 * Gs * Gs * 2),
        name="group_mix_gate",
    )(pooled, w_group, scale.reshape(1, D), proj)


def _window_sums(ext, levels):
    sums = []
    cur = ext
    for lv in range(levels):
        cur = cur + pltpu.roll(cur, 1 << lv, 0)
        sums.append(cur)
    return sums


def _pool_mix_prompt_kernel(u_ref, halo_ref, hist_ref, zlo_ref, zhi_ref, w_ref, s_ref, o_ref, *, pos0, tt):
    i = pl.program_id(1)
    ng = len(POOL_WINDOWS)
    gs = u_ref.shape[1] // ng
    row = lax.broadcasted_iota(jnp.int32, (tt, 1), 0) + i * tt
    pos1 = (row + (pos0 + 1)).astype(F32)
    first = i == 0
    for g, w in enumerate(POOL_WINDOWS):
        lo, hi = g * gs, (g + 1) * gs
        u = u_ref[:, lo:hi].astype(F32)
        halo = jnp.where(first, hist_ref[:, lo:hi], halo_ref[:, lo:hi].astype(F32))
        ext = jnp.concatenate([halo, u], axis=0)
        win = _window_sums(ext, g + 1)[-1][HALO:, :]
        inv_cnt = 1.0 / jnp.minimum(float(w), pos1)
        pooled = (win * inv_cnt - u).astype(BF16)
        mixed = jnp.dot(pooled, w_ref[g], preferred_element_type=F32) * s_ref[:, lo:hi]
        z_ref, zg = (zlo_ref, g) if g < ng // 2 else (zhi_ref, g - ng // 2)
        z = z_ref[:, zg * gs:(zg + 1) * gs].astype(F32)
        o_ref[:, lo:hi] = (mixed * _silu(z)).astype(o_ref.dtype)


def pool_mix_prompt(proj, hist, w_group, scale, batch, pos0, z_off, tt=512):
    D = hist.shape[-1]
    G, Gs, _ = w_group.shape
    T = proj.shape[0] // batch
    tt = min(tt, T)
    nt = T // tt
    dh = D // 2
    assert z_off % dh == 0 and G == len(POOL_WINDOWS)
    zb = z_off // dh
    hist16 = jnp.concatenate([jnp.zeros((batch, 1, D), F32), hist.astype(F32)], axis=1)
    vmem = 2 * (3 * tt * D * 2 + G * Gs * Gs * 2) + 12 * (tt + HALO) * Gs * 4
    return pl.pallas_call(
        functools.partial(_pool_mix_prompt_kernel, pos0=pos0, tt=tt),
        grid=(batch, nt),
        in_specs=[pl.BlockSpec((tt, D), lambda b, i: (b * nt + i, 0)),
                  pl.BlockSpec((HALO, D),
                               lambda b, i: (jnp.maximum((b * nt + i) * (tt // HALO) - 1, 0), 0)),
                  pl.BlockSpec((None, HALO, D), lambda b, i: (b, 0, 0)),
                  pl.BlockSpec((tt, dh), lambda b, i: (b * nt + i, zb)),
                  pl.BlockSpec((tt, dh), lambda b, i: (b * nt + i, zb + 1)),
                  pl.BlockSpec((G, Gs, Gs), lambda b, i: (0, 0, 0)),
                  pl.BlockSpec((1, D), lambda b, i: (0, 0))],
        out_specs=pl.BlockSpec((tt, D), lambda b, i: (b * nt + i, 0)),
        out_shape=jax.ShapeDtypeStruct((batch * T, D), BF16),
        compiler_params=_params(("parallel", "parallel"), vmem + (4 << 20)),
        name="pool_mix_prompt",
    )(proj, proj, hist16, proj, proj, w_group, scale.reshape(1, D))


def _pool_sample_kernel(ext_ref, o_ref, *, pos0):
    steps = o_ref.shape[0]
    gs = ext_ref.shape[2] // len(POOL_WINDOWS)
    for g, w in enumerate(POOL_WINDOWS):
        lo, hi = g * gs, (g + 1) * gs
        for t in range(steps):
            cur = ext_ref[POOL_HIST + t, :, lo:hi]
            win = cur
            for r in range(1, w):
                win = win + ext_ref[POOL_HIST + t - r, :, lo:hi]
            cnt = float(min(w, pos0 + t + 1))
            o_ref[t, :, lo:hi] = (win / cnt - cur).astype(o_ref.dtype)


def pool_sample(ext_t, pos0, bt=32):
    R, B, D = ext_t.shape
    T = R - POOL_HIST
    bt = min(bt, B)
    return pl.pallas_call(
        functools.partial(_pool_sample_kernel, pos0=pos0),
        grid=(B // bt,),
        in_specs=[pl.BlockSpec((R, bt, D), lambda i: (0, i, 0))],
        out_specs=pl.BlockSpec((T, bt, D), lambda i: (0, i, 0)),
        out_shape=jax.ShapeDtypeStruct((T, B, D), BF16),
        compiler_params=_params(("parallel",), 3 * R * bt * D * 4),
        name="pool_sample",
    )(ext_t)


def _softmax_rows(s):
    m = jnp.max(s, axis=-1, keepdims=True)
    p = jnp.exp(s - m)
    return p, jnp.sum(p, axis=-1, keepdims=True)


def _attn_prompt_kernel(q_ref, k_ref, v_ref, z_ref, o_ref):
    hd = q_ref.shape[1] // XATT_HEADS
    scale = hd ** -0.5
    for h in range(XATT_HEADS):
        lo, hi = h * hd, (h + 1) * hd
        s = lax.dot_general(q_ref[:, lo:hi], k_ref[:, lo:hi], (((1,), (1,)), ((), ())),
                            preferred_element_type=F32) * scale
        p, l = _softmax_rows(s)
        o = jnp.dot(p.astype(BF16), v_ref[:, lo:hi], preferred_element_type=F32) / l
        o_ref[:, lo:hi] = (o * _silu(z_ref[:, lo:hi].astype(F32))).astype(o_ref.dtype)


def attend_prompt(q_arr, qb, z_arr, zb, k, v, tq=512):
    B, S, DX = k.shape
    T = q_arr.shape[0] // B
    tq = min(tq, T)
    nq = T // tq
    return pl.pallas_call(
        _attn_prompt_kernel,
        grid=(B, nq),
        in_specs=[pl.BlockSpec((tq, DX), lambda b, i: (b * nq + i, qb)),
                  pl.BlockSpec((None, S, DX), lambda b, i: (b, 0, 0)),
                  pl.BlockSpec((None, S, DX), lambda b, i: (b, 0, 0)),
                  pl.BlockSpec((tq, DX), lambda b, i: (b * nq + i, zb))],
        out_specs=pl.BlockSpec((tq, DX), lambda b, i: (b * nq + i, 0)),
        out_shape=jax.ShapeDtypeStruct((B * T, DX), BF16),
        compiler_params=_params(("parallel", "parallel"), 8 * tq * DX * 4 + 8 * S * DX * 2),
        name="attend_prompt",
    )(q_arr, k, v, z_arr)


def _attn_sample_kernel(q_ref, k_hbm, v_hbm, z_ref, o_ref, kbuf, vbuf, sem, *, steps, layer, bb):
    i = pl.program_id(0)
    rows, dx = q_ref.shape
    hd = dx // XATT_HEADS
    scale = hd ** -0.5

    def head_copies(step, slot):
        out = []
        for t, (src, dst) in enumerate(((k_hbm, kbuf), (v_hbm, vbuf))):
            for h in range(XATT_HEADS):
                out.append(pltpu.make_async_copy(src.at[layer, pl.ds(step * bb, bb), :, h, :],
                                                 dst.at[slot, h], sem.at[t, slot, h]))
        return out

    @pl.when(i == 0)
    def _():
        for cp in head_copies(0, 0):
            cp.start()

    @pl.when(i + 1 < pl.num_programs(0))
    def _():
        for cp in head_copies(i + 1, (i + 1) % 2):
            cp.start()

    slot = i % 2
    for cp in head_copies(i, slot):
        cp.wait()

    row_batch = lax.broadcasted_iota(jnp.int32, (rows, 1), 0) // steps
    pairs = [(h, j) for h in range(XATT_HEADS) for j in range(bb)]
    scores = [lax.dot_general(q_ref[:, h * hd:(h + 1) * hd], kbuf[slot, h, j].astype(BF16),
                              (((1,), (1,)), ((), ())), preferred_element_type=F32) * scale
              for h, j in pairs]
    probs = [_softmax_rows(s) for s in scores]
    outs = [jnp.dot(p.astype(BF16), vbuf[slot, h, j].astype(BF16), preferred_element_type=F32) / l
            for (h, j), (p, l) in zip(pairs, probs)]
    for h in range(XATT_HEADS):
        lo, hi = h * hd, (h + 1) * hd
        acc = jnp.zeros((rows, hd), F32)
        for j in range(bb):
            acc = jnp.where(row_batch == j, outs[h * bb + j], acc)
        o_ref[:, lo:hi] = (acc * _silu(z_ref[:, lo:hi].astype(F32))).astype(o_ref.dtype)


def attend_sample(q_arr, qb, z_arr, zb, k, v, layer, steps, bb=4):
    _, B, S, nh, hd = k.shape
    assert nh == XATT_HEADS
    DX = nh * hd
    rows = bb * steps
    slot_bytes = nh * bb * S * hd * 4
    return pl.pallas_call(
        functools.partial(_attn_sample_kernel, steps=steps, layer=layer, bb=bb),
        grid=(B // bb,),
        in_specs=[pl.BlockSpec((rows, DX), lambda i: (i, qb)),
                  pl.BlockSpec(memory_space=pl.ANY), pl.BlockSpec(memory_space=pl.ANY),
                  pl.BlockSpec((rows, DX), lambda i: (i, zb))],
        out_specs=pl.BlockSpec((rows, DX), lambda i: (i, 0)),
        out_shape=jax.ShapeDtypeStruct((B * steps, DX), BF16),
        scratch_shapes=[pltpu.VMEM((2, nh, bb, S, hd), F32), pltpu.VMEM((2, nh, bb, S, hd), F32),
                        pltpu.SemaphoreType.DMA((2, 2, nh))],
        compiler_params=_params(("arbitrary",), 4 * slot_bytes + (12 << 20)),
        name="attend_sample",
    )(q_arr, k, v, z_arr)


def _head_output(num, inv_den, o, z, g):
    ms = jnp.mean(num * num, axis=-1, keepdims=True) * (inv_den * inv_den)
    row_scale = inv_den * lax.rsqrt(ms + NORM_EPS)
    return (num * row_scale) * (g * (_sigmoid(o) * _silu(z)))


def _mlstm_prompt_kernel(q_ref, k_ref, v_ref, o_ref, zlo_ref, zhi_ref, g_ref, gh_ref,
                         y_ref, c_out, n_out, m_out, ct_scr, n_scr, m_scr):
    c = pl.program_id(1)
    nh = MLSTM_HEADS
    L = q_ref.shape[0]
    dqk = q_ref.shape[1] // nh
    dv = v_ref.shape[1] // nh
    qscale = dqk ** -0.5

    @pl.when(c == 0)
    def _():
        ct_scr[...] = jnp.zeros_like(ct_scr)
        n_scr[...] = jnp.zeros_like(n_scr)
        m_scr[...] = jnp.zeros_like(m_scr)

    r_i = lax.broadcasted_iota(jnp.int32, (L, L), 0)
    c_i = lax.broadcasted_iota(jnp.int32, (L, L), 1)
    causal = c_i <= r_i
    eye = (c_i == r_i).astype(F32)
    tril = causal.astype(F32)
    triu = (r_i <= c_i).astype(F32)
    gates = g_ref[...]

    for h in range(nh):
        q = (q_ref[:, h * dqk:(h + 1) * dqk].astype(F32) * qscale)
        qb = q.astype(BF16)
        k = k_ref[:, h * dqk:(h + 1) * dqk]
        v = v_ref[:, h * dv:(h + 1) * dv]
        ig = gates[:, h:h + 1]
        lf = gates[:, nh + h:nh + h + 1]
        m_prev = m_scr[h:h + 1, 0:1]
        n_prev = n_scr[h:h + 1, :]
        ct = ct_scr[h]

        lf_row = jnp.sum(lf * eye, axis=0, keepdims=True)
        ig_row = jnp.sum(ig * eye, axis=0, keepdims=True)
        b_col = jnp.sum(tril * lf_row, axis=1, keepdims=True)
        b_row = jnp.sum(triu * lf, axis=0, keepdims=True)
        d = jnp.where(causal, b_col - b_row + ig_row, -jnp.inf)
        inter = b_col + m_prev
        m_t = jnp.maximum(inter, jnp.max(d, axis=1, keepdims=True))
        w = jnp.exp(d - m_t)
        w_inter = jnp.exp(inter - m_t)
        s = lax.dot_general(qb, k, (((1,), (1,)), ((), ())), preferred_element_type=F32) * w
        num = (jnp.dot(s.astype(BF16), v, preferred_element_type=F32)
               + w_inter * jnp.dot(qb, ct.astype(BF16), preferred_element_type=F32))
        den = (jnp.sum(s, axis=1, keepdims=True)
               + w_inter * jnp.sum(q * n_prev, axis=1, keepdims=True))
        inv_den = 1.0 / jnp.maximum(jnp.abs(den), jnp.exp(-m_t))

        b_last = b_col[L - 1:L, :]
        m_new = m_t[L - 1:L, :]
        w_state = jnp.exp(b_last - b_col + ig - m_new)
        decay = jnp.exp(b_last + m_prev - m_new)
        kw = k.astype(F32) * w_state
        ct_scr[h] = decay * ct + lax.dot_general(kw.astype(BF16), v, (((0,), (0,)), ((), ())),
                                                 preferred_element_type=F32)
        n_scr[h:h + 1, :] = decay * n_prev + jnp.sum(kw, axis=0, keepdims=True)
        m_scr[h:h + 1, :] = jnp.broadcast_to(m_new, (1, m_scr.shape[1]))

        lo, hi = h * dv, (h + 1) * dv
        z_ref, zh = (zlo_ref, h) if h < nh // 2 else (zhi_ref, h - nh // 2)
        y_ref[:, lo:hi] = _head_output(num, inv_den, o_ref[:, lo:hi].astype(F32),
                                       z_ref[:, zh * dv:(zh + 1) * dv].astype(F32),
                                       gh_ref[:, lo:hi]).astype(y_ref.dtype)

    @pl.when(c == pl.num_programs(1) - 1)
    def _():
        for h in range(nh):
            c_out[h] = ct_scr[h].T
        n_out[...] = n_scr[...]
        m_out[...] = m_scr[...]


def mlstm_prompt(proj, proj_z, gates, g_head, batch, dqk, dv):
    nh = MLSTM_HEADS
    T = proj.shape[0] // batch
    L = min(MLSTM_CHUNK, T)
    nc = T // L
    wq, wv = nh * dqk, nh * dv
    assert wv == 2 * wq
    rows = lambda b, c: b * nc + c
    return pl.pallas_call(
        _mlstm_prompt_kernel,
        grid=(batch, nc),
        in_specs=[pl.BlockSpec((L, wq), lambda b, c: (rows(b, c), 0)),
                  pl.BlockSpec((L, wq), lambda b, c: (rows(b, c), 1)),
                  pl.BlockSpec((L, wv), lambda b, c: (rows(b, c), 1)),
                  pl.BlockSpec((L, wv), lambda b, c: (rows(b, c), 2)),
                  pl.BlockSpec((L, wq), lambda b, c: (rows(b, c), 1)),
                  pl.BlockSpec((L, wq), lambda b, c: (rows(b, c), 2)),
                  pl.BlockSpec((L, GATE_LANES), lambda b, c: (rows(b, c), 0)),
                  pl.BlockSpec((1, wv), lambda b, c: (0, 0))],
        out_specs=[pl.BlockSpec((L, wv), lambda b, c: (rows(b, c), 0)),
                   pl.BlockSpec((None, nh, dv, dqk), lambda b, c: (b, 0, 0, 0)),
                   pl.BlockSpec((None, nh, dqk), lambda b, c: (b, 0, 0)),
                   pl.BlockSpec((None, nh, GATE_LANES), lambda b, c: (b, 0, 0))],
        out_shape=[jax.ShapeDtypeStruct((batch * T, wv), BF16),
                   jax.ShapeDtypeStruct((batch, nh, dv, dqk), F32),
                   jax.ShapeDtypeStruct((batch, nh, dqk), F32),
                   jax.ShapeDtypeStruct((batch, nh, GATE_LANES), F32)],
        scratch_shapes=[pltpu.VMEM((nh, dqk, dv), F32),
                        pltpu.VMEM((nh, dqk), F32),
                        pltpu.VMEM((nh, GATE_LANES), F32)],
        compiler_params=_params(("parallel", "arbitrary"),
                                24 * L * wv + 5 * nh * dv * dqk * 4 + (16 << 20)),
        name="mlstm_prompt",
    )(proj, proj, proj, proj, proj_z, proj_z, gates, g_head.reshape(1, wv))


def _lane_pick(x, lane, idx):
    return jnp.sum(jnp.where(lane == idx, x, 0.0), axis=1, keepdims=True)


def _mlstm_sample_kernel(q_ref, k_ref, v_ref, o_ref, z_ref, g_ref, gh_ref, c_ref, n_ref,
                         y_ref, c_out, n_out, m_out, *, steps):
    head = pl.program_id(1)
    nh = MLSTM_HEADS
    R, dqk = q_ref.shape
    nb = R // steps
    qscale = dqk ** -0.5

    gates = g_ref[...]
    lane = lax.broadcasted_iota(jnp.int32, gates.shape, 1)
    ig = _lane_pick(gates, lane, head)
    lf = _lane_pick(gates, lane, nh + head)
    m_prev = _lane_pick(gates, lane, 2 * nh + head)

    r_i = lax.broadcasted_iota(jnp.int32, (R, R), 0)
    c_i = lax.broadcasted_iota(jnp.int32, (R, R), 1)
    same = (r_i // steps) == (c_i // steps)
    causal = same & (c_i <= r_i)
    eye = (c_i == r_i).astype(F32)
    lower = causal.astype(F32)
    upper = (same & (r_i <= c_i)).astype(F32)
    last_of_row_batch = (c_i == (r_i // steps) * steps + (steps - 1)).astype(F32)
    row_batch = lax.broadcasted_iota(jnp.int32, (R, 1), 0) // steps

    q = q_ref[...].astype(F32) * qscale
    qb = q.astype(BF16)
    k = k_ref[...]
    v = v_ref[...]

    lf_row = jnp.sum(lf * eye, axis=0, keepdims=True)
    ig_row = jnp.sum(ig * eye, axis=0, keepdims=True)
    b_col = jnp.sum(lower * lf_row, axis=1, keepdims=True)
    b_row = jnp.sum(upper * lf, axis=0, keepdims=True)
    d = jnp.where(causal, b_col - b_row + ig_row, -jnp.inf)
    inter = b_col + m_prev
    m_t = jnp.maximum(inter, jnp.max(d, axis=1, keepdims=True))
    w = jnp.exp(d - m_t)
    w_inter = jnp.exp(inter - m_t)
    s = lax.dot_general(qb, k, (((1,), (1,)), ((), ())), preferred_element_type=F32) * w

    qc = jnp.zeros((R, v_ref.shape[1]), F32)
    n_rows = jnp.zeros((R, dqk), F32)
    for j in range(nb):
        mine = row_batch == j
        qcj = lax.dot_general(qb, c_ref[j].astype(BF16), (((1,), (1,)), ((), ())),
                              preferred_element_type=F32)
        qc = jnp.where(mine, qcj, qc)
        n_rows = jnp.where(mine, n_ref[j:j + 1, :], n_rows)

    num = jnp.dot(s.astype(BF16), v, preferred_element_type=F32) + w_inter * qc
    den = jnp.sum(s, axis=1, keepdims=True) + w_inter * jnp.sum(q * n_rows, axis=1, keepdims=True)
    inv_den = 1.0 / jnp.maximum(jnp.abs(den), jnp.exp(-m_t))
    y_ref[...] = _head_output(num, inv_den, o_ref[...].astype(F32), z_ref[...].astype(F32),
                              gh_ref[...]).astype(y_ref.dtype)

    m_row = jnp.sum(m_t * eye, axis=0, keepdims=True)
    b_last = jnp.sum(last_of_row_batch * b_row, axis=1, keepdims=True)
    m_new = jnp.sum(last_of_row_batch * m_row, axis=1, keepdims=True)
    w_state = jnp.exp(b_last - b_col + ig - m_new)
    decay = jnp.exp(b_last + m_prev - m_new)
    vw = v.astype(F32) * w_state
    kw = k.astype(F32) * w_state
    for j in range(nb):
        mine = row_batch == j
        r0 = j * steps
        dj = decay[r0:r0 + 1, :]
        upd = lax.dot_general(jnp.where(mine, vw, 0.0).astype(BF16), k, (((0,), (0,)), ((), ())),
                              preferred_element_type=F32)
        c_out[j] = dj * c_ref[j] + upd
        n_out[j:j + 1, :] = dj * n_ref[j:j + 1, :] + jnp.sum(jnp.where(mine, kw, 0.0), axis=0, keepdims=True)
        m_out[j:j + 1, :] = jnp.broadcast_to(m_new[r0:r0 + 1, :], (1, m_out.shape[1]))


def mlstm_sample(proj, proj_z, gates, g_head, c0, n0, steps, dqk, dv):
    nh = MLSTM_HEADS
    B = c0.shape[0]
    nb = SAMPLE_BATCH_BLOCK
    R = nb * steps
    kq = nh
    kv = (2 * nh * dqk) // dv
    ko = kv + nh
    kz = (nh * dqk) // dv
    return pl.pallas_call(
        functools.partial(_mlstm_sample_kernel, steps=steps),
        grid=(B // nb, nh),
        in_specs=[pl.BlockSpec((R, dqk), lambda i, h: (i, h)),
                  pl.BlockSpec((R, dqk), lambda i, h: (i, kq + h)),
                  pl.BlockSpec((R, dv), lambda i, h: (i, kv + h)),
                  pl.BlockSpec((R, dv), lambda i, h: (i, ko + h)),
                  pl.BlockSpec((R, dv), lambda i, h: (i, kz + h)),
                  pl.BlockSpec((R, GATE_LANES), lambda i, h: (i, 0)),
                  pl.BlockSpec((1, dv), lambda i, h: (0, h)),
                  pl.BlockSpec((nb, None, dv, dqk), lambda i, h: (i, h, 0, 0)),
                  pl.BlockSpec((None, nb, dqk), lambda i, h: (h, i, 0))],
        out_specs=[pl.BlockSpec((R, dv), lambda i, h: (i, h)),
                   pl.BlockSpec((nb, None, dv, dqk), lambda i, h: (i, h, 0, 0)),
                   pl.BlockSpec((None, nb, dqk), lambda i, h: (h, i, 0)),
                   pl.BlockSpec((None, nb, GATE_LANES), lambda i, h: (h, i, 0))],
        out_shape=[jax.ShapeDtypeStruct((B * steps, nh * dv), BF16),
                   jax.ShapeDtypeStruct(c0.shape, F32),
                   jax.ShapeDtypeStruct((nh, B, dqk), F32),
                   jax.ShapeDtypeStruct((nh, B, GATE_LANES), F32)],
        compiler_params=_params(("parallel", "parallel"), 6 * nb * dv * dqk * 4 + (8 << 20)),
        name="mlstm_sample",
    )(proj, proj, proj, proj, proj_z, gates, g_head.reshape(1, nh * dv), c0, n0)


def _finish(x, h_mix, h_att, w_out, layer, g_post, g_next):
    y = matmul([h_mix, h_att], w_out, BF16, layer=layer, tm=512, tn=1024)
    return post_norm_residual(x, y, g_post, g_next)


def kernel(x_prompt, x_sample, mem_prompt, state_pool, state_mlstm_C, state_mlstm_n, state_mlstm_m,
           cache_mem_k, cache_mem_v, norm_pre, norm_post, norm_mem, w_mem_kv, w_out,
           w_in_pool, w_pool_group, pool_scale, w_in_mlstm, b_igate, b_fgate, mlstm_head_norm):
    B, T, D = x_prompt.shape
    Bs, Ts, _ = x_sample.shape
    S = mem_prompt.shape[1]
    depth = w_out.shape[0]
    nh = MLSTM_HEADS
    dx = D // 2
    dv = D // nh
    dqk = dv // 2
    hd = dx // XATT_HEADS

    xp = x_prompt.reshape(B * T, D)
    xs = x_sample.reshape(Bs * Ts, D)
    mem = mem_prompt.reshape(B * S, D)
    hp = rmsnorm_cast(xp, norm_pre[0])
    hs = rmsnorm_cast(xs, norm_pre[0])

    mem_k, mem_v, mem_k16, mem_v16 = mem_kv(mem, norm_mem, w_mem_kv.astype(BF16))
    w_o = w_out.astype(BF16)
    w_mlstm_t = jnp.swapaxes(w_in_mlstm, 1, 2)

    pool_p_l, pool_s_l = [], []
    cp_l, np_l, mp_l, cs_l, ns_l, ms_l = [], [], [], [], [], []
    for layer in range(depth):
        j = layer // 2
        g_next = norm_pre[layer + 1] if layer + 1 < depth else None
        mk16 = mem_k16[layer].reshape(B, S, dx)
        mv16 = mem_v16[layer].reshape(B, S, dx)
        if layer % 2 == 0:
            w_g = w_pool_group[j].astype(BF16)
            z_mix, qb, zb = D + dx, D // dx, (2 * D + dx) // dx
            proj, w_pool16 = matmul_cast(hp, w_in_pool, j, BF16)
            mix = pool_mix_prompt(proj, jnp.zeros((B, POOL_HIST, D), F32), w_g, pool_scale[j], B, 0, z_mix)
            att = attend_prompt(proj, qb, proj, zb, mk16, mv16)
            xp, hp = _finish(xp, mix, att, w_o, layer, norm_post[layer], g_next)
            pool_p_l.append(proj.reshape(B, T, -1)[:, T - POOL_HIST:, :D].astype(F32))
            proj = matmul([hs], w_pool16[None], BF16)
            u = proj[:, :D].astype(F32).reshape(Bs, Ts, D)
            ext = jnp.concatenate([state_pool[j], u], axis=1)
            pooled = pool_sample(ext.transpose(1, 0, 2), PAST_LEN)
            pooled = pooled.transpose(1, 0, 2).reshape(Bs * Ts, D)
            mix = group_mix_gate(pooled, w_g, pool_scale[j], proj, z_mix)
            att = attend_sample(proj, qb, proj, zb, cache_mem_k, cache_mem_v, layer, Ts)
            xs, hs = _finish(xs, mix, att, w_o, layer, norm_post[layer], g_next)
            pool_s_l.append(ext[:, Ts:])
        else:
            bias = jnp.pad(jnp.concatenate([b_igate[j], b_fgate[j]]), (0, GATE_LANES - 2 * nh))
            bias = bias.reshape(1, GATE_LANES)
            qb, zb = 0, (dx + D) // dx
            g0 = 3 * D
            w_gates = jnp.pad(w_mlstm_t[j, g0:g0 + 2 * nh, :], ((0, GATE_LANES - 2 * nh), (0, 0))).astype(BF16)
            proj, w_main = matmul_cast(hp, w_mlstm_t, j, BF16, 0, g0, transposed=True)
            proj_z, w_z = matmul_cast(hp, w_mlstm_t, j, BF16, g0 + 2 * nh, None, transposed=True)
            gates = mlstm_gates(hp, w_gates, bias)
            mix, c1, n1, m1 = mlstm_prompt(proj, proj_z, gates, mlstm_head_norm[j], B, dqk, dv)
            att = attend_prompt(proj_z, qb, proj_z, zb, mk16, mv16)
            xp, hp = _finish(xp, mix, att, w_o, layer, norm_post[layer], g_next)
            cp_l.append(c1); np_l.append(n1); mp_l.append(m1[:, :, 0])
            proj = matmul_nt(hs, w_main, BF16)
            proj_z = matmul_nt(hs, w_z, BF16)
            gates = mlstm_gates(hs, w_gates, bias)
            m_rows = jnp.repeat(state_mlstm_m[j], Ts, axis=0)
            gates = jnp.concatenate([gates[:, :2 * nh], m_rows,
                                     jnp.zeros((Bs * Ts, GATE_LANES - 3 * nh), F32)], axis=1)
            mix, c1, n1, m1 = mlstm_sample(proj, proj_z, gates, mlstm_head_norm[j], state_mlstm_C[j],
                                           state_mlstm_n[j].transpose(1, 0, 2), Ts, dqk, dv)
            att = attend_sample(proj_z, qb, proj_z, zb, cache_mem_k, cache_mem_v, layer, Ts)
            xs, hs = _finish(xs, mix, att, w_o, layer, norm_post[layer], g_next)
            cs_l.append(c1); ns_l.append(n1.transpose(1, 0, 2)); ms_l.append(m1[:, :, 0].T)

    return (xp.reshape(B, T, D), xs.reshape(Bs, Ts, D), jnp.stack(pool_p_l),
            mem_k.reshape(depth, B, S, XATT_HEADS, hd), mem_v.reshape(depth, B, S, XATT_HEADS, hd),
            jnp.stack(cp_l), jnp.stack(np_l), jnp.stack(mp_l),
            jnp.stack(pool_s_l), jnp.stack(cs_l), jnp.stack(ns_l), jnp.stack(ms_l))
```

```python
import functools

import jax
import jax.numpy as jnp
from jax import lax
from jax.experimental import pallas as pl
from jax.experimental.pallas import tpu as pltpu

F32 = jnp.float32
BF16 = jnp.bfloat16

NORM_EPS = 1e-6
GATE_SOFTCAP = 15.0
POOL_WINDOWS = (2, 4, 8, 16)
POOL_HIST = max(POOL_WINDOWS) - 1
HALO = POOL_HIST + 1
XATT_HEADS = 4
MLSTM_HEADS = 8
PAST_LEN = 16384
GATE_LANES = 128
MLSTM_CHUNK = 256
SAMPLE_BATCH_BLOCK = 8

V7X_VMEM_BYTES = 64 * 1024 * 1024
VMEM_CAP = V7X_VMEM_BYTES - 8 * 1024 * 1024


def _params(semantics, vmem_bytes):
    limit = int(min(max(vmem_bytes, 16 * 1024 * 1024), VMEM_CAP))
    return pltpu.CompilerParams(dimension_semantics=semantics, vmem_limit_bytes=limit)


NEG_LOG2E = -1.4426950408889634


def _sigmoid(x):
    return 1.0 / (1.0 + jnp.exp2(x * NEG_LOG2E))


def _silu(x):
    return x * _sigmoid(x)


def _rms(x, g):
    r = lax.rsqrt(jnp.mean(x * x, axis=-1, keepdims=True) + NORM_EPS)
    return x * r * g


def _rmsnorm_kernel(x_ref, g_ref, o_ref):
    o_ref[...] = _rms(x_ref[...], g_ref[...]).astype(o_ref.dtype)


def rmsnorm_cast(x, g, tm=256):
    M, D = x.shape
    L = g.shape[0]
    tm = min(tm, M)
    return pl.pallas_call(
        _rmsnorm_kernel,
        grid=(L, M // tm),
        in_specs=[pl.BlockSpec((tm, D), lambda l, i: (i, 0)),
                  pl.BlockSpec((None, 1, D), lambda l, i: (l, 0, 0))],
        out_specs=pl.BlockSpec((None, tm, D), lambda l, i: (l, i, 0)),
        out_shape=jax.ShapeDtypeStruct((L, M, D), BF16),
        compiler_params=_params(("parallel", "parallel"), 8 * tm * D * 4),
        name="rmsnorm_cast",
    )(x, g.reshape(L, 1, D))


def _post_kernel(*refs, with_next):
    x_ref, y_ref, g_ref = refs[:3]
    out = x_ref[...] + _rms(y_ref[...].astype(F32), g_ref[...])
    if with_next:
        gn_ref, o_ref, h_ref = refs[3:]
        h_ref[...] = _rms(out, gn_ref[...]).astype(h_ref.dtype)
    else:
        o_ref = refs[3]
    o_ref[...] = out


def post_norm_residual(x, y, g_post, g_next=None, tm=256):
    M, D = x.shape
    tm = min(tm, M)
    row = pl.BlockSpec((tm, D), lambda i: (i, 0))
    vec = pl.BlockSpec((1, D), lambda i: (0, 0))
    with_next = g_next is not None
    args = [x, y, g_post.reshape(1, D)] + ([g_next.reshape(1, D)] if with_next else [])
    out_shape = [jax.ShapeDtypeStruct((M, D), F32)] + ([jax.ShapeDtypeStruct((M, D), BF16)] if with_next else [])
    res = pl.pallas_call(
        functools.partial(_post_kernel, with_next=with_next),
        grid=(M // tm,),
        in_specs=[row, row, vec] + ([vec] if with_next else []),
        out_specs=[row] * len(out_shape),
        out_shape=out_shape,
        compiler_params=_params(("parallel",), 12 * tm * D * 4),
        name="post_norm_residual",
    )(*args)
    return (res[0], res[1]) if with_next else (res[0], None)


BF16_ROW_TILE = 16


def _mm_kernel(*refs, ksizes):
    n_a = len(ksizes)
    a_refs, w_ref, o_ref = refs[:n_a], refs[n_a], refs[n_a + 1]
    acc = None
    off = 0
    for a_ref, ks in zip(a_refs, ksizes):
        part = jnp.dot(a_ref[...], w_ref[off:off + ks, :], preferred_element_type=F32)
        acc = part if acc is None else acc + part
        off += ks
    o_ref[...] = acc.astype(o_ref.dtype)


def matmul(a_list, w, out_dtype, layer=0, tm=1024, tn=1024):
    M = a_list[0].shape[0]
    _, K, N = w.shape
    ksizes = tuple(a.shape[1] for a in a_list)
    assert sum(ksizes) == K
    tm, tn = min(tm, M), min(tn, N)
    osz = jnp.dtype(out_dtype).itemsize
    vmem = 2 * (tm * K * 2 + K * tn * 2 + tm * tn * osz) + 2 * tm * tn * 4
    return pl.pallas_call(
        functools.partial(_mm_kernel, ksizes=ksizes),
        grid=(N // tn, M // tm),
        in_specs=[pl.BlockSpec((tm, ks), lambda j, i: (i, 0)) for ks in ksizes]
        + [pl.BlockSpec((None, K, tn), lambda j, i: (layer, 0, j))],
        out_specs=pl.BlockSpec((tm, tn), lambda j, i: (i, j)),
        out_shape=jax.ShapeDtypeStruct((M, N), out_dtype),
        compiler_params=_params(("parallel", "parallel"), vmem + (4 << 20)),
        name="matmul",
    )(*a_list, w)


def _mm_nt_kernel(a_ref, wt_ref, o_ref):
    o_ref[...] = lax.dot_general(a_ref[...], wt_ref[...], (((1,), (1,)), ((), ())),
                                 preferred_element_type=F32).astype(o_ref.dtype)


def matmul_nt(a, wt, out_dtype, tm=1024, tn=1024):
    M, K = a.shape
    N, _ = wt.shape
    tm, tn = min(tm, M), min(tn, N)
    osz = jnp.dtype(out_dtype).itemsize
    vmem = 2 * (tm * K * 2 + K * tn * 2 + tm * tn * osz) + 2 * tm * tn * 4
    return pl.pallas_call(
        _mm_nt_kernel,
        grid=(N // tn, M // tm),
        in_specs=[pl.BlockSpec((tm, K), lambda j, i: (i, 0)),
                  pl.BlockSpec((tn, K), lambda j, i: (j, 0))],
        out_specs=pl.BlockSpec((tm, tn), lambda j, i: (i, j)),
        out_shape=jax.ShapeDtypeStruct((M, N), out_dtype),
        compiler_params=_params(("parallel", "parallel"), vmem + (4 << 20)),
        name="matmul_nt",
    )(a, wt)


def _mm_castw_kernel(*refs, ksizes, shift, transposed):
    n_a = len(ksizes)
    a_refs = refs[:n_a]
    if shift:
        w_ref, e_ref, o_ref, wbf_ref = refs[n_a:]
    else:
        w_ref, o_ref, wbf_ref = refs[n_a:]

    @pl.when(pl.program_id(1) == 0)
    def _():
        if shift:
            keep = w_ref.shape[0] - shift
            wbf_ref[0:keep, :] = w_ref[shift:, :].astype(wbf_ref.dtype)
            wbf_ref[keep:, :] = e_ref[...].astype(wbf_ref.dtype)
        else:
            wbf_ref[...] = w_ref[...].astype(wbf_ref.dtype)

    if transposed:
        acc = lax.dot_general(a_refs[0][...], wbf_ref[...], (((1,), (1,)), ((), ())), preferred_element_type=F32)
    else:
        acc, off = None, 0
        for a_ref, ks in zip(a_refs, ksizes):
            part = jnp.dot(a_ref[...], wbf_ref[off:off + ks, :], preferred_element_type=F32)
            acc = part if acc is None else acc + part
            off += ks
    o_ref[...] = acc.astype(o_ref.dtype)


def matmul_cast(a_list, w, layer, out_dtype, n0=0, n=None, transposed=False, tm=1024, tn=512):
    M = a_list[0].shape[0]
    ksizes = tuple(a.shape[1] for a in a_list)
    K = sum(ksizes)
    assert not transposed or len(a_list) == 1
    n_all = w.shape[1] if transposed else w.shape[2]
    n = n_all - n0 if n is None else n
    tm, tn = min(tm, M), min(tn, n)
    shift = n0 % tn
    base = n0 - shift
    assert n % tn == 0 and M % tm == 0
    osz = jnp.dtype(out_dtype).itemsize
    if transposed:
        w_specs = [pl.BlockSpec((None, tn, K), lambda j, i: (layer, base // tn + j, 0))]
        if shift:
            assert shift % BF16_ROW_TILE == 0 and tn % shift == 0 and base % shift == 0
            w_specs.append(pl.BlockSpec((None, shift, K), lambda j, i: (layer, (base + (j + 1) * tn) // shift, 0)))
        wbf_spec = pl.BlockSpec((tn, K), lambda j, i: (j, 0))
        wbf_shape = (n, K)
    else:
        assert shift == 0
        w_specs = [pl.BlockSpec((None, K, tn), lambda j, i: (layer, 0, base // tn + j))]
        wbf_spec = pl.BlockSpec((K, tn), lambda j, i: (0, j))
        wbf_shape = (K, n)
    vmem = 2 * (tm * K * 2 + K * tn * 4 + K * tn * 2 + tm * tn * osz) + 2 * tm * tn * 4 + 2 * K * tn * 2
    return pl.pallas_call(
        functools.partial(_mm_castw_kernel, ksizes=ksizes, shift=shift, transposed=transposed),
        grid=(n // tn, M // tm),
        in_specs=[pl.BlockSpec((tm, ks), lambda j, i: (i, 0)) for ks in ksizes] + w_specs,
        out_specs=[pl.BlockSpec((tm, tn), lambda j, i: (i, j)), wbf_spec],
        out_shape=[jax.ShapeDtypeStruct((M, n), out_dtype), jax.ShapeDtypeStruct(wbf_shape, BF16)],
        compiler_params=_params(("parallel", "arbitrary"), vmem + (2 << 20)),
        name="matmul_cast",
    )(*a_list, *([w] * len(w_specs)))


def mem_proj(h, w_kv, n0, n, tm=1024, tn=512):
    depth, M, D = h.shape
    tm, tn = min(tm, M), min(tn, n)
    assert n % tn == 0 and n0 % tn == 0 and M % tm == 0
    nj = n // tn
    vmem = 2 * (tm * D * 2 + D * tn * 4 + tm * tn * 4) + 2 * tm * tn * 4 + 3 * D * tn * 2
    return pl.pallas_call(
        functools.partial(_mm_castw_kernel, ksizes=(D,), shift=0, transposed=False),
        grid=(depth * nj, M // tm),
        in_specs=[pl.BlockSpec((None, tm, D), lambda jj, i: (jj // nj, i, 0)),
                  pl.BlockSpec((None, D, tn), lambda jj, i: (jj // nj, 0, n0 // tn + jj % nj))],
        out_specs=pl.BlockSpec((None, tm, tn), lambda jj, i: (jj // nj, i, jj % nj)),
        out_shape=jax.ShapeDtypeStruct((depth, M, n), F32),
        scratch_shapes=[pltpu.VMEM((D, tn), BF16)],
        compiler_params=_params(("parallel", "arbitrary"), vmem + (2 << 20)),
        name="mem_proj",
    )(h, w_kv)


def _gates_kernel(h_ref, w_ref, b_ref, o_ref):
    nh = MLSTM_HEADS
    pre = lax.dot_general(h_ref[...], w_ref[...], (((1,), (1,)), ((), ())),
                          preferred_element_type=F32) + b_ref[...]
    capped = GATE_SOFTCAP * jnp.tanh(pre / GATE_SOFTCAP)
    log_sig = jnp.minimum(capped, 0.0) - jnp.log(1.0 + jnp.exp(-jnp.abs(capped)))
    lane = lax.broadcasted_iota(jnp.int32, capped.shape, 1)
    o_ref[...] = jnp.where(lane < nh, capped, log_sig)


def mlstm_gates(h, w_gates, bias, tm=512):
    M, D = h.shape
    tm = min(tm, M)
    return pl.pallas_call(
        _gates_kernel,
        grid=(M // tm,),
        in_specs=[pl.BlockSpec((tm, D), lambda i: (i, 0)),
                  pl.BlockSpec((GATE_LANES, D), lambda i: (0, 0)),
                  pl.BlockSpec((1, GATE_LANES), lambda i: (0, 0))],
        out_specs=pl.BlockSpec((tm, GATE_LANES), lambda i: (i, 0)),
        out_shape=jax.ShapeDtypeStruct((M, GATE_LANES), F32),
        compiler_params=_params(("parallel",), 4 * tm * D * 2 + 4 * D * GATE_LANES * 2),
        name="mlstm_gates",
    )(h, w_gates, bias)


def _group_mm_kernel(a_ref, w_ref, s_ref, z_ref, o_ref):
    mixed = jnp.dot(a_ref[...], w_ref[...], preferred_element_type=F32) * s_ref[...]
    o_ref[...] = (mixed * _silu(z_ref[...].astype(F32))).astype(o_ref.dtype)


def group_mix_gate(pooled, w_group, scale, proj, z_off, tm=512):
    M, D = pooled.shape
    G, Gs, _ = w_group.shape
    tm = min(tm, M)
    zb = z_off // Gs
    return pl.pallas_call(
        _group_mm_kernel,
        grid=(G, M // tm),
        in_specs=[pl.BlockSpec((tm, Gs), lambda g, i: (i, g)),
                  pl.BlockSpec((None, Gs, Gs), lambda g, i: (g, 0, 0)),
                  pl.BlockSpec((1, Gs), lambda g, i: (0, g)),
                  pl.BlockSpec((tm, Gs), lambda g, i: (i, zb + g))],
        out_specs=pl.BlockSpec((tm, Gs), lambda g, i: (i, g)),
        out_shape=jax.ShapeDtypeStruct((M, D), BF16),
        compiler_params=_params(("parallel", "parallel"), 6 * tm * Gs * 4 + 4 * Gs * Gs * 2),
        name="group_mix_gate",
    )(pooled, w_group, scale.reshape(1, D), proj)


def _window_sums(ext, levels):
    sums = []
    cur = ext
    for lv in range(levels):
        cur = cur + pltpu.roll(cur, 1 << lv, 0)
        sums.append(cur)
    return sums


def _pool_mix_prompt_kernel(u_ref, halo_ref, hist_ref, zlo_ref, zhi_ref, w_ref, s_ref, o_ref, *, pos0, tt):
    i = pl.program_id(1)
    ng = len(POOL_WINDOWS)
    gs = u_ref.shape[1] // ng
    row = lax.broadcasted_iota(jnp.int32, (tt, 1), 0) + i * tt
    pos1 = (row + (pos0 + 1)).astype(F32)
    first = i == 0
    for g, w in enumerate(POOL_WINDOWS):
        lo, hi = g * gs, (g + 1) * gs
        u = u_ref[:, lo:hi].astype(F32)
        halo = jnp.where(first, hist_ref[:, lo:hi], halo_ref[:, lo:hi].astype(F32))
        ext = jnp.concatenate([halo, u], axis=0)
        win = _window_sums(ext, g + 1)[-1][HALO:, :]
        inv_cnt = 1.0 / jnp.minimum(float(w), pos1)
        pooled = (win * inv_cnt - u).astype(BF16)
        mixed = jnp.dot(pooled, w_ref[g], preferred_element_type=F32) * s_ref[:, lo:hi]
        z_ref, zg = (zlo_ref, g) if g < ng // 2 else (zhi_ref, g - ng // 2)
        z = z_ref[:, zg * gs:(zg + 1) * gs].astype(F32)
        o_ref[:, lo:hi] = (mixed * _silu(z)).astype(o_ref.dtype)


def pool_mix_prompt(proj, hist, w_group, scale, batch, pos0, z_off, tt=512):
    D = hist.shape[-1]
    G, Gs, _ = w_group.shape
    T = proj.shape[0] // batch
    tt = min(tt, T)
    nt = T // tt
    dh = D // 2
    assert z_off % dh == 0 and G == len(POOL_WINDOWS)
    zb = z_off // dh
    hist16 = jnp.concatenate([jnp.zeros((batch, 1, D), F32), hist.astype(F32)], axis=1)
    vmem = 2 * (3 * tt * D * 2 + G * Gs * Gs * 2) + 12 * (tt + HALO) * Gs * 4
    return pl.pallas_call(
        functools.partial(_pool_mix_prompt_kernel, pos0=pos0, tt=tt),
        grid=(batch, nt),
        in_specs=[pl.BlockSpec((tt, D), lambda b, i: (b * nt + i, 0)),
                  pl.BlockSpec((HALO, D),
                               lambda b, i: (jnp.maximum((b * nt + i) * (tt // HALO) - 1, 0), 0)),
                  pl.BlockSpec((None, HALO, D), lambda b, i: (b, 0, 0)),
                  pl.BlockSpec((tt, dh), lambda b, i: (b * nt + i, zb)),
                  pl.BlockSpec((tt, dh), lambda b, i: (b * nt + i, zb + 1)),
                  pl.BlockSpec((G, Gs, Gs), lambda b, i: (0, 0, 0)),
                  pl.BlockSpec((1, D), lambda b, i: (0, 0))],
        out_specs=pl.BlockSpec((tt, D), lambda b, i: (b * nt + i, 0)),
        out_shape=jax.ShapeDtypeStruct((batch * T, D), BF16),
        compiler_params=_params(("parallel", "parallel"), vmem + (4 << 20)),
        name="pool_mix_prompt",
    )(proj, proj, hist16, proj, proj, w_group, scale.reshape(1, D))


def _pool_sample_kernel(ext_ref, o_ref, *, pos0):
    steps = o_ref.shape[0]
    gs = ext_ref.shape[2] // len(POOL_WINDOWS)
    for g, w in enumerate(POOL_WINDOWS):
        lo, hi = g * gs, (g + 1) * gs
        for t in range(steps):
            cur = ext_ref[POOL_HIST + t, :, lo:hi]
            win = cur
            for r in range(1, w):
                win = win + ext_ref[POOL_HIST + t - r, :, lo:hi]
            cnt = float(min(w, pos0 + t + 1))
            o_ref[t, :, lo:hi] = (win / cnt - cur).astype(o_ref.dtype)


def pool_sample(ext_t, pos0, bt=32):
    R, B, D = ext_t.shape
    T = R - POOL_HIST
    bt = min(bt, B)
    return pl.pallas_call(
        functools.partial(_pool_sample_kernel, pos0=pos0),
        grid=(B // bt,),
        in_specs=[pl.BlockSpec((R, bt, D), lambda i: (0, i, 0))],
        out_specs=pl.BlockSpec((T, bt, D), lambda i: (0, i, 0)),
        out_shape=jax.ShapeDtypeStruct((T, B, D), BF16),
        compiler_params=_params(("parallel",), 3 * R * bt * D * 4),
        name="pool_sample",
    )(ext_t)


def _softmax_rows(s):
    m = jnp.max(s, axis=-1, keepdims=True)
    p = jnp.exp(s - m)
    return p, jnp.sum(p, axis=-1, keepdims=True)


def _attn_prompt_kernel(q_ref, k_ref, v_ref, z_ref, o_ref):
    hd = q_ref.shape[1] // XATT_HEADS
    scale = hd ** -0.5
    for h in range(XATT_HEADS):
        lo, hi = h * hd, (h + 1) * hd
        s = lax.dot_general(q_ref[:, lo:hi], k_ref[:, lo:hi].astype(BF16), (((1,), (1,)), ((), ())),
                            preferred_element_type=F32) * scale
        p, l = _softmax_rows(s)
        o = jnp.dot(p.astype(BF16), v_ref[:, lo:hi].astype(BF16), preferred_element_type=F32) / l
        o_ref[:, lo:hi] = (o * _silu(z_ref[:, lo:hi].astype(F32))).astype(o_ref.dtype)


def attend_prompt(q_arr, qb, z_arr, zb, k, v, layer, S, tq=512):
    _, BS, DX = k.shape
    B = BS // S
    T = q_arr.shape[0] // B
    tq = min(tq, T)
    nq = T // tq
    return pl.pallas_call(
        _attn_prompt_kernel,
        grid=(B, nq),
        in_specs=[pl.BlockSpec((tq, DX), lambda b, i: (b * nq + i, qb)),
                  pl.BlockSpec((None, S, DX), lambda b, i: (layer, b, 0)),
                  pl.BlockSpec((None, S, DX), lambda b, i: (layer, b, 0)),
                  pl.BlockSpec((tq, DX), lambda b, i: (b * nq + i, zb))],
        out_specs=pl.BlockSpec((tq, DX), lambda b, i: (b * nq + i, 0)),
        out_shape=jax.ShapeDtypeStruct((B * T, DX), BF16),
        compiler_params=_params(("parallel", "parallel"), 8 * tq * DX * 4 + 8 * S * DX * 4),
        name="attend_prompt",
    )(q_arr, k, v, z_arr)


def _attn_sample_kernel(q_ref, k_hbm, v_hbm, z_ref, o_ref, kbuf, vbuf, sem, *, steps, layer, bb):
    i = pl.program_id(0)
    rows, dx = q_ref.shape
    hd = dx // XATT_HEADS
    scale = hd ** -0.5

    def head_copies(step, slot):
        out = []
        for t, (src, dst) in enumerate(((k_hbm, kbuf), (v_hbm, vbuf))):
            for h in range(XATT_HEADS):
                out.append(pltpu.make_async_copy(src.at[layer, pl.ds(step * bb, bb), :, h, :],
                                                 dst.at[slot, h], sem.at[t, slot, h]))
        return out

    @pl.when(i == 0)
    def _():
        for cp in head_copies(0, 0):
            cp.start()

    @pl.when(i + 1 < pl.num_programs(0))
    def _():
        for cp in head_copies(i + 1, (i + 1) % 2):
            cp.start()

    slot = i % 2
    for cp in head_copies(i, slot):
        cp.wait()

    row_batch = lax.broadcasted_iota(jnp.int32, (rows, 1), 0) // steps
    pairs = [(h, j) for h in range(XATT_HEADS) for j in range(bb)]
    scores = [lax.dot_general(q_ref[:, h * hd:(h + 1) * hd], kbuf[slot, h, j].astype(BF16),
                              (((1,), (1,)), ((), ())), preferred_element_type=F32) * scale
              for h, j in pairs]
    probs = [_softmax_rows(s) for s in scores]
    outs = [jnp.dot(p.astype(BF16), vbuf[slot, h, j].astype(BF16), preferred_element_type=F32) / l
            for (h, j), (p, l) in zip(pairs, probs)]
    for h in range(XATT_HEADS):
        lo, hi = h * hd, (h + 1) * hd
        acc = jnp.zeros((rows, hd), F32)
        for j in range(bb):
            acc = jnp.where(row_batch == j, outs[h * bb + j], acc)
        o_ref[:, lo:hi] = (acc * _silu(z_ref[:, lo:hi].astype(F32))).astype(o_ref.dtype)


def attend_sample(q_arr, qb, z_arr, zb, k, v, layer, steps, bb=4):
    _, B, S, nh, hd = k.shape
    assert nh == XATT_HEADS
    DX = nh * hd
    rows = bb * steps
    slot_bytes = nh * bb * S * hd * 4
    return pl.pallas_call(
        functools.partial(_attn_sample_kernel, steps=steps, layer=layer, bb=bb),
        grid=(B // bb,),
        in_specs=[pl.BlockSpec((rows, DX), lambda i: (i, qb)),
                  pl.BlockSpec(memory_space=pl.ANY), pl.BlockSpec(memory_space=pl.ANY),
                  pl.BlockSpec((rows, DX), lambda i: (i, zb))],
        out_specs=pl.BlockSpec((rows, DX), lambda i: (i, 0)),
        out_shape=jax.ShapeDtypeStruct((B * steps, DX), BF16),
        scratch_shapes=[pltpu.VMEM((2, nh, bb, S, hd), F32), pltpu.VMEM((2, nh, bb, S, hd), F32),
                        pltpu.SemaphoreType.DMA((2, 2, nh))],
        compiler_params=_params(("arbitrary",), 4 * slot_bytes + (12 << 20)),
        name="attend_sample",
    )(q_arr, k, v, z_arr)


def _head_output(num, inv_den, o, z, g):
    ms = jnp.mean(num * num, axis=-1, keepdims=True) * (inv_den * inv_den)
    row_scale = inv_den * lax.rsqrt(ms + NORM_EPS)
    return (num * row_scale) * (g * (_sigmoid(o) * _silu(z)))


def _mlstm_prompt_kernel(q_ref, k_ref, v_ref, o_ref, zlo_ref, zhi_ref, g_ref, gh_ref,
                         y_ref, c_out, n_out, m_out, ct_scr, n_scr, m_scr):
    c = pl.program_id(1)
    nh = MLSTM_HEADS
    L = q_ref.shape[0]
    dqk = q_ref.shape[1] // nh
    dv = v_ref.shape[1] // nh
    qscale = dqk ** -0.5

    @pl.when(c == 0)
    def _():
        ct_scr[...] = jnp.zeros_like(ct_scr)
        n_scr[...] = jnp.zeros_like(n_scr)
        m_scr[...] = jnp.zeros_like(m_scr)

    r_i = lax.broadcasted_iota(jnp.int32, (L, L), 0)
    c_i = lax.broadcasted_iota(jnp.int32, (L, L), 1)
    causal = c_i <= r_i
    eye = (c_i == r_i).astype(F32)
    tril = causal.astype(F32)
    triu = (r_i <= c_i).astype(F32)
    gates = g_ref[...]

    for h in range(nh):
        q = (q_ref[:, h * dqk:(h + 1) * dqk].astype(F32) * qscale)
        qb = q.astype(BF16)
        k = k_ref[:, h * dqk:(h + 1) * dqk]
        v = v_ref[:, h * dv:(h + 1) * dv]
        ig = gates[:, h:h + 1]
        lf = gates[:, nh + h:nh + h + 1]
        m_prev = m_scr[h:h + 1, 0:1]
        n_prev = n_scr[h:h + 1, :]
        ct = ct_scr[h]

        lf_row = jnp.sum(lf * eye, axis=0, keepdims=True)
        ig_row = jnp.sum(ig * eye, axis=0, keepdims=True)
        b_col = jnp.sum(tril * lf_row, axis=1, keepdims=True)
        b_row = jnp.sum(triu * lf, axis=0, keepdims=True)
        d = jnp.where(causal, b_col - b_row + ig_row, -jnp.inf)
        inter = b_col + m_prev
        m_t = jnp.maximum(inter, jnp.max(d, axis=1, keepdims=True))
        w = jnp.exp(d - m_t)
        w_inter = jnp.exp(inter - m_t)
        s = lax.dot_general(qb, k, (((1,), (1,)), ((), ())), preferred_element_type=F32) * w
        num = (jnp.dot(s.astype(BF16), v, preferred_element_type=F32)
               + w_inter * jnp.dot(qb, ct.astype(BF16), preferred_element_type=F32))
        den = (jnp.sum(s, axis=1, keepdims=True)
               + w_inter * jnp.sum(q * n_prev, axis=1, keepdims=True))
        inv_den = 1.0 / jnp.maximum(jnp.abs(den), jnp.exp(-m_t))

        b_last = b_col[L - 1:L, :]
        m_new = m_t[L - 1:L, :]
        w_state = jnp.exp(b_last - b_col + ig - m_new)
        decay = jnp.exp(b_last + m_prev - m_new)
        kw = k.astype(F32) * w_state
        ct_scr[h] = decay * ct + lax.dot_general(kw.astype(BF16), v, (((0,), (0,)), ((), ())),
                                                 preferred_element_type=F32)
        n_scr[h:h + 1, :] = decay * n_prev + jnp.sum(kw, axis=0, keepdims=True)
        m_scr[h:h + 1, :] = jnp.broadcast_to(m_new, (1, m_scr.shape[1]))

        lo, hi = h * dv, (h + 1) * dv
        z_ref, zh = (zlo_ref, h) if h < nh // 2 else (zhi_ref, h - nh // 2)
        y_ref[:, lo:hi] = _head_output(num, inv_den, o_ref[:, lo:hi].astype(F32),
                                       z_ref[:, zh * dv:(zh + 1) * dv].astype(F32),
                                       gh_ref[:, lo:hi]).astype(y_ref.dtype)

    @pl.when(c == pl.num_programs(1) - 1)
    def _():
        for h in range(nh):
            c_out[h] = ct_scr[h].T
        n_out[...] = n_scr[...]
        m_out[...] = m_scr[...]


def mlstm_prompt(proj, proj_z, gates, g_head, batch, dqk, dv):
    nh = MLSTM_HEADS
    T = proj.shape[0] // batch
    L = min(MLSTM_CHUNK, T)
    nc = T // L
    wq, wv = nh * dqk, nh * dv
    assert wv == 2 * wq
    rows = lambda b, c: b * nc + c
    return pl.pallas_call(
        _mlstm_prompt_kernel,
        grid=(batch, nc),
        in_specs=[pl.BlockSpec((L, wq), lambda b, c: (rows(b, c), 0)),
                  pl.BlockSpec((L, wq), lambda b, c: (rows(b, c), 1)),
                  pl.BlockSpec((L, wv), lambda b, c: (rows(b, c), 1)),
                  pl.BlockSpec((L, wv), lambda b, c: (rows(b, c), 2)),
                  pl.BlockSpec((L, wq), lambda b, c: (rows(b, c), 1)),
                  pl.BlockSpec((L, wq), lambda b, c: (rows(b, c), 2)),
                  pl.BlockSpec((L, GATE_LANES), lambda b, c: (rows(b, c), 0)),
                  pl.BlockSpec((1, wv), lambda b, c: (0, 0))],
        out_specs=[pl.BlockSpec((L, wv), lambda b, c: (rows(b, c), 0)),
                   pl.BlockSpec((None, nh, dv, dqk), lambda b, c: (b, 0, 0, 0)),
                   pl.BlockSpec((None, nh, dqk), lambda b, c: (b, 0, 0)),
                   pl.BlockSpec((None, nh, GATE_LANES), lambda b, c: (b, 0, 0))],
        out_shape=[jax.ShapeDtypeStruct((batch * T, wv), BF16),
                   jax.ShapeDtypeStruct((batch, nh, dv, dqk), F32),
                   jax.ShapeDtypeStruct((batch, nh, dqk), F32),
                   jax.ShapeDtypeStruct((batch, nh, GATE_LANES), F32)],
        scratch_shapes=[pltpu.VMEM((nh, dqk, dv), F32),
                        pltpu.VMEM((nh, dqk), F32),
                        pltpu.VMEM((nh, GATE_LANES), F32)],
        compiler_params=_params(("parallel", "arbitrary"),
                                24 * L * wv + 5 * nh * dv * dqk * 4 + (16 << 20)),
        name="mlstm_prompt",
    )(proj, proj, proj, proj, proj_z, proj_z, gates, g_head.reshape(1, wv))


def _lane_pick(x, lane, idx):
    return jnp.sum(jnp.where(lane == idx, x, 0.0), axis=1, keepdims=True)


def _mlstm_sample_kernel(q_ref, k_ref, v_ref, o_ref, z_ref, g_ref, gh_ref, c_ref, n_ref,
                         y_ref, c_out, n_out, m_out, *, steps):
    head = pl.program_id(1)
    nh = MLSTM_HEADS
    R, dqk = q_ref.shape
    nb = R // steps
    qscale = dqk ** -0.5

    gates = g_ref[...]
    lane = lax.broadcasted_iota(jnp.int32, gates.shape, 1)
    ig = _lane_pick(gates, lane, head)
    lf = _lane_pick(gates, lane, nh + head)
    m_prev = _lane_pick(gates, lane, 2 * nh + head)

    r_i = lax.broadcasted_iota(jnp.int32, (R, R), 0)
    c_i = lax.broadcasted_iota(jnp.int32, (R, R), 1)
    same = (r_i // steps) == (c_i // steps)
    causal = same & (c_i <= r_i)
    eye = (c_i == r_i).astype(F32)
    lower = causal.astype(F32)
    upper = (same & (r_i <= c_i)).astype(F32)
    last_of_row_batch = (c_i == (r_i // steps) * steps + (steps - 1)).astype(F32)
    row_batch = lax.broadcasted_iota(jnp.int32, (R, 1), 0) // steps

    q = q_ref[...].astype(F32) * qscale
    qb = q.astype(BF16)
    k = k_ref[...]
    v = v_ref[...]

    lf_row = jnp.sum(lf * eye, axis=0, keepdims=True)
    ig_row = jnp.sum(ig * eye, axis=0, keepdims=True)
    b_col = jnp.sum(lower * lf_row, axis=1, keepdims=True)
    b_row = jnp.sum(upper * lf, axis=0, keepdims=True)
    d = jnp.where(causal, b_col - b_row + ig_row, -jnp.inf)
    inter = b_col + m_prev
    m_t = jnp.maximum(inter, jnp.max(d, axis=1, keepdims=True))
    w = jnp.exp(d - m_t)
    w_inter = jnp.exp(inter - m_t)
    s = lax.dot_general(qb, k, (((1,), (1,)), ((), ())), preferred_element_type=F32) * w

    qc = jnp.zeros((R, v_ref.shape[1]), F32)
    n_rows = jnp.zeros((R, dqk), F32)
    for j in range(nb):
        mine = row_batch == j
        qcj = lax.dot_general(qb, c_ref[j].astype(BF16), (((1,), (1,)), ((), ())),
                              preferred_element_type=F32)
        qc = jnp.where(mine, qcj, qc)
        n_rows = jnp.where(mine, n_ref[j:j + 1, :], n_rows)

    num = jnp.dot(s.astype(BF16), v, preferred_element_type=F32) + w_inter * qc
    den = jnp.sum(s, axis=1, keepdims=True) + w_inter * jnp.sum(q * n_rows, axis=1, keepdims=True)
    inv_den = 1.0 / jnp.maximum(jnp.abs(den), jnp.exp(-m_t))
    y_ref[...] = _head_output(num, inv_den, o_ref[...].astype(F32), z_ref[...].astype(F32),
                              gh_ref[...]).astype(y_ref.dtype)

    m_row = jnp.sum(m_t * eye, axis=0, keepdims=True)
    b_last = jnp.sum(last_of_row_batch * b_row, axis=1, keepdims=True)
    m_new = jnp.sum(last_of_row_batch * m_row, axis=1, keepdims=True)
    w_state = jnp.exp(b_last - b_col + ig - m_new)
    decay = jnp.exp(b_last + m_prev - m_new)
    vw = v.astype(F32) * w_state
    kw = k.astype(F32) * w_state
    for j in range(nb):
        mine = row_batch == j
        r0 = j * steps
        dj = decay[r0:r0 + 1, :]
        upd = lax.dot_general(jnp.where(mine, vw, 0.0).astype(BF16), k, (((0,), (0,)), ((), ())),
                              preferred_element_type=F32)
        c_out[j] = dj * c_ref[j] + upd
        n_out[j:j + 1, :] = dj * n_ref[j:j + 1, :] + jnp.sum(jnp.where(mine, kw, 0.0), axis=0, keepdims=True)
        m_out[j:j + 1, :] = jnp.broadcast_to(m_new[r0:r0 + 1, :], (1, m_out.shape[1]))


def mlstm_sample(proj, proj_z, gates, g_head, c0, n0, steps, dqk, dv):
    nh = MLSTM_HEADS
    B = c0.shape[0]
    nb = SAMPLE_BATCH_BLOCK
    R = nb * steps
    kq = nh
    kv = (2 * nh * dqk) // dv
    ko = kv + nh
    kz = (nh * dqk) // dv
    return pl.pallas_call(
        functools.partial(_mlstm_sample_kernel, steps=steps),
        grid=(B // nb, nh),
        in_specs=[pl.BlockSpec((R, dqk), lambda i, h: (i, h)),
                  pl.BlockSpec((R, dqk), lambda i, h: (i, kq + h)),
                  pl.BlockSpec((R, dv), lambda i, h: (i, kv + h)),
                  pl.BlockSpec((R, dv), lambda i, h: (i, ko + h)),
                  pl.BlockSpec((R, dv), lambda i, h: (i, kz + h)),
                  pl.BlockSpec((R, GATE_LANES), lambda i, h: (i, 0)),
                  pl.BlockSpec((1, dv), lambda i, h: (0, h)),
                  pl.BlockSpec((nb, None, dv, dqk), lambda i, h: (i, h, 0, 0)),
                  pl.BlockSpec((None, nb, dqk), lambda i, h: (h, i, 0))],
        out_specs=[pl.BlockSpec((R, dv), lambda i, h: (i, h)),
                   pl.BlockSpec((nb, None, dv, dqk), lambda i, h: (i, h, 0, 0)),
                   pl.BlockSpec((None, nb, dqk), lambda i, h: (h, i, 0)),
                   pl.BlockSpec((None, nb, GATE_LANES), lambda i, h: (h, i, 0))],
        out_shape=[jax.ShapeDtypeStruct((B * steps, nh * dv), BF16),
                   jax.ShapeDtypeStruct(c0.shape, F32),
                   jax.ShapeDtypeStruct((nh, B, dqk), F32),
                   jax.ShapeDtypeStruct((nh, B, GATE_LANES), F32)],
        compiler_params=_params(("parallel", "parallel"), 6 * nb * dv * dqk * 4 + (8 << 20)),
        name="mlstm_sample",
    )(proj, proj, proj, proj, proj_z, gates, g_head.reshape(1, nh * dv), c0, n0)


def _finish_casting(x, h_mix, h_att, w_out, layer, g_post, g_next):
    y, w16 = matmul_cast([h_mix, h_att], w_out, layer, BF16, tm=512, tn=256)
    return post_norm_residual(x, y, g_post, g_next) + (w16,)


def _finish(x, h_mix, h_att, w16, g_post, g_next):
    y = matmul([h_mix, h_att], w16[None], BF16, tm=512, tn=1024)
    return post_norm_residual(x, y, g_post, g_next)


def kernel(x_prompt, x_sample, mem_prompt, state_pool, state_mlstm_C, state_mlstm_n, state_mlstm_m,
           cache_mem_k, cache_mem_v, norm_pre, norm_post, norm_mem, w_mem_kv, w_out,
           w_in_pool, w_pool_group, pool_scale, w_in_mlstm, b_igate, b_fgate, mlstm_head_norm):
    B, T, D = x_prompt.shape
    Bs, Ts, _ = x_sample.shape
    S = mem_prompt.shape[1]
    depth = w_out.shape[0]
    nh = MLSTM_HEADS
    dx = D // 2
    dv = D // nh
    dqk = dv // 2
    hd = dx // XATT_HEADS

    xp = x_prompt.reshape(B * T, D)
    xs = x_sample.reshape(Bs * Ts, D)
    mem = mem_prompt.reshape(B * S, D)
    hp = rmsnorm_cast(xp, norm_pre[:1])[0]
    hs = rmsnorm_cast(xs, norm_pre[:1])[0]

    h_mem = rmsnorm_cast(mem, norm_mem)
    mem_k = mem_proj(h_mem, w_mem_kv, 0, dx)
    mem_v = mem_proj(h_mem, w_mem_kv, dx, dx)
    w_mlstm_t = jnp.swapaxes(w_in_mlstm, 1, 2)

    pool_p_l, pool_s_l = [], []
    cp_l, np_l, mp_l, cs_l, ns_l, ms_l = [], [], [], [], [], []
    for layer in range(depth):
        j = layer // 2
        g_next = norm_pre[layer + 1] if layer + 1 < depth else None
        if layer % 2 == 0:
            w_g = w_pool_group[j].astype(BF16)
            z_mix, qb, zb = D + dx, D // dx, (2 * D + dx) // dx
            proj, w_pool16 = matmul_cast([hs], w_in_pool, j, BF16)
            u = proj[:, :D].astype(F32).reshape(Bs, Ts, D)
            ext = jnp.concatenate([state_pool[j], u], axis=1)
            pooled = pool_sample(ext.transpose(1, 0, 2), PAST_LEN)
            pooled = pooled.transpose(1, 0, 2).reshape(Bs * Ts, D)
            mix = group_mix_gate(pooled, w_g, pool_scale[j], proj, z_mix)
            att = attend_sample(proj, qb, proj, zb, cache_mem_k, cache_mem_v, layer, Ts)
            xs, hs, w_o16 = _finish_casting(xs, mix, att, w_out, layer, norm_post[layer], g_next)
            pool_s_l.append(ext[:, Ts:])
            proj = matmul([hp], w_pool16[None], BF16)
            mix = pool_mix_prompt(proj, jnp.zeros((B, POOL_HIST, D), F32), w_g, pool_scale[j], B, 0, z_mix)
            att = attend_prompt(proj, qb, proj, zb, mem_k, mem_v, layer, S)
            xp, hp = _finish(xp, mix, att, w_o16, norm_post[layer], g_next)
            pool_p_l.append(proj.reshape(B, T, -1)[:, T - POOL_HIST:, :D].astype(F32))
        else:
            bias = jnp.pad(jnp.concatenate([b_igate[j], b_fgate[j]]), (0, GATE_LANES - 2 * nh))
            bias = bias.reshape(1, GATE_LANES)
            qb, zb = 0, (dx + D) // dx
            g0 = 3 * D
            w_gates = jnp.pad(w_mlstm_t[j, g0:g0 + 2 * nh, :], ((0, GATE_LANES - 2 * nh), (0, 0))).astype(BF16)
            proj, w_main = matmul_cast([hs], w_mlstm_t, j, BF16, 0, g0, transposed=True)
            proj_z, w_z = matmul_cast([hs], w_mlstm_t, j, BF16, g0 + 2 * nh, None, transposed=True)
            gates = mlstm_gates(hs, w_gates, bias)
            m_rows = jnp.repeat(state_mlstm_m[j], Ts, axis=0)
            gates = jnp.concatenate([gates[:, :2 * nh], m_rows,
                                     jnp.zeros((Bs * Ts, GATE_LANES - 3 * nh), F32)], axis=1)
            mix, c1, n1, m1 = mlstm_sample(proj, proj_z, gates, mlstm_head_norm[j], state_mlstm_C[j],
                                           state_mlstm_n[j].transpose(1, 0, 2), Ts, dqk, dv)
            att = attend_sample(proj_z, qb, proj_z, zb, cache_mem_k, cache_mem_v, layer, Ts)
            xs, hs, w_o16 = _finish_casting(xs, mix, att, w_out, layer, norm_post[layer], g_next)
            cs_l.append(c1); ns_l.append(n1.transpose(1, 0, 2)); ms_l.append(m1[:, :, 0].T)
            proj = matmul_nt(hp, w_main, BF16)
            proj_z = matmul_nt(hp, w_z, BF16)
            gates = mlstm_gates(hp, w_gates, bias)
            mix, c1, n1, m1 = mlstm_prompt(proj, proj_z, gates, mlstm_head_norm[j], B, dqk, dv)
            att = attend_prompt(proj_z, qb, proj_z, zb, mem_k, mem_v, layer, S)
            xp, hp = _finish(xp, mix, att, w_o16, norm_post[layer], g_next)
            cp_l.append(c1); np_l.append(n1); mp_l.append(m1[:, :, 0])

    return (xp.reshape(B, T, D), xs.reshape(Bs, Ts, D), jnp.stack(pool_p_l),
            mem_k.reshape(depth, B, S, XATT_HEADS, hd), mem_v.reshape(depth, B, S, XATT_HEADS, hd),
            jnp.stack(cp_l), jnp.stack(np_l), jnp.stack(mp_l),
            jnp.stack(pool_s_l), jnp.stack(cs_l), jnp.stack(ns_l), jnp.stack(ms_l))
```

```python
import functools

import jax
import jax.numpy as jnp
from jax import lax
from jax.experimental import pallas as pl
from jax.experimental.pallas import tpu as pltpu

F32 = jnp.float32
BF16 = jnp.bfloat16

NORM_EPS = 1e-6
GATE_SOFTCAP = 15.0
POOL_WINDOWS = (2, 4, 8, 16)
POOL_HIST = max(POOL_WINDOWS) - 1
HALO = POOL_HIST + 1
XATT_HEADS = 4
MLSTM_HEADS = 8
PAST_LEN = 16384
GATE_LANES = 128
MLSTM_CHUNK = 256
MLSTM_HEAD_GROUP = 2
SAMPLE_BATCH_BLOCK = 8

V7X_VMEM_BYTES = 64 * 1024 * 1024
VMEM_CAP = V7X_VMEM_BYTES - 8 * 1024 * 1024


def _params(semantics, vmem_bytes):
    limit = int(min(max(vmem_bytes, 16 * 1024 * 1024), VMEM_CAP))
    return pltpu.CompilerParams(dimension_semantics=semantics, vmem_limit_bytes=limit)


LOG2E = 1.4426950408889634
LN2 = 0.6931471805599453
NEG_LOG2E = -LOG2E


def _sigmoid(x):
    return 1.0 / (1.0 + jnp.exp2(x * NEG_LOG2E))


def _silu(x):
    return x * _sigmoid(x)


def _rms(x, g):
    r = lax.rsqrt(jnp.mean(x * x, axis=-1, keepdims=True) + NORM_EPS)
    return x * r * g


def _rmsnorm_kernel(x_ref, g_ref, o_ref):
    o_ref[...] = _rms(x_ref[...], g_ref[...]).astype(o_ref.dtype)


def rmsnorm_cast(x, g, tm=256):
    M, D = x.shape
    L = g.shape[0]
    tm = min(tm, M)
    return pl.pallas_call(
        _rmsnorm_kernel,
        grid=(L, M // tm),
        in_specs=[pl.BlockSpec((tm, D), lambda l, i: (i, 0)),
                  pl.BlockSpec((None, 1, D), lambda l, i: (l, 0, 0))],
        out_specs=pl.BlockSpec((None, tm, D), lambda l, i: (l, i, 0)),
        out_shape=jax.ShapeDtypeStruct((L, M, D), BF16),
        compiler_params=_params(("parallel", "parallel"), 8 * tm * D * 4),
        name="rmsnorm_cast",
    )(x, g.reshape(L, 1, D))


def _post_kernel(*refs, with_next):
    x_ref, y_ref, g_ref = refs[:3]
    out = x_ref[...] + _rms(y_ref[...].astype(F32), g_ref[...])
    if with_next:
        gn_ref, o_ref, h_ref = refs[3:]
        h_ref[...] = _rms(out, gn_ref[...]).astype(h_ref.dtype)
    else:
        o_ref = refs[3]
    o_ref[...] = out


def post_norm_residual(x, y, g_post, g_next=None, tm=256):
    M, D = x.shape
    tm = min(tm, M)
    row = pl.BlockSpec((tm, D), lambda i: (i, 0))
    vec = pl.BlockSpec((1, D), lambda i: (0, 0))
    with_next = g_next is not None
    args = [x, y, g_post.reshape(1, D)] + ([g_next.reshape(1, D)] if with_next else [])
    out_shape = [jax.ShapeDtypeStruct((M, D), F32)] + ([jax.ShapeDtypeStruct((M, D), BF16)] if with_next else [])
    res = pl.pallas_call(
        functools.partial(_post_kernel, with_next=with_next),
        grid=(M // tm,),
        in_specs=[row, row, vec] + ([vec] if with_next else []),
        out_specs=[row] * len(out_shape),
        out_shape=out_shape,
        compiler_params=_params(("parallel",), 12 * tm * D * 4),
        name="post_norm_residual",
    )(*args)
    return (res[0], res[1]) if with_next else (res[0], None)


BF16_ROW_TILE = 16


def _mm_kernel(*refs, ksizes):
    n_a = len(ksizes)
    a_refs, w_ref, o_ref = refs[:n_a], refs[n_a], refs[n_a + 1]
    acc = None
    off = 0
    for a_ref, ks in zip(a_refs, ksizes):
        part = jnp.dot(a_ref[...], w_ref[off:off + ks, :], preferred_element_type=F32)
        acc = part if acc is None else acc + part
        off += ks
    o_ref[...] = acc.astype(o_ref.dtype)


def matmul(a_list, w, out_dtype, layer=0, tm=1024, tn=1024):
    M = a_list[0].shape[0]
    _, K, N = w.shape
    ksizes = tuple(a.shape[1] for a in a_list)
    assert sum(ksizes) == K
    tm, tn = min(tm, M), min(tn, N)
    osz = jnp.dtype(out_dtype).itemsize
    vmem = 2 * (tm * K * 2 + K * tn * 2 + tm * tn * osz) + 2 * tm * tn * 4
    return pl.pallas_call(
        functools.partial(_mm_kernel, ksizes=ksizes),
        grid=(N // tn, M // tm),
        in_specs=[pl.BlockSpec((tm, ks), lambda j, i: (i, 0)) for ks in ksizes]
        + [pl.BlockSpec((None, K, tn), lambda j, i: (layer, 0, j))],
        out_specs=pl.BlockSpec((tm, tn), lambda j, i: (i, j)),
        out_shape=jax.ShapeDtypeStruct((M, N), out_dtype),
        compiler_params=_params(("parallel", "parallel"), vmem + (4 << 20)),
        name="matmul",
    )(*a_list, w)


def _mm_nt_kernel(a_ref, wt_ref, o_ref):
    o_ref[...] = lax.dot_general(a_ref[...], wt_ref[...], (((1,), (1,)), ((), ())),
                                 preferred_element_type=F32).astype(o_ref.dtype)


def matmul_nt(a, wt, out_dtype, tm=1024, tn=1024):
    M, K = a.shape
    N, _ = wt.shape
    tm, tn = min(tm, M), min(tn, N)
    osz = jnp.dtype(out_dtype).itemsize
    vmem = 2 * (tm * K * 2 + K * tn * 2 + tm * tn * osz) + 2 * tm * tn * 4
    return pl.pallas_call(
        _mm_nt_kernel,
        grid=(N // tn, M // tm),
        in_specs=[pl.BlockSpec((tm, K), lambda j, i: (i, 0)),
                  pl.BlockSpec((tn, K), lambda j, i: (j, 0))],
        out_specs=pl.BlockSpec((tm, tn), lambda j, i: (i, j)),
        out_shape=jax.ShapeDtypeStruct((M, N), out_dtype),
        compiler_params=_params(("parallel", "parallel"), vmem + (4 << 20)),
        name="matmul_nt",
    )(a, wt)


def _mm_castw_kernel(*refs, ksizes, shift, transposed):
    n_a = len(ksizes)
    a_refs = refs[:n_a]
    if shift:
        w_ref, e_ref, o_ref, wbf_ref = refs[n_a:]
    else:
        w_ref, o_ref, wbf_ref = refs[n_a:]

    @pl.when(pl.program_id(1) == 0)
    def _():
        if shift:
            keep = w_ref.shape[0] - shift
            wbf_ref[0:keep, :] = w_ref[shift:, :].astype(wbf_ref.dtype)
            wbf_ref[keep:, :] = e_ref[...].astype(wbf_ref.dtype)
        else:
            wbf_ref[...] = w_ref[...].astype(wbf_ref.dtype)

    if transposed:
        acc = lax.dot_general(a_refs[0][...], wbf_ref[...], (((1,), (1,)), ((), ())), preferred_element_type=F32)
    else:
        acc, off = None, 0
        for a_ref, ks in zip(a_refs, ksizes):
            part = jnp.dot(a_ref[...], wbf_ref[off:off + ks, :], preferred_element_type=F32)
            acc = part if acc is None else acc + part
            off += ks
    o_ref[...] = acc.astype(o_ref.dtype)


def matmul_cast(a_list, w, layer, out_dtype, n0=0, n=None, transposed=False, tm=1024, tn=512):
    M = a_list[0].shape[0]
    ksizes = tuple(a.shape[1] for a in a_list)
    K = sum(ksizes)
    assert not transposed or len(a_list) == 1
    n_all = w.shape[1] if transposed else w.shape[2]
    n = n_all - n0 if n is None else n
    tm, tn = min(tm, M), min(tn, n)
    shift = n0 % tn
    base = n0 - shift
    assert n % tn == 0 and M % tm == 0
    osz = jnp.dtype(out_dtype).itemsize
    if transposed:
        w_specs = [pl.BlockSpec((None, tn, K), lambda j, i: (layer, base // tn + j, 0))]
        if shift:
            assert shift % BF16_ROW_TILE == 0 and tn % shift == 0 and base % shift == 0
            w_specs.append(pl.BlockSpec((None, shift, K), lambda j, i: (layer, (base + (j + 1) * tn) // shift, 0)))
        wbf_spec = pl.BlockSpec((tn, K), lambda j, i: (j, 0))
        wbf_shape = (n, K)
    else:
        assert shift == 0
        w_specs = [pl.BlockSpec((None, K, tn), lambda j, i: (layer, 0, base // tn + j))]
        wbf_spec = pl.BlockSpec((K, tn), lambda j, i: (0, j))
        wbf_shape = (K, n)
    vmem = 2 * (tm * K * 2 + K * tn * 4 + K * tn * 2 + tm * tn * osz) + 2 * tm * tn * 4 + 2 * K * tn * 2
    return pl.pallas_call(
        functools.partial(_mm_castw_kernel, ksizes=ksizes, shift=shift, transposed=transposed),
        grid=(n // tn, M // tm),
        in_specs=[pl.BlockSpec((tm, ks), lambda j, i: (i, 0)) for ks in ksizes] + w_specs,
        out_specs=[pl.BlockSpec((tm, tn), lambda j, i: (i, j)), wbf_spec],
        out_shape=[jax.ShapeDtypeStruct((M, n), out_dtype), jax.ShapeDtypeStruct(wbf_shape, BF16)],
        compiler_params=_params(("parallel", "arbitrary"), vmem + (2 << 20)),
        name="matmul_cast",
    )(*a_list, *([w] * len(w_specs)))


def mem_proj(h, w_kv, n0, n, tm=1024, tn=512):
    depth, M, D = h.shape
    tm, tn = min(tm, M), min(tn, n)
    assert n % tn == 0 and n0 % tn == 0 and M % tm == 0
    nj = n // tn
    vmem = 2 * (tm * D * 2 + D * tn * 4 + tm * tn * 4) + 2 * tm * tn * 4 + 3 * D * tn * 2
    return pl.pallas_call(
        functools.partial(_mm_castw_kernel, ksizes=(D,), shift=0, transposed=False),
        grid=(depth * nj, M // tm),
        in_specs=[pl.BlockSpec((None, tm, D), lambda jj, i: (jj // nj, i, 0)),
                  pl.BlockSpec((None, D, tn), lambda jj, i: (jj // nj, 0, n0 // tn + jj % nj))],
        out_specs=pl.BlockSpec((None, tm, tn), lambda jj, i: (jj // nj, i, jj % nj)),
        out_shape=jax.ShapeDtypeStruct((depth, M, n), F32),
        scratch_shapes=[pltpu.VMEM((D, tn), BF16)],
        compiler_params=_params(("parallel", "arbitrary"), vmem + (2 << 20)),
        name="mem_proj",
    )(h, w_kv)


def _gates_kernel(h_ref, w_ref, b_ref, o_ref):
    nh = MLSTM_HEADS
    pre = lax.dot_general(h_ref[...], w_ref[...], (((1,), (1,)), ((), ())),
                          preferred_element_type=F32) + b_ref[...]
    capped = GATE_SOFTCAP * jnp.tanh(pre / GATE_SOFTCAP)
    log_sig = jnp.minimum(capped, 0.0) - jnp.log(1.0 + jnp.exp(-jnp.abs(capped)))
    lane = lax.broadcasted_iota(jnp.int32, capped.shape, 1)
    o_ref[...] = jnp.where(lane < nh, capped, log_sig)


def mlstm_gates(h, w_gates, bias, tm=512):
    M, D = h.shape
    tm = min(tm, M)
    return pl.pallas_call(
        _gates_kernel,
        grid=(M // tm,),
        in_specs=[pl.BlockSpec((tm, D), lambda i: (i, 0)),
                  pl.BlockSpec((GATE_LANES, D), lambda i: (0, 0)),
                  pl.BlockSpec((1, GATE_LANES), lambda i: (0, 0))],
        out_specs=pl.BlockSpec((tm, GATE_LANES), lambda i: (i, 0)),
        out_shape=jax.ShapeDtypeStruct((M, GATE_LANES), F32),
        compiler_params=_params(("parallel",), 4 * tm * D * 2 + 4 * D * GATE_LANES * 2),
        name="mlstm_gates",
    )(h, w_gates, bias)


def _group_mm_kernel(a_ref, w_ref, s_ref, z_ref, o_ref):
    mixed = jnp.dot(a_ref[...], w_ref[...], preferred_element_type=F32) * s_ref[...]
    o_ref[...] = (mixed * _silu(z_ref[...].astype(F32))).astype(o_ref.dtype)


def group_mix_gate(pooled, w_group, scale, proj, z_off, tm=512):
    M, D = pooled.shape
    G, Gs, _ = w_group.shape
    tm = min(tm, M)
    zb = z_off // Gs
    return pl.pallas_call(
        _group_mm_kernel,
        grid=(G, M // tm),
        in_specs=[pl.BlockSpec((tm, Gs), lambda g, i: (i, g)),
                  pl.BlockSpec((None, Gs, Gs), lambda g, i: (g, 0, 0)),
                  pl.BlockSpec((1, Gs), lambda g, i: (0, g)),
                  pl.BlockSpec((tm, Gs), lambda g, i: (i, zb + g))],
        out_specs=pl.BlockSpec((tm, Gs), lambda g, i: (i, g)),
        out_shape=jax.ShapeDtypeStruct((M, D), BF16),
        compiler_params=_params(("parallel", "parallel"), 6 * tm * Gs * 4 + 4 * Gs * Gs * 2),
        name="group_mix_gate",
    )(pooled, w_group, scale.reshape(1, D), proj)


def _window_sums(ext, levels):
    sums = []
    cur = ext
    for lv in range(levels):
        cur = cur + pltpu.roll(cur, 1 << lv, 0)
        sums.append(cur)
    return sums


def _pool_mix_prompt_kernel(u_ref, halo_ref, hist_ref, zlo_ref, zhi_ref, w_ref, s_ref, o_ref, *, pos0, tt):
    i = pl.program_id(1)
    ng = len(POOL_WINDOWS)
    gs = u_ref.shape[1] // ng
    row = lax.broadcasted_iota(jnp.int32, (tt, 1), 0) + i * tt
    pos1 = (row + (pos0 + 1)).astype(F32)
    first = i == 0
    for g, w in enumerate(POOL_WINDOWS):
        lo, hi = g * gs, (g + 1) * gs
        u = u_ref[:, lo:hi].astype(F32)
        halo = jnp.where(first, hist_ref[:, lo:hi], halo_ref[:, lo:hi].astype(F32))
        ext = jnp.concatenate([halo, u], axis=0)
        win = _window_sums(ext, g + 1)[-1][HALO:, :]
        inv_cnt = 1.0 / jnp.minimum(float(w), pos1)
        pooled = (win * inv_cnt - u).astype(BF16)
        mixed = jnp.dot(pooled, w_ref[g], preferred_element_type=F32) * s_ref[:, lo:hi]
        z_ref, zg = (zlo_ref, g) if g < ng // 2 else (zhi_ref, g - ng // 2)
        z = z_ref[:, zg * gs:(zg + 1) * gs].astype(F32)
        o_ref[:, lo:hi] = (mixed * _silu(z)).astype(o_ref.dtype)


def pool_mix_prompt(proj, hist, w_group, scale, batch, pos0, z_off, tt=512):
    D = hist.shape[-1]
    G, Gs, _ = w_group.shape
    T = proj.shape[0] // batch
    tt = min(tt, T)
    nt = T // tt
    dh = D // 2
    assert z_off % dh == 0 and G == len(POOL_WINDOWS)
    zb = z_off // dh
    hist16 = jnp.concatenate([jnp.zeros((batch, 1, D), F32), hist.astype(F32)], axis=1)
    vmem = 2 * (3 * tt * D * 2 + G * Gs * Gs * 2) + 12 * (tt + HALO) * Gs * 4
    return pl.pallas_call(
        functools.partial(_pool_mix_prompt_kernel, pos0=pos0, tt=tt),
        grid=(batch, nt),
        in_specs=[pl.BlockSpec((tt, D), lambda b, i: (b * nt + i, 0)),
                  pl.BlockSpec((HALO, D),
                               lambda b, i: (jnp.maximum((b * nt + i) * (tt // HALO) - 1, 0), 0)),
                  pl.BlockSpec((None, HALO, D), lambda b, i: (b, 0, 0)),
                  pl.BlockSpec((tt, dh), lambda b, i: (b * nt + i, zb)),
                  pl.BlockSpec((tt, dh), lambda b, i: (b * nt + i, zb + 1)),
                  pl.BlockSpec((G, Gs, Gs), lambda b, i: (0, 0, 0)),
                  pl.BlockSpec((1, D), lambda b, i: (0, 0))],
        out_specs=pl.BlockSpec((tt, D), lambda b, i: (b * nt + i, 0)),
        out_shape=jax.ShapeDtypeStruct((batch * T, D), BF16),
        compiler_params=_params(("parallel", "parallel"), vmem + (4 << 20)),
        name="pool_mix_prompt",
    )(proj, proj, hist16, proj, proj, w_group, scale.reshape(1, D))


def _pool_sample_kernel(hist_ref, u_ref, o_ref, new_hist_ref, *, pos0):
    steps = u_ref.shape[0]
    gs = u_ref.shape[2] // len(POOL_WINDOWS)

    def row(r, lo, hi):
        if r < POOL_HIST:
            return hist_ref[r, :, lo:hi]
        return u_ref[r - POOL_HIST, :, lo:hi].astype(F32)

    for g, w in enumerate(POOL_WINDOWS):
        lo, hi = g * gs, (g + 1) * gs
        for t in range(steps):
            cur = row(POOL_HIST + t, lo, hi)
            win = cur
            for r in range(1, w):
                win = win + row(POOL_HIST + t - r, lo, hi)
            cnt = float(min(w, pos0 + t + 1))
            o_ref[t, :, lo:hi] = (win / cnt - cur).astype(o_ref.dtype)
    for r in range(POOL_HIST):
        new_hist_ref[r] = row(r + steps, 0, u_ref.shape[2])


def pool_sample(hist_t, u_t, pos0, bt=32):
    T, B, D = u_t.shape
    bt = min(bt, B)
    return pl.pallas_call(
        functools.partial(_pool_sample_kernel, pos0=pos0),
        grid=(B // bt,),
        in_specs=[pl.BlockSpec((POOL_HIST, bt, D), lambda i: (0, i, 0)),
                  pl.BlockSpec((T, bt, D), lambda i: (0, i, 0))],
        out_specs=[pl.BlockSpec((T, bt, D), lambda i: (0, i, 0)),
                   pl.BlockSpec((POOL_HIST, bt, D), lambda i: (0, i, 0))],
        out_shape=[jax.ShapeDtypeStruct((T, B, D), BF16), jax.ShapeDtypeStruct((POOL_HIST, B, D), F32)],
        compiler_params=_params(("parallel",), 5 * (POOL_HIST + T) * bt * D * 4),
        name="pool_sample",
    )(hist_t, u_t)


def _softmax_rows(s):
    m = jnp.max(s, axis=-1, keepdims=True)
    p = jnp.exp(s - m)
    return p, jnp.sum(p, axis=-1, keepdims=True)


def _attn_prompt_kernel(q_ref, k_ref, v_ref, z_ref, o_ref):
    hd = q_ref.shape[1] // XATT_HEADS
    scale = hd ** -0.5
    for h in range(XATT_HEADS):
        lo, hi = h * hd, (h + 1) * hd
        s = lax.dot_general(q_ref[:, lo:hi], k_ref[:, lo:hi].astype(BF16), (((1,), (1,)), ((), ())),
                            preferred_element_type=F32) * scale
        p, l = _softmax_rows(s)
        o = jnp.dot(p.astype(BF16), v_ref[:, lo:hi].astype(BF16), preferred_element_type=F32) / l
        o_ref[:, lo:hi] = (o * _silu(z_ref[:, lo:hi].astype(F32))).astype(o_ref.dtype)


def attend_prompt(q_arr, qb, z_arr, zb, k, v, layer, S, tq=512):
    _, BS, DX = k.shape
    B = BS // S
    T = q_arr.shape[0] // B
    tq = min(tq, T)
    nq = T // tq
    return pl.pallas_call(
        _attn_prompt_kernel,
        grid=(B, nq),
        in_specs=[pl.BlockSpec((tq, DX), lambda b, i: (b * nq + i, qb)),
                  pl.BlockSpec((None, S, DX), lambda b, i: (layer, b, 0)),
                  pl.BlockSpec((None, S, DX), lambda b, i: (layer, b, 0)),
                  pl.BlockSpec((tq, DX), lambda b, i: (b * nq + i, zb))],
        out_specs=pl.BlockSpec((tq, DX), lambda b, i: (b * nq + i, 0)),
        out_shape=jax.ShapeDtypeStruct((B * T, DX), BF16),
        compiler_params=_params(("parallel", "parallel"), 8 * tq * DX * 4 + 8 * S * DX * 4),
        name="attend_prompt",
    )(q_arr, k, v, z_arr)


def _attn_sample_kernel(q_ref, k_hbm, v_hbm, z_ref, o_ref, kbuf, vbuf, sem, *, steps, layer, bb):
    i = pl.program_id(0)
    rows, dx = q_ref.shape
    hd = dx // XATT_HEADS
    scale = hd ** -0.5

    def head_copies(step, slot):
        out = []
        for t, (src, dst) in enumerate(((k_hbm, kbuf), (v_hbm, vbuf))):
            for h in range(XATT_HEADS):
                out.append(pltpu.make_async_copy(src.at[layer, pl.ds(step * bb, bb), :, h, :],
                                                 dst.at[slot, h], sem.at[t, slot, h]))
        return out

    @pl.when(i == 0)
    def _():
        for cp in head_copies(0, 0):
            cp.start()

    @pl.when(i + 1 < pl.num_programs(0))
    def _():
        for cp in head_copies(i + 1, (i + 1) % 2):
            cp.start()

    slot = i % 2
    for cp in head_copies(i, slot):
        cp.wait()

    row_batch = lax.broadcasted_iota(jnp.int32, (rows, 1), 0) // steps
    pairs = [(h, j) for h in range(XATT_HEADS) for j in range(bb)]
    scores = [lax.dot_general(q_ref[:, h * hd:(h + 1) * hd], kbuf[slot, h, j].astype(BF16),
                              (((1,), (1,)), ((), ())), preferred_element_type=F32) * scale
              for h, j in pairs]
    probs = [_softmax_rows(s) for s in scores]
    outs = [jnp.dot(p.astype(BF16), vbuf[slot, h, j].astype(BF16), preferred_element_type=F32) / l
            for (h, j), (p, l) in zip(pairs, probs)]
    for h in range(XATT_HEADS):
        lo, hi = h * hd, (h + 1) * hd
        acc = jnp.zeros((rows, hd), F32)
        for j in range(bb):
            acc = jnp.where(row_batch == j, outs[h * bb + j], acc)
        o_ref[:, lo:hi] = (acc * _silu(z_ref[:, lo:hi].astype(F32))).astype(o_ref.dtype)


def attend_sample(q_arr, qb, z_arr, zb, k, v, layer, steps, bb=4):
    _, B, S, nh, hd = k.shape
    assert nh == XATT_HEADS
    DX = nh * hd
    rows = bb * steps
    slot_bytes = nh * bb * S * hd * 4
    return pl.pallas_call(
        functools.partial(_attn_sample_kernel, steps=steps, layer=layer, bb=bb),
        grid=(B // bb,),
        in_specs=[pl.BlockSpec((rows, DX), lambda i: (i, qb)),
                  pl.BlockSpec(memory_space=pl.ANY), pl.BlockSpec(memory_space=pl.ANY),
                  pl.BlockSpec((rows, DX), lambda i: (i, zb))],
        out_specs=pl.BlockSpec((rows, DX), lambda i: (i, 0)),
        out_shape=jax.ShapeDtypeStruct((B * steps, DX), BF16),
        scratch_shapes=[pltpu.VMEM((2, nh, bb, S, hd), F32), pltpu.VMEM((2, nh, bb, S, hd), F32),
                        pltpu.SemaphoreType.DMA((2, 2, nh))],
        compiler_params=_params(("arbitrary",), 4 * slot_bytes + (12 << 20)),
        name="attend_sample",
    )(q_arr, k, v, z_arr)


def _head_output(num, inv_den, o, z, g):
    ms = jnp.mean(num * num, axis=-1, keepdims=True) * (inv_den * inv_den)
    row_scale = inv_den * lax.rsqrt(ms + NORM_EPS)
    return (num * row_scale) * (g * (_sigmoid(o) * _silu(z)))


def _mlstm_prompt_kernel(q_ref, k_ref, v_ref, o_ref, zlo_ref, zhi_ref, g_ref, gh_ref,
                         y_ref, c_out, n_out, m_out, ct_scr, n_scr, m_scr):
    c = pl.program_id(1)
    nh = MLSTM_HEADS
    L = q_ref.shape[0]
    dqk = q_ref.shape[1] // nh
    dv = v_ref.shape[1] // nh
    qscale = dqk ** -0.5

    @pl.when(c == 0)
    def _():
        ct_scr[...] = jnp.zeros_like(ct_scr)
        n_scr[...] = jnp.zeros_like(n_scr)
        m_scr[...] = jnp.zeros_like(m_scr)

    r_i = lax.broadcasted_iota(jnp.int32, (L, L), 0)
    c_i = lax.broadcasted_iota(jnp.int32, (L, L), 1)
    causal = c_i <= r_i
    gates = g_ref[...]

    tril16 = jnp.where(causal, 1.0, 0.0).astype(BF16)
    g_hi = gates.astype(BF16)
    rest = gates - g_hi.astype(F32)
    g_mid = rest.astype(BF16)
    g_lo = (rest - g_mid.astype(F32)).astype(BF16)
    cums = (jnp.dot(tril16, g_hi, preferred_element_type=F32)
            + jnp.dot(tril16, g_mid, preferred_element_type=F32)
            + jnp.dot(tril16, g_lo, preferred_element_type=F32))
    gates_t = gates.T
    cums_t = cums.T

    def decay_weights(h):
        m_prev = m_scr[h:h + 1, 0:1]
        b_col = cums[:, nh + h:nh + h + 1]
        rel_row = gates_t[h:h + 1, :] - cums_t[nh + h:nh + h + 1, :]
        b2 = b_col * LOG2E
        d2 = jnp.where(causal, b2 + rel_row * LOG2E, -jnp.inf)
        inter2 = b2 + m_prev * LOG2E
        m2 = jnp.maximum(inter2, jnp.max(d2, axis=1, keepdims=True))
        return dict(m_prev=m_prev, b_col=b_col, m2=m2, m_t=m2 * LN2,
                    w=jnp.exp2(d2 - m2), w_inter=jnp.exp2(inter2 - m2))

    def read_out(h, st):
        q = q_ref[:, h * dqk:(h + 1) * dqk].astype(F32) * qscale
        k = k_ref[:, h * dqk:(h + 1) * dqk]
        v = v_ref[:, h * dv:(h + 1) * dv]
        s = lax.dot_general(q.astype(BF16), k, (((1,), (1,)), ((), ())), preferred_element_type=F32) * st["w"]
        qw = q * st["w_inter"]
        num = (jnp.dot(qw.astype(BF16), ct_scr[h].astype(BF16), preferred_element_type=F32)
               + jnp.dot(s.astype(BF16), v, preferred_element_type=F32))
        den = (jnp.sum(s, axis=1, keepdims=True)
               + jnp.sum(qw * n_scr[h:h + 1, :], axis=1, keepdims=True))
        return num, 1.0 / jnp.maximum(jnp.abs(den), jnp.exp2(-st["m2"]))

    def update_state(h, st):
        k = k_ref[:, h * dqk:(h + 1) * dqk]
        v = v_ref[:, h * dv:(h + 1) * dv]
        b_col, m_prev = st["b_col"], st["m_prev"]
        b_last = b_col[L - 1:L, :]
        m_new = st["m_t"][L - 1:L, :]
        w_state = jnp.exp(b_last - b_col + gates[:, h:h + 1] - m_new)
        decay = jnp.exp(b_last + m_prev - m_new)
        kw = k.astype(F32) * w_state
        ct_scr[h] = decay * ct_scr[h] + lax.dot_general(kw.astype(BF16), v, (((0,), (0,)), ((), ())),
                                                        preferred_element_type=F32)
        n_scr[h:h + 1, :] = decay * n_scr[h:h + 1, :] + jnp.sum(kw, axis=0, keepdims=True)
        m_scr[h:h + 1, :] = jnp.broadcast_to(m_new, (1, m_scr.shape[1]))

    def write_out(h, num, inv_den):
        lo, hi = h * dv, (h + 1) * dv
        z_ref, zh = (zlo_ref, h) if h < nh // 2 else (zhi_ref, h - nh // 2)
        y_ref[:, lo:hi] = _head_output(num, inv_den, o_ref[:, lo:hi].astype(F32),
                                       z_ref[:, zh * dv:(zh + 1) * dv].astype(F32),
                                       gh_ref[:, lo:hi]).astype(y_ref.dtype)

    for h0 in range(0, nh, MLSTM_HEAD_GROUP):
        group = range(h0, h0 + MLSTM_HEAD_GROUP)
        stats = {h: decay_weights(h) for h in group}
        outs = {h: read_out(h, stats[h]) for h in group}
        for h in group:
            update_state(h, stats[h])
        for h in group:
            write_out(h, *outs[h])

    @pl.when(c == pl.num_programs(1) - 1)
    def _():
        for h in range(nh):
            c_out[h] = ct_scr[h].T
        n_out[...] = n_scr[...]
        m_out[...] = m_scr[...]


def mlstm_prompt(proj, proj_z, gates, g_head, batch, dqk, dv):
    nh = MLSTM_HEADS
    T = proj.shape[0] // batch
    L = min(MLSTM_CHUNK, T)
    nc = T // L
    wq, wv = nh * dqk, nh * dv
    assert wv == 2 * wq
    rows = lambda b, c: b * nc + c
    return pl.pallas_call(
        _mlstm_prompt_kernel,
        grid=(batch, nc),
        in_specs=[pl.BlockSpec((L, wq), lambda b, c: (rows(b, c), 0)),
                  pl.BlockSpec((L, wq), lambda b, c: (rows(b, c), 1)),
                  pl.BlockSpec((L, wv), lambda b, c: (rows(b, c), 1)),
                  pl.BlockSpec((L, wv), lambda b, c: (rows(b, c), 2)),
                  pl.BlockSpec((L, wq), lambda b, c: (rows(b, c), 1)),
                  pl.BlockSpec((L, wq), lambda b, c: (rows(b, c), 2)),
                  pl.BlockSpec((L, GATE_LANES), lambda b, c: (rows(b, c), 0)),
                  pl.BlockSpec((1, wv), lambda b, c: (0, 0))],
        out_specs=[pl.BlockSpec((L, wv), lambda b, c: (rows(b, c), 0)),
                   pl.BlockSpec((None, nh, dv, dqk), lambda b, c: (b, 0, 0, 0)),
                   pl.BlockSpec((None, nh, dqk), lambda b, c: (b, 0, 0)),
                   pl.BlockSpec((None, nh, GATE_LANES), lambda b, c: (b, 0, 0))],
        out_shape=[jax.ShapeDtypeStruct((batch * T, wv), BF16),
                   jax.ShapeDtypeStruct((batch, nh, dv, dqk), F32),
                   jax.ShapeDtypeStruct((batch, nh, dqk), F32),
                   jax.ShapeDtypeStruct((batch, nh, GATE_LANES), F32)],
        scratch_shapes=[pltpu.VMEM((nh, dqk, dv), F32),
                        pltpu.VMEM((nh, dqk), F32),
                        pltpu.VMEM((nh, GATE_LANES), F32)],
        compiler_params=_params(("parallel", "arbitrary"),
                                24 * L * wv + 5 * nh * dv * dqk * 4 + (16 << 20)),
        name="mlstm_prompt",
    )(proj, proj, proj, proj, proj_z, proj_z, gates, g_head.reshape(1, wv))


def _lane_pick(x, lane, idx):
    return jnp.sum(jnp.where(lane == idx, x, 0.0), axis=1, keepdims=True)


def _mlstm_sample_kernel(q_ref, k_ref, v_ref, o_ref, z_ref, g_ref, gh_ref, c_ref, n_ref,
                         y_ref, c_out, n_out, m_out, *, steps):
    head = pl.program_id(1)
    nh = MLSTM_HEADS
    R, dqk = q_ref.shape
    nb = R // steps
    qscale = dqk ** -0.5

    gates = g_ref[...]
    lane = lax.broadcasted_iota(jnp.int32, gates.shape, 1)
    ig = _lane_pick(gates, lane, head)
    lf = _lane_pick(gates, lane, nh + head)
    m_prev = _lane_pick(gates, lane, 2 * nh + head)

    r_i = lax.broadcasted_iota(jnp.int32, (R, R), 0)
    c_i = lax.broadcasted_iota(jnp.int32, (R, R), 1)
    same = (r_i // steps) == (c_i // steps)
    causal = same & (c_i <= r_i)
    eye = (c_i == r_i).astype(F32)
    lower = causal.astype(F32)
    upper = (same & (r_i <= c_i)).astype(F32)
    last_of_row_batch = (c_i == (r_i // steps) * steps + (steps - 1)).astype(F32)
    row_batch = lax.broadcasted_iota(jnp.int32, (R, 1), 0) // steps

    q = q_ref[...].astype(F32) * qscale
    qb = q.astype(BF16)
    k = k_ref[...]
    v = v_ref[...]

    lf_row = jnp.sum(lf * eye, axis=0, keepdims=True)
    ig_row = jnp.sum(ig * eye, axis=0, keepdims=True)
    b_col = jnp.sum(lower * lf_row, axis=1, keepdims=True)
    b_row = jnp.sum(upper * lf, axis=0, keepdims=True)
    d = jnp.where(causal, b_col - b_row + ig_row, -jnp.inf)
    inter = b_col + m_prev
    m_t = jnp.maximum(inter, jnp.max(d, axis=1, keepdims=True))
    w = jnp.exp(d - m_t)
    w_inter = jnp.exp(inter - m_t)
    s = lax.dot_general(qb, k, (((1,), (1,)), ((), ())), preferred_element_type=F32) * w

    qc = jnp.zeros((R, v_ref.shape[1]), F32)
    n_rows = jnp.zeros((R, dqk), F32)
    for j in range(nb):
        mine = row_batch == j
        qcj = lax.dot_general(qb, c_ref[j].astype(BF16), (((1,), (1,)), ((), ())),
                              preferred_element_type=F32)
        qc = jnp.where(mine, qcj, qc)
        n_rows = jnp.where(mine, n_ref[j:j + 1, :], n_rows)

    num = jnp.dot(s.astype(BF16), v, preferred_element_type=F32) + w_inter * qc
    den = jnp.sum(s, axis=1, keepdims=True) + w_inter * jnp.sum(q * n_rows, axis=1, keepdims=True)
    inv_den = 1.0 / jnp.maximum(jnp.abs(den), jnp.exp(-m_t))
    y_ref[...] = _head_output(num, inv_den, o_ref[...].astype(F32), z_ref[...].astype(F32),
                              gh_ref[...]).astype(y_ref.dtype)

    m_row = jnp.sum(m_t * eye, axis=0, keepdims=True)
    b_last = jnp.sum(last_of_row_batch * b_row, axis=1, keepdims=True)
    m_new = jnp.sum(last_of_row_batch * m_row, axis=1, keepdims=True)
    w_state = jnp.exp(b_last - b_col + ig - m_new)
    decay = jnp.exp(b_last + m_prev - m_new)
    vw = v.astype(F32) * w_state
    kw = k.astype(F32) * w_state
    for j in range(nb):
        mine = row_batch == j
        r0 = j * steps
        dj = decay[r0:r0 + 1, :]
        upd = lax.dot_general(jnp.where(mine, vw, 0.0).astype(BF16), k, (((0,), (0,)), ((), ())),
                              preferred_element_type=F32)
        c_out[j] = dj * c_ref[j] + upd
        n_out[j:j + 1, :] = dj * n_ref[j:j + 1, :] + jnp.sum(jnp.where(mine, kw, 0.0), axis=0, keepdims=True)
        m_out[j:j + 1, :] = jnp.broadcast_to(m_new[r0:r0 + 1, :], (1, m_out.shape[1]))


def mlstm_sample(proj, proj_z, gates, g_head, c0, n0, steps, dqk, dv):
    nh = MLSTM_HEADS
    B = c0.shape[0]
    nb = SAMPLE_BATCH_BLOCK
    R = nb * steps
    kq = nh
    kv = (2 * nh * dqk) // dv
    ko = kv + nh
    kz = (nh * dqk) // dv
    return pl.pallas_call(
        functools.partial(_mlstm_sample_kernel, steps=steps),
        grid=(B // nb, nh),
        in_specs=[pl.BlockSpec((R, dqk), lambda i, h: (i, h)),
                  pl.BlockSpec((R, dqk), lambda i, h: (i, kq + h)),
                  pl.BlockSpec((R, dv), lambda i, h: (i, kv + h)),
                  pl.BlockSpec((R, dv), lambda i, h: (i, ko + h)),
                  pl.BlockSpec((R, dv), lambda i, h: (i, kz + h)),
                  pl.BlockSpec((R, GATE_LANES), lambda i, h: (i, 0)),
                  pl.BlockSpec((1, dv), lambda i, h: (0, h)),
                  pl.BlockSpec((nb, None, dv, dqk), lambda i, h: (i, h, 0, 0)),
                  pl.BlockSpec((None, nb, dqk), lambda i, h: (h, i, 0))],
        out_specs=[pl.BlockSpec((R, dv), lambda i, h: (i, h)),
                   pl.BlockSpec((nb, None, dv, dqk), lambda i, h: (i, h, 0, 0)),
                   pl.BlockSpec((None, nb, dqk), lambda i, h: (h, i, 0)),
                   pl.BlockSpec((None, nb, GATE_LANES), lambda i, h: (h, i, 0))],
        out_shape=[jax.ShapeDtypeStruct((B * steps, nh * dv), BF16),
                   jax.ShapeDtypeStruct(c0.shape, F32),
                   jax.ShapeDtypeStruct((nh, B, dqk), F32),
                   jax.ShapeDtypeStruct((nh, B, GATE_LANES), F32)],
        compiler_params=_params(("parallel", "parallel"), 6 * nb * dv * dqk * 4 + (8 << 20)),
        name="mlstm_sample",
    )(proj, proj, proj, proj, proj_z, gates, g_head.reshape(1, nh * dv), c0, n0)


def _finish_casting(x, h_mix, h_att, w_out, layer, g_post, g_next):
    y, w16 = matmul_cast([h_mix, h_att], w_out, layer, BF16, tm=512, tn=256)
    return post_norm_residual(x, y, g_post, g_next) + (w16,)


def _finish(x, h_mix, h_att, w16, g_post, g_next):
    y = matmul([h_mix, h_att], w16[None], BF16, tm=512, tn=1024)
    return post_norm_residual(x, y, g_post, g_next)


def kernel(x_prompt, x_sample, mem_prompt, state_pool, state_mlstm_C, state_mlstm_n, state_mlstm_m,
           cache_mem_k, cache_mem_v, norm_pre, norm_post, norm_mem, w_mem_kv, w_out,
           w_in_pool, w_pool_group, pool_scale, w_in_mlstm, b_igate, b_fgate, mlstm_head_norm):
    B, T, D = x_prompt.shape
    Bs, Ts, _ = x_sample.shape
    S = mem_prompt.shape[1]
    depth = w_out.shape[0]
    nh = MLSTM_HEADS
    dx = D // 2
    dv = D // nh
    dqk = dv // 2
    hd = dx // XATT_HEADS

    xp = x_prompt.reshape(B * T, D)
    xs = x_sample.reshape(Bs * Ts, D)
    mem = mem_prompt.reshape(B * S, D)
    hp = rmsnorm_cast(xp, norm_pre[:1])[0]
    hs = rmsnorm_cast(xs, norm_pre[:1])[0]

    h_mem = rmsnorm_cast(mem, norm_mem)
    mem_k = mem_proj(h_mem, w_mem_kv, 0, dx)
    mem_v = mem_proj(h_mem, w_mem_kv, dx, dx)
    w_mlstm_t = jnp.swapaxes(w_in_mlstm, 1, 2)

    pool_p_l, pool_s_l = [], []
    cp_l, np_l, mp_l, cs_l, ns_l, ms_l = [], [], [], [], [], []
    for layer in range(depth):
        j = layer // 2
        g_next = norm_pre[layer + 1] if layer + 1 < depth else None
        if layer % 2 == 0:
            w_g = w_pool_group[j].astype(BF16)
            z_mix, qb, zb = D + dx, D // dx, (2 * D + dx) // dx
            proj, w_pool16 = matmul_cast([hs], w_in_pool, j, BF16)
            u_t = proj[:, :D].reshape(Bs, Ts, D).transpose(1, 0, 2)
            pooled, hist_t = pool_sample(state_pool[j].transpose(1, 0, 2), u_t, PAST_LEN)
            pooled = pooled.transpose(1, 0, 2).reshape(Bs * Ts, D)
            mix = group_mix_gate(pooled, w_g, pool_scale[j], proj, z_mix)
            att = attend_sample(proj, qb, proj, zb, cache_mem_k, cache_mem_v, layer, Ts)
            xs, hs, w_o16 = _finish_casting(xs, mix, att, w_out, layer, norm_post[layer], g_next)
            pool_s_l.append(hist_t.transpose(1, 0, 2))
            proj = matmul([hp], w_pool16[None], BF16)
            mix = pool_mix_prompt(proj, jnp.zeros((B, POOL_HIST, D), F32), w_g, pool_scale[j], B, 0, z_mix)
            att = attend_prompt(proj, qb, proj, zb, mem_k, mem_v, layer, S)
            xp, hp = _finish(xp, mix, att, w_o16, norm_post[layer], g_next)
            pool_p_l.append(proj.reshape(B, T, -1)[:, T - POOL_HIST:, :D].astype(F32))
        else:
            bias = jnp.pad(jnp.concatenate([b_igate[j], b_fgate[j]]), (0, GATE_LANES - 2 * nh))
            bias = bias.reshape(1, GATE_LANES)
            qb, zb = 0, (dx + D) // dx
            g0 = 3 * D
            w_gates = jnp.pad(w_mlstm_t[j, g0:g0 + 2 * nh, :], ((0, GATE_LANES - 2 * nh), (0, 0))).astype(BF16)
            proj, w_main = matmul_cast([hs], w_mlstm_t, j, BF16, 0, g0, transposed=True)
            proj_z, w_z = matmul_cast([hs], w_mlstm_t, j, BF16, g0 + 2 * nh, None, transposed=True)
            gates = mlstm_gates(hs, w_gates, bias)
            m_rows = jnp.repeat(state_mlstm_m[j], Ts, axis=0)
            gates = jnp.concatenate([gates[:, :2 * nh], m_rows,
                                     jnp.zeros((Bs * Ts, GATE_LANES - 3 * nh), F32)], axis=1)
            mix, c1, n1, m1 = mlstm_sample(proj, proj_z, gates, mlstm_head_norm[j], state_mlstm_C[j],
                                           state_mlstm_n[j].transpose(1, 0, 2), Ts, dqk, dv)
            att = attend_sample(proj_z, qb, proj_z, zb, cache_mem_k, cache_mem_v, layer, Ts)
            xs, hs, w_o16 = _finish_casting(xs, mix, att, w_out, layer, norm_post[layer], g_next)
            cs_l.append(c1); ns_l.append(n1.transpose(1, 0, 2)); ms_l.append(m1[:, :, 0].T)
            proj = matmul_nt(hp, w_main, BF16)
            proj_z = matmul_nt(hp, w_z, BF16)
            gates = mlstm_gates(hp, w_gates, bias)
            mix, c1, n1, m1 = mlstm_prompt(proj, proj_z, gates, mlstm_head_norm[j], B, dqk, dv)
            att = attend_prompt(proj_z, qb, proj_z, zb, mem_k, mem_v, layer, S)
            xp, hp = _finish(xp, mix, att, w_o16, norm_post[layer], g_next)
            cp_l.append(c1); np_l.append(n1); mp_l.append(m1[:, :, 0])

    return (xp.reshape(B, T, D), xs.reshape(Bs, Ts, D), jnp.stack(pool_p_l),
            mem_k.reshape(depth, B, S, XATT_HEADS, hd), mem_v.reshape(depth, B, S, XATT_HEADS, hd),
            jnp.stack(cp_l), jnp.stack(np_l), jnp.stack(mp_l),
            jnp.stack(pool_s_l), jnp.stack(cs_l), jnp.stack(ns_l), jnp.stack(ms_l))
```

```python
import functools

import jax
import jax.numpy as jnp
from jax import lax
from jax.experimental import pallas as pl
from jax.experimental.pallas import tpu as pltpu

F32 = jnp.float32
BF16 = jnp.bfloat16

NORM_EPS = 1e-6
GATE_SOFTCAP = 15.0
POOL_WINDOWS = (2, 4, 8, 16)
POOL_HIST = max(POOL_WINDOWS) - 1
HALO = POOL_HIST + 1
XATT_HEADS = 4
MLSTM_HEADS = 8
PAST_LEN = 16384
GATE_LANES = 128
MLSTM_CHUNK = 256
MLSTM_HEAD_GROUP = 2
SAMPLE_BATCH_BLOCK = 8

V7X_VMEM_BYTES = 64 * 1024 * 1024
VMEM_CAP = V7X_VMEM_BYTES - 8 * 1024 * 1024


def _params(semantics, vmem_bytes):
    limit = int(min(max(vmem_bytes, 16 * 1024 * 1024), VMEM_CAP))
    return pltpu.CompilerParams(dimension_semantics=semantics, vmem_limit_bytes=limit)


LOG2E = 1.4426950408889634
LN2 = 0.6931471805599453
NEG_LOG2E = -LOG2E


def _sigmoid(x):
    return 1.0 / (1.0 + jnp.exp2(x * NEG_LOG2E))


def _silu(x):
    return x * _sigmoid(x)


def _rms(x, g):
    r = lax.rsqrt(jnp.mean(x * x, axis=-1, keepdims=True) + NORM_EPS)
    return x * r * g


def _rmsnorm_kernel(x_ref, g_ref, o_ref):
    o_ref[...] = _rms(x_ref[...], g_ref[...]).astype(o_ref.dtype)


def rmsnorm_cast(x, g, tm=256):
    M, D = x.shape
    L = g.shape[0]
    tm = min(tm, M)
    return pl.pallas_call(
        _rmsnorm_kernel,
        grid=(L, M // tm),
        in_specs=[pl.BlockSpec((tm, D), lambda l, i: (i, 0)),
                  pl.BlockSpec((None, 1, D), lambda l, i: (l, 0, 0))],
        out_specs=pl.BlockSpec((None, tm, D), lambda l, i: (l, i, 0)),
        out_shape=jax.ShapeDtypeStruct((L, M, D), BF16),
        compiler_params=_params(("parallel", "parallel"), 8 * tm * D * 4),
        name="rmsnorm_cast",
    )(x, g.reshape(L, 1, D))


def _post_kernel(*refs, with_next):
    x_ref, y_ref, g_ref = refs[:3]
    out = x_ref[...] + _rms(y_ref[...].astype(F32), g_ref[...])
    if with_next:
        gn_ref, o_ref, h_ref = refs[3:]
        h_ref[...] = _rms(out, gn_ref[...]).astype(h_ref.dtype)
    else:
        o_ref = refs[3]
    o_ref[...] = out


def post_norm_residual(x, y, g_post, g_next=None, tm=256):
    M, D = x.shape
    tm = min(tm, M)
    row = pl.BlockSpec((tm, D), lambda i: (i, 0))
    vec = pl.BlockSpec((1, D), lambda i: (0, 0))
    with_next = g_next is not None
    args = [x, y, g_post.reshape(1, D)] + ([g_next.reshape(1, D)] if with_next else [])
    out_shape = [jax.ShapeDtypeStruct((M, D), F32)] + ([jax.ShapeDtypeStruct((M, D), BF16)] if with_next else [])
    res = pl.pallas_call(
        functools.partial(_post_kernel, with_next=with_next),
        grid=(M // tm,),
        in_specs=[row, row, vec] + ([vec] if with_next else []),
        out_specs=[row] * len(out_shape),
        out_shape=out_shape,
        compiler_params=_params(("parallel",), 12 * tm * D * 4),
        name="post_norm_residual",
    )(*args)
    return (res[0], res[1]) if with_next else (res[0], None)


BF16_ROW_TILE = 16


def _mm_kernel(*refs, ksizes):
    n_a = len(ksizes)
    a_refs, w_ref, o_ref = refs[:n_a], refs[n_a], refs[n_a + 1]
    acc = None
    off = 0
    for a_ref, ks in zip(a_refs, ksizes):
        part = jnp.dot(a_ref[...], w_ref[off:off + ks, :], preferred_element_type=F32)
        acc = part if acc is None else acc + part
        off += ks
    o_ref[...] = acc.astype(o_ref.dtype)


def matmul(a_list, w, out_dtype, layer=0, tm=1024, tn=1024):
    M = a_list[0].shape[0]
    _, K, N = w.shape
    ksizes = tuple(a.shape[1] for a in a_list)
    assert sum(ksizes) == K
    tm, tn = min(tm, M), min(tn, N)
    osz = jnp.dtype(out_dtype).itemsize
    vmem = 2 * (tm * K * 2 + K * tn * 2 + tm * tn * osz) + 2 * tm * tn * 4
    return pl.pallas_call(
        functools.partial(_mm_kernel, ksizes=ksizes),
        grid=(N // tn, M // tm),
        in_specs=[pl.BlockSpec((tm, ks), lambda j, i: (i, 0)) for ks in ksizes]
        + [pl.BlockSpec((None, K, tn), lambda j, i: (layer, 0, j))],
        out_specs=pl.BlockSpec((tm, tn), lambda j, i: (i, j)),
        out_shape=jax.ShapeDtypeStruct((M, N), out_dtype),
        compiler_params=_params(("parallel", "parallel"), vmem + (4 << 20)),
        name="matmul",
    )(*a_list, w)


def _mm_nt_kernel(a_ref, wt_ref, o_ref):
    o_ref[...] = lax.dot_general(a_ref[...], wt_ref[...], (((1,), (1,)), ((), ())),
                                 preferred_element_type=F32).astype(o_ref.dtype)


def matmul_nt(a, wt, out_dtype, tm=1024, tn=1024):
    M, K = a.shape
    N, _ = wt.shape
    tm, tn = min(tm, M), min(tn, N)
    osz = jnp.dtype(out_dtype).itemsize
    vmem = 2 * (tm * K * 2 + K * tn * 2 + tm * tn * osz) + 2 * tm * tn * 4
    return pl.pallas_call(
        _mm_nt_kernel,
        grid=(N // tn, M // tm),
        in_specs=[pl.BlockSpec((tm, K), lambda j, i: (i, 0)),
                  pl.BlockSpec((tn, K), lambda j, i: (j, 0))],
        out_specs=pl.BlockSpec((tm, tn), lambda j, i: (i, j)),
        out_shape=jax.ShapeDtypeStruct((M, N), out_dtype),
        compiler_params=_params(("parallel", "parallel"), vmem + (4 << 20)),
        name="matmul_nt",
    )(a, wt)


def _mm_castw_kernel(*refs, ksizes, shift, transposed):
    n_a = len(ksizes)
    a_refs = refs[:n_a]
    if shift:
        w_ref, e_ref, o_ref, wbf_ref = refs[n_a:]
    else:
        w_ref, o_ref, wbf_ref = refs[n_a:]

    @pl.when(pl.program_id(1) == 0)
    def _():
        if shift:
            keep = w_ref.shape[0] - shift
            wbf_ref[0:keep, :] = w_ref[shift:, :].astype(wbf_ref.dtype)
            wbf_ref[keep:, :] = e_ref[...].astype(wbf_ref.dtype)
        else:
            wbf_ref[...] = w_ref[...].astype(wbf_ref.dtype)

    if transposed:
        acc = lax.dot_general(a_refs[0][...], wbf_ref[...], (((1,), (1,)), ((), ())), preferred_element_type=F32)
    else:
        acc, off = None, 0
        for a_ref, ks in zip(a_refs, ksizes):
            part = jnp.dot(a_ref[...], wbf_ref[off:off + ks, :], preferred_element_type=F32)
            acc = part if acc is None else acc + part
            off += ks
    o_ref[...] = acc.astype(o_ref.dtype)


def matmul_cast(a_list, w, layer, out_dtype, n0=0, n=None, transposed=False, tm=1024, tn=512):
    M = a_list[0].shape[0]
    ksizes = tuple(a.shape[1] for a in a_list)
    K = sum(ksizes)
    assert not transposed or len(a_list) == 1
    n_all = w.shape[1] if transposed else w.shape[2]
    n = n_all - n0 if n is None else n
    tm, tn = min(tm, M), min(tn, n)
    shift = n0 % tn
    base = n0 - shift
    assert n % tn == 0 and M % tm == 0
    osz = jnp.dtype(out_dtype).itemsize
    if transposed:
        w_specs = [pl.BlockSpec((None, tn, K), lambda j, i: (layer, base // tn + j, 0))]
        if shift:
            assert shift % BF16_ROW_TILE == 0 and tn % shift == 0 and base % shift == 0
            w_specs.append(pl.BlockSpec((None, shift, K), lambda j, i: (layer, (base + (j + 1) * tn) // shift, 0)))
        wbf_spec = pl.BlockSpec((tn, K), lambda j, i: (j, 0))
        wbf_shape = (n, K)
    else:
        assert shift == 0
        w_specs = [pl.BlockSpec((None, K, tn), lambda j, i: (layer, 0, base // tn + j))]
        wbf_spec = pl.BlockSpec((K, tn), lambda j, i: (0, j))
        wbf_shape = (K, n)
    vmem = 2 * (tm * K * 2 + K * tn * 4 + K * tn * 2 + tm * tn * osz) + 2 * tm * tn * 4 + 2 * K * tn * 2
    return pl.pallas_call(
        functools.partial(_mm_castw_kernel, ksizes=ksizes, shift=shift, transposed=transposed),
        grid=(n // tn, M // tm),
        in_specs=[pl.BlockSpec((tm, ks), lambda j, i: (i, 0)) for ks in ksizes] + w_specs,
        out_specs=[pl.BlockSpec((tm, tn), lambda j, i: (i, j)), wbf_spec],
        out_shape=[jax.ShapeDtypeStruct((M, n), out_dtype), jax.ShapeDtypeStruct(wbf_shape, BF16)],
        compiler_params=_params(("parallel", "arbitrary"), vmem + (2 << 20)),
        name="matmul_cast",
    )(*a_list, *([w] * len(w_specs)))


def _mem_proj_kernel(a_ref, w_ref, o_ref, heads_hbm, wbf_scr, stage, sem, *, nheads, nb):
    jj, i = pl.program_id(0), pl.program_id(1)
    step = jj * pl.num_programs(1) + i
    last = pl.num_programs(0) * pl.num_programs(1) - 1
    slot = step % 2

    def head_copy(s):
        return pltpu.make_async_copy(stage.at[s], heads_hbm.at[jj // nheads, pl.ds(i * nb, nb), :, jj % nheads, :],
                                     sem.at[s])

    @pl.when(i == 0)
    def _():
        wbf_scr[...] = w_ref[...].astype(wbf_scr.dtype)

    kv = jnp.dot(a_ref[...], wbf_scr[...], preferred_element_type=F32)
    o_ref[...] = kv

    @pl.when(step >= 2)
    def _():
        head_copy(slot).wait()

    stage[slot] = kv.reshape(stage.shape[1:])
    head_copy(slot).start()

    @pl.when(step == last)
    def _():
        head_copy(slot).wait()

        @pl.when(step >= 1)
        def _():
            head_copy(1 - slot).wait()


def mem_proj(h, w_kv, n0, batch, nheads, tm=1024):
    depth, M, D = h.shape
    S = M // batch
    hd = (w_kv.shape[2] // 2) // nheads
    n = nheads * hd
    tm = min(tm, M)
    assert n0 % hd == 0 and M % tm == 0 and tm % S == 0
    nb = tm // S
    vmem = 2 * (tm * D * 2 + D * hd * 4 + tm * hd * 4) + 4 * tm * hd * 4 + 3 * D * hd * 2
    return pl.pallas_call(
        functools.partial(_mem_proj_kernel, nheads=nheads, nb=nb),
        grid=(depth * nheads, M // tm),
        in_specs=[pl.BlockSpec((None, tm, D), lambda jj, i: (jj // nheads, i, 0)),
                  pl.BlockSpec((None, D, hd), lambda jj, i: (jj // nheads, 0, n0 // hd + jj % nheads))],
        out_specs=[pl.BlockSpec((None, tm, hd), lambda jj, i: (jj // nheads, i, jj % nheads)),
                   pl.BlockSpec(memory_space=pl.ANY)],
        out_shape=[jax.ShapeDtypeStruct((depth, M, n), F32),
                   jax.ShapeDtypeStruct((depth, batch, S, nheads, hd), F32)],
        scratch_shapes=[pltpu.VMEM((D, hd), BF16), pltpu.VMEM((2, nb, S, hd), F32),
                        pltpu.SemaphoreType.DMA((2,))],
        compiler_params=_params(("arbitrary", "arbitrary"), vmem + (2 << 20)),
        name="mem_proj",
    )(h, w_kv)


def _gates_kernel(h_ref, w_ref, b_ref, o_ref):
    nh = MLSTM_HEADS
    pre = lax.dot_general(h_ref[...], w_ref[...], (((1,), (1,)), ((), ())),
                          preferred_element_type=F32) + b_ref[...]
    capped = GATE_SOFTCAP * jnp.tanh(pre / GATE_SOFTCAP)
    log_sig = jnp.minimum(capped, 0.0) - jnp.log(1.0 + jnp.exp(-jnp.abs(capped)))
    lane = lax.broadcasted_iota(jnp.int32, capped.shape, 1)
    o_ref[...] = jnp.where(lane < nh, capped, log_sig)


def mlstm_gates(h, w_gates, bias, tm=512):
    M, D = h.shape
    tm = min(tm, M)
    return pl.pallas_call(
        _gates_kernel,
        grid=(M // tm,),
        in_specs=[pl.BlockSpec((tm, D), lambda i: (i, 0)),
                  pl.BlockSpec((GATE_LANES, D), lambda i: (0, 0)),
                  pl.BlockSpec((1, GATE_LANES), lambda i: (0, 0))],
        out_specs=pl.BlockSpec((tm, GATE_LANES), lambda i: (i, 0)),
        out_shape=jax.ShapeDtypeStruct((M, GATE_LANES), F32),
        compiler_params=_params(("parallel",), 4 * tm * D * 2 + 4 * D * GATE_LANES * 2),
        name="mlstm_gates",
    )(h, w_gates, bias)


def _group_mm_kernel(a_ref, w_ref, s_ref, z_ref, o_ref):
    mixed = jnp.dot(a_ref[...], w_ref[...], preferred_element_type=F32) * s_ref[...]
    o_ref[...] = (mixed * _silu(z_ref[...].astype(F32))).astype(o_ref.dtype)


def group_mix_gate(pooled, w_group, scale, proj, z_off, tm=512):
    M, D = pooled.shape
    G, Gs, _ = w_group.shape
    tm = min(tm, M)
    zb = z_off // Gs
    return pl.pallas_call(
        _group_mm_kernel,
        grid=(G, M // tm),
        in_specs=[pl.BlockSpec((tm, Gs), lambda g, i: (i, g)),
                  pl.BlockSpec((None, Gs, Gs), lambda g, i: (g, 0, 0)),
                  pl.BlockSpec((1, Gs), lambda g, i: (0, g)),
                  pl.BlockSpec((tm, Gs), lambda g, i: (i, zb + g))],
        out_specs=pl.BlockSpec((tm, Gs), lambda g, i: (i, g)),
        out_shape=jax.ShapeDtypeStruct((M, D), BF16),
        compiler_params=_params(("parallel", "parallel"), 6 * tm * Gs * 4 + 4 * Gs * Gs * 2),
        name="group_mix_gate",
    )(pooled, w_group, scale.reshape(1, D), proj)


def _window_sums(ext, levels):
    sums = []
    cur = ext
    for lv in range(levels):
        cur = cur + pltpu.roll(cur, 1 << lv, 0)
        sums.append(cur)
    return sums


def _pool_mix_prompt_kernel(u_ref, halo_ref, hist_ref, zlo_ref, zhi_ref, w_ref, s_ref, o_ref, *, pos0, tt):
    i = pl.program_id(1)
    ng = len(POOL_WINDOWS)
    gs = u_ref.shape[1] // ng
    row = lax.broadcasted_iota(jnp.int32, (tt, 1), 0) + i * tt
    pos1 = (row + (pos0 + 1)).astype(F32)
    first = i == 0
    for g, w in enumerate(POOL_WINDOWS):
        lo, hi = g * gs, (g + 1) * gs
        u = u_ref[:, lo:hi].astype(F32)
        halo = jnp.where(first, hist_ref[:, lo:hi], halo_ref[:, lo:hi].astype(F32))
        ext = jnp.concatenate([halo, u], axis=0)
        win = _window_sums(ext, g + 1)[-1][HALO:, :]
        inv_cnt = 1.0 / jnp.minimum(float(w), pos1)
        pooled = (win * inv_cnt - u).astype(BF16)
        mixed = jnp.dot(pooled, w_ref[g], preferred_element_type=F32) * s_ref[:, lo:hi]
        z_ref, zg = (zlo_ref, g) if g < ng // 2 else (zhi_ref, g - ng // 2)
        z = z_ref[:, zg * gs:(zg + 1) * gs].astype(F32)
        o_ref[:, lo:hi] = (mixed * _silu(z)).astype(o_ref.dtype)


def pool_mix_prompt(proj, hist, w_group, scale, batch, pos0, z_off, tt=512):
    D = hist.shape[-1]
    G, Gs, _ = w_group.shape
    T = proj.shape[0] // batch
    tt = min(tt, T)
    nt = T // tt
    dh = D // 2
    assert z_off % dh == 0 and G == len(POOL_WINDOWS)
    zb = z_off // dh
    hist16 = jnp.concatenate([jnp.zeros((batch, 1, D), F32), hist.astype(F32)], axis=1)
    vmem = 2 * (3 * tt * D * 2 + G * Gs * Gs * 2) + 12 * (tt + HALO) * Gs * 4
    return pl.pallas_call(
        functools.partial(_pool_mix_prompt_kernel, pos0=pos0, tt=tt),
        grid=(batch, nt),
        in_specs=[pl.BlockSpec((tt, D), lambda b, i: (b * nt + i, 0)),
                  pl.BlockSpec((HALO, D),
                               lambda b, i: (jnp.maximum((b * nt + i) * (tt // HALO) - 1, 0), 0)),
                  pl.BlockSpec((None, HALO, D), lambda b, i: (b, 0, 0)),
                  pl.BlockSpec((tt, dh), lambda b, i: (b * nt + i, zb)),
                  pl.BlockSpec((tt, dh), lambda b, i: (b * nt + i, zb + 1)),
                  pl.BlockSpec((G, Gs, Gs), lambda b, i: (0, 0, 0)),
                  pl.BlockSpec((1, D), lambda b, i: (0, 0))],
        out_specs=pl.BlockSpec((tt, D), lambda b, i: (b * nt + i, 0)),
        out_shape=jax.ShapeDtypeStruct((batch * T, D), BF16),
        compiler_params=_params(("parallel", "parallel"), vmem + (4 << 20)),
        name="pool_mix_prompt",
    )(proj, proj, hist16, proj, proj, w_group, scale.reshape(1, D))


def _pool_sample_kernel(hist_ref, u_ref, o_ref, new_hist_ref, *, pos0):
    steps = u_ref.shape[0]
    gs = u_ref.shape[2] // len(POOL_WINDOWS)

    def row(r, lo, hi):
        if r < POOL_HIST:
            return hist_ref[r, :, lo:hi]
        return u_ref[r - POOL_HIST, :, lo:hi].astype(F32)

    for g, w in enumerate(POOL_WINDOWS):
        lo, hi = g * gs, (g + 1) * gs
        for t in range(steps):
            cur = row(POOL_HIST + t, lo, hi)
            win = cur
            for r in range(1, w):
                win = win + row(POOL_HIST + t - r, lo, hi)
            cnt = float(min(w, pos0 + t + 1))
            o_ref[t, :, lo:hi] = (win / cnt - cur).astype(o_ref.dtype)
    for r in range(POOL_HIST):
        new_hist_ref[r] = row(r + steps, 0, u_ref.shape[2])


def pool_sample(hist_t, u_t, pos0, bt=32):
    T, B, D = u_t.shape
    bt = min(bt, B)
    return pl.pallas_call(
        functools.partial(_pool_sample_kernel, pos0=pos0),
        grid=(B // bt,),
        in_specs=[pl.BlockSpec((POOL_HIST, bt, D), lambda i: (0, i, 0)),
                  pl.BlockSpec((T, bt, D), lambda i: (0, i, 0))],
        out_specs=[pl.BlockSpec((T, bt, D), lambda i: (0, i, 0)),
                   pl.BlockSpec((POOL_HIST, bt, D), lambda i: (0, i, 0))],
        out_shape=[jax.ShapeDtypeStruct((T, B, D), BF16), jax.ShapeDtypeStruct((POOL_HIST, B, D), F32)],
        compiler_params=_params(("parallel",), 5 * (POOL_HIST + T) * bt * D * 4),
        name="pool_sample",
    )(hist_t, u_t)


def _softmax_rows(qk, scale):
    m = jnp.max(qk, axis=-1, keepdims=True)
    p = jnp.exp2((qk - m) * (scale * LOG2E))
    return p, jnp.sum(p, axis=-1, keepdims=True)


def _attn_prompt_kernel(q_ref, k_ref, v_ref, z_ref, o_ref):
    hd = q_ref.shape[1] // XATT_HEADS
    scale = hd ** -0.5
    cols = [(h * hd, (h + 1) * hd) for h in range(XATT_HEADS)]
    scores = [lax.dot_general(q_ref[:, lo:hi], k_ref[:, lo:hi].astype(BF16), (((1,), (1,)), ((), ())),
                              preferred_element_type=F32) for lo, hi in cols]
    probs = [_softmax_rows(s, scale) for s in scores]
    outs = [jnp.dot(p.astype(BF16), v_ref[:, lo:hi].astype(BF16), preferred_element_type=F32) / l
            for (lo, hi), (p, l) in zip(cols, probs)]
    for (lo, hi), o in zip(cols, outs):
        o_ref[:, lo:hi] = (o * _silu(z_ref[:, lo:hi].astype(F32))).astype(o_ref.dtype)


def attend_prompt(q_arr, qb, z_arr, zb, k, v, layer, S, tq=512):
    _, BS, DX = k.shape
    B = BS // S
    T = q_arr.shape[0] // B
    tq = min(tq, T)
    nq = T // tq
    return pl.pallas_call(
        _attn_prompt_kernel,
        grid=(B, nq),
        in_specs=[pl.BlockSpec((tq, DX), lambda b, i: (b * nq + i, qb)),
                  pl.BlockSpec((None, S, DX), lambda b, i: (layer, b, 0)),
                  pl.BlockSpec((None, S, DX), lambda b, i: (layer, b, 0)),
                  pl.BlockSpec((tq, DX), lambda b, i: (b * nq + i, zb))],
        out_specs=pl.BlockSpec((tq, DX), lambda b, i: (b * nq + i, 0)),
        out_shape=jax.ShapeDtypeStruct((B * T, DX), BF16),
        compiler_params=_params(("parallel", "parallel"), 8 * tq * DX * 4 + 8 * S * DX * 4),
        name="attend_prompt",
    )(q_arr, k, v, z_arr)


def _attn_sample_kernel(q_ref, k_hbm, v_hbm, z_ref, o_ref, kbuf, vbuf, sem, *, steps, layer, bb):
    i = pl.program_id(0)
    rows, dx = q_ref.shape
    hd = dx // XATT_HEADS
    scale = hd ** -0.5

    def head_copies(step, slot):
        out = []
        for t, (src, dst) in enumerate(((k_hbm, kbuf), (v_hbm, vbuf))):
            for h in range(XATT_HEADS):
                out.append(pltpu.make_async_copy(src.at[layer, pl.ds(step * bb, bb), :, h, :],
                                                 dst.at[slot, h], sem.at[t, slot, h]))
        return out

    @pl.when(i == 0)
    def _():
        for cp in head_copies(0, 0):
            cp.start()

    @pl.when(i + 1 < pl.num_programs(0))
    def _():
        for cp in head_copies(i + 1, (i + 1) % 2):
            cp.start()

    slot = i % 2
    for cp in head_copies(i, slot):
        cp.wait()

    row_batch = lax.broadcasted_iota(jnp.int32, (rows, 1), 0) // steps
    pairs = [(h, j) for h in range(XATT_HEADS) for j in range(bb)]
    scores = [lax.dot_general(q_ref[:, h * hd:(h + 1) * hd], kbuf[slot, h, j].astype(BF16),
                              (((1,), (1,)), ((), ())), preferred_element_type=F32)
              for h, j in pairs]
    probs = [_softmax_rows(s, scale) for s in scores]
    outs = [jnp.dot(p.astype(BF16), vbuf[slot, h, j].astype(BF16), preferred_element_type=F32) / l
            for (h, j), (p, l) in zip(pairs, probs)]
    for h in range(XATT_HEADS):
        lo, hi = h * hd, (h + 1) * hd
        acc = jnp.zeros((rows, hd), F32)
        for j in range(bb):
            acc = jnp.where(row_batch == j, outs[h * bb + j], acc)
        o_ref[:, lo:hi] = (acc * _silu(z_ref[:, lo:hi].astype(F32))).astype(o_ref.dtype)


def attend_sample(q_arr, qb, z_arr, zb, k, v, layer, steps, bb=4):
    _, B, S, nh, hd = k.shape
    assert nh == XATT_HEADS
    DX = nh * hd
    rows = bb * steps
    slot_bytes = nh * bb * S * hd * 4
    return pl.pallas_call(
        functools.partial(_attn_sample_kernel, steps=steps, layer=layer, bb=bb),
        grid=(B // bb,),
        in_specs=[pl.BlockSpec((rows, DX), lambda i: (i, qb)),
                  pl.BlockSpec(memory_space=pl.ANY), pl.BlockSpec(memory_space=pl.ANY),
                  pl.BlockSpec((rows, DX), lambda i: (i, zb))],
        out_specs=pl.BlockSpec((rows, DX), lambda i: (i, 0)),
        out_shape=jax.ShapeDtypeStruct((B * steps, DX), BF16),
        scratch_shapes=[pltpu.VMEM((2, nh, bb, S, hd), F32), pltpu.VMEM((2, nh, bb, S, hd), F32),
                        pltpu.SemaphoreType.DMA((2, 2, nh))],
        compiler_params=_params(("arbitrary",), 4 * slot_bytes + (12 << 20)),
        name="attend_sample",
    )(q_arr, k, v, z_arr)


def _head_output(num, inv_den, o, z, g):
    ms = jnp.mean(num * num, axis=-1, keepdims=True) * (inv_den * inv_den)
    row_scale = inv_den * lax.rsqrt(ms + NORM_EPS)
    return (num * row_scale) * (g * (_sigmoid(o) * _silu(z)))


def _mlstm_prompt_kernel(q_ref, k_ref, v_ref, o_ref, zlo_ref, zhi_ref, g_ref, gh_ref,
                         y_ref, c_out, n_out, m_out, ct_scr, n_scr, m_scr):
    c = pl.program_id(1)
    nh = MLSTM_HEADS
    L = q_ref.shape[0]
    dqk = q_ref.shape[1] // nh
    dv = v_ref.shape[1] // nh
    qscale = dqk ** -0.5

    @pl.when(c == 0)
    def _():
        ct_scr[...] = jnp.zeros_like(ct_scr)
        n_scr[...] = jnp.zeros_like(n_scr)
        m_scr[...] = jnp.zeros_like(m_scr)

    r_i = lax.broadcasted_iota(jnp.int32, (L, L), 0)
    c_i = lax.broadcasted_iota(jnp.int32, (L, L), 1)
    causal = c_i <= r_i
    gates = g_ref[...]

    tril16 = jnp.where(causal, 1.0, 0.0).astype(BF16)
    g_hi = gates.astype(BF16)
    rest = gates - g_hi.astype(F32)
    g_mid = rest.astype(BF16)
    g_lo = (rest - g_mid.astype(F32)).astype(BF16)
    cums = (jnp.dot(tril16, g_hi, preferred_element_type=F32)
            + jnp.dot(tril16, g_mid, preferred_element_type=F32)
            + jnp.dot(tril16, g_lo, preferred_element_type=F32))
    gates_t = gates.T
    cums_t = cums.T

    def decay_weights(h):
        m_prev = m_scr[h:h + 1, 0:1]
        b_col = cums[:, nh + h:nh + h + 1]
        rel_row = gates_t[h:h + 1, :] - cums_t[nh + h:nh + h + 1, :]
        b2 = b_col * LOG2E
        d2 = jnp.where(causal, b2 + rel_row * LOG2E, -jnp.inf)
        inter2 = b2 + m_prev * LOG2E
        m2 = jnp.maximum(inter2, jnp.max(d2, axis=1, keepdims=True))
        return dict(m_prev=m_prev, b_col=b_col, m2=m2, m_t=m2 * LN2,
                    w=jnp.exp2(d2 - m2), w_inter=jnp.exp2(inter2 - m2))

    def read_out(h, st):
        q = q_ref[:, h * dqk:(h + 1) * dqk].astype(F32) * qscale
        k = k_ref[:, h * dqk:(h + 1) * dqk]
        v = v_ref[:, h * dv:(h + 1) * dv]
        s = lax.dot_general(q.astype(BF16), k, (((1,), (1,)), ((), ())), preferred_element_type=F32) * st["w"]
        qw = q * st["w_inter"]
        num = (jnp.dot(qw.astype(BF16), ct_scr[h].astype(BF16), preferred_element_type=F32)
               + jnp.dot(s.astype(BF16), v, preferred_element_type=F32))
        den = (jnp.sum(s, axis=1, keepdims=True)
               + jnp.sum(qw * n_scr[h:h + 1, :], axis=1, keepdims=True))
        return num, 1.0 / jnp.maximum(jnp.abs(den), jnp.exp2(-st["m2"]))

    def update_state(h, st):
        k = k_ref[:, h * dqk:(h + 1) * dqk]
        v = v_ref[:, h * dv:(h + 1) * dv]
        b_col, m_prev = st["b_col"], st["m_prev"]
        b_last = b_col[L - 1:L, :]
        m_new = st["m_t"][L - 1:L, :]
        w_state = jnp.exp(b_last - b_col + gates[:, h:h + 1] - m_new)
        decay = jnp.exp(b_last + m_prev - m_new)
        kw = k.astype(F32) * w_state
        ct_scr[h] = decay * ct_scr[h] + lax.dot_general(kw.astype(BF16), v, (((0,), (0,)), ((), ())),
                                                        preferred_element_type=F32)
        n_scr[h:h + 1, :] = decay * n_scr[h:h + 1, :] + jnp.sum(kw, axis=0, keepdims=True)
        m_scr[h:h + 1, :] = jnp.broadcast_to(m_new, (1, m_scr.shape[1]))

    def write_out(h, num, inv_den):
        lo, hi = h * dv, (h + 1) * dv
        z_ref, zh = (zlo_ref, h) if h < nh // 2 else (zhi_ref, h - nh // 2)
        y_ref[:, lo:hi] = _head_output(num, inv_den, o_ref[:, lo:hi].astype(F32),
                                       z_ref[:, zh * dv:(zh + 1) * dv].astype(F32),
                                       gh_ref[:, lo:hi]).astype(y_ref.dtype)

    for h0 in range(0, nh, MLSTM_HEAD_GROUP):
        group = range(h0, h0 + MLSTM_HEAD_GROUP)
        stats = {h: decay_weights(h) for h in group}
        outs = {h: read_out(h, stats[h]) for h in group}
        for h in group:
            update_state(h, stats[h])
        for h in group:
            write_out(h, *outs[h])

    @pl.when(c == pl.num_programs(1) - 1)
    def _():
        for h in range(nh):
            c_out[h] = ct_scr[h].T
        n_out[...] = n_scr[...]
        m_out[...] = m_scr[...]


def mlstm_prompt(proj, proj_z, gates, g_head, batch, dqk, dv):
    nh = MLSTM_HEADS
    T = proj.shape[0] // batch
    L = min(MLSTM_CHUNK, T)
    nc = T // L
    wq, wv = nh * dqk, nh * dv
    assert wv == 2 * wq
    rows = lambda b, c: b * nc + c
    return pl.pallas_call(
        _mlstm_prompt_kernel,
        grid=(batch, nc),
        in_specs=[pl.BlockSpec((L, wq), lambda b, c: (rows(b, c), 0)),
                  pl.BlockSpec((L, wq), lambda b, c: (rows(b, c), 1)),
                  pl.BlockSpec((L, wv), lambda b, c: (rows(b, c), 1)),
                  pl.BlockSpec((L, wv), lambda b, c: (rows(b, c), 2)),
                  pl.BlockSpec((L, wq), lambda b, c: (rows(b, c), 1)),
                  pl.BlockSpec((L, wq), lambda b, c: (rows(b, c), 2)),
                  pl.BlockSpec((L, GATE_LANES), lambda b, c: (rows(b, c), 0)),
                  pl.BlockSpec((1, wv), lambda b, c: (0, 0))],
        out_specs=[pl.BlockSpec((L, wv), lambda b, c: (rows(b, c), 0)),
                   pl.BlockSpec((None, nh, dv, dqk), lambda b, c: (b, 0, 0, 0)),
                   pl.BlockSpec((None, nh, dqk), lambda b, c: (b, 0, 0)),
                   pl.BlockSpec((None, nh, GATE_LANES), lambda b, c: (b, 0, 0))],
        out_shape=[jax.ShapeDtypeStruct((batch * T, wv), BF16),
                   jax.ShapeDtypeStruct((batch, nh, dv, dqk), F32),
                   jax.ShapeDtypeStruct((batch, nh, dqk), F32),
                   jax.ShapeDtypeStruct((batch, nh, GATE_LANES), F32)],
        scratch_shapes=[pltpu.VMEM((nh, dqk, dv), F32),
                        pltpu.VMEM((nh, dqk), F32),
                        pltpu.VMEM((nh, GATE_LANES), F32)],
        compiler_params=_params(("parallel", "arbitrary"),
                                24 * L * wv + 5 * nh * dv * dqk * 4 + (16 << 20)),
        name="mlstm_prompt",
    )(proj, proj, proj, proj, proj_z, proj_z, gates, g_head.reshape(1, wv))


def _lane_pick(x, lane, idx):
    return jnp.sum(jnp.where(lane == idx, x, 0.0), axis=1, keepdims=True)


def _mlstm_sample_kernel(q_ref, k_ref, v_ref, o_ref, z_ref, g_ref, gh_ref, c_ref, n_ref,
                         y_ref, c_out, n_out, m_out, *, steps):
    head = pl.program_id(1)
    nh = MLSTM_HEADS
    R, dqk = q_ref.shape
    nb = R // steps
    qscale = dqk ** -0.5

    gates = g_ref[...]
    lane = lax.broadcasted_iota(jnp.int32, gates.shape, 1)
    ig = _lane_pick(gates, lane, head)
    lf = _lane_pick(gates, lane, nh + head)
    m_prev = _lane_pick(gates, lane, 2 * nh + head)

    r_i = lax.broadcasted_iota(jnp.int32, (R, R), 0)
    c_i = lax.broadcasted_iota(jnp.int32, (R, R), 1)
    same = (r_i // steps) == (c_i // steps)
    causal = same & (c_i <= r_i)
    eye = (c_i == r_i).astype(F32)
    lower = causal.astype(F32)
    upper = (same & (r_i <= c_i)).astype(F32)
    last_of_row_batch = (c_i == (r_i // steps) * steps + (steps - 1)).astype(F32)
    row_batch = lax.broadcasted_iota(jnp.int32, (R, 1), 0) // steps

    q = q_ref[...].astype(F32) * qscale
    qb = q.astype(BF16)
    k = k_ref[...]
    v = v_ref[...]

    lf_row = jnp.sum(lf * eye, axis=0, keepdims=True)
    ig_row = jnp.sum(ig * eye, axis=0, keepdims=True)
    b_col = jnp.sum(lower * lf_row, axis=1, keepdims=True)
    b_row = jnp.sum(upper * lf, axis=0, keepdims=True)
    d = jnp.where(causal, b_col - b_row + ig_row, -jnp.inf)
    inter = b_col + m_prev
    m_t = jnp.maximum(inter, jnp.max(d, axis=1, keepdims=True))
    w = jnp.exp(d - m_t)
    w_inter = jnp.exp(inter - m_t)
    s = lax.dot_general(qb, k, (((1,), (1,)), ((), ())), preferred_element_type=F32) * w

    qc = jnp.zeros((R, v_ref.shape[1]), F32)
    n_rows = jnp.zeros((R, dqk), F32)
    for j in range(nb):
        mine = row_batch == j
        qcj = lax.dot_general(qb, c_ref[j].astype(BF16), (((1,), (1,)), ((), ())),
                              preferred_element_type=F32)
        qc = jnp.where(mine, qcj, qc)
        n_rows = jnp.where(mine, n_ref[j:j + 1, :], n_rows)

    num = jnp.dot(s.astype(BF16), v, preferred_element_type=F32) + w_inter * qc
    den = jnp.sum(s, axis=1, keepdims=True) + w_inter * jnp.sum(q * n_rows, axis=1, keepdims=True)
    inv_den = 1.0 / jnp.maximum(jnp.abs(den), jnp.exp(-m_t))
    y_ref[...] = _head_output(num, inv_den, o_ref[...].astype(F32), z_ref[...].astype(F32),
                              gh_ref[...]).astype(y_ref.dtype)

    m_row = jnp.sum(m_t * eye, axis=0, keepdims=True)
    b_last = jnp.sum(last_of_row_batch * b_row, axis=1, keepdims=True)
    m_new = jnp.sum(last_of_row_batch * m_row, axis=1, keepdims=True)
    w_state = jnp.exp(b_last - b_col + ig - m_new)
    decay = jnp.exp(b_last + m_prev - m_new)
    vw = v.astype(F32) * w_state
    kw = k.astype(F32) * w_state
    for j in range(nb):
        mine = row_batch == j
        r0 = j * steps
        dj = decay[r0:r0 + 1, :]
        upd = lax.dot_general(jnp.where(mine, vw, 0.0).astype(BF16), k, (((0,), (0,)), ((), ())),
                              preferred_element_type=F32)
        c_out[j] = dj * c_ref[j] + upd
        n_out[j:j + 1, :] = dj * n_ref[j:j + 1, :] + jnp.sum(jnp.where(mine, kw, 0.0), axis=0, keepdims=True)
        m_out[j:j + 1, :] = jnp.broadcast_to(m_new[r0:r0 + 1, :], (1, m_out.shape[1]))


def mlstm_sample(proj, proj_z, gates, g_head, c0, n0, steps, dqk, dv):
    nh = MLSTM_HEADS
    B = c0.shape[0]
    nb = SAMPLE_BATCH_BLOCK
    R = nb * steps
    kq = nh
    kv = (2 * nh * dqk) // dv
    ko = kv + nh
    kz = (nh * dqk) // dv
    return pl.pallas_call(
        functools.partial(_mlstm_sample_kernel, steps=steps),
        grid=(B // nb, nh),
        in_specs=[pl.BlockSpec((R, dqk), lambda i, h: (i, h)),
                  pl.BlockSpec((R, dqk), lambda i, h: (i, kq + h)),
                  pl.BlockSpec((R, dv), lambda i, h: (i, kv + h)),
                  pl.BlockSpec((R, dv), lambda i, h: (i, ko + h)),
                  pl.BlockSpec((R, dv), lambda i, h: (i, kz + h)),
                  pl.BlockSpec((R, GATE_LANES), lambda i, h: (i, 0)),
                  pl.BlockSpec((1, dv), lambda i, h: (0, h)),
                  pl.BlockSpec((nb, None, dv, dqk), lambda i, h: (i, h, 0, 0)),
                  pl.BlockSpec((None, nb, dqk), lambda i, h: (h, i, 0))],
        out_specs=[pl.BlockSpec((R, dv), lambda i, h: (i, h)),
                   pl.BlockSpec((nb, None, dv, dqk), lambda i, h: (i, h, 0, 0)),
                   pl.BlockSpec((None, nb, dqk), lambda i, h: (h, i, 0)),
                   pl.BlockSpec((None, nb, GATE_LANES), lambda i, h: (h, i, 0))],
        out_shape=[jax.ShapeDtypeStruct((B * steps, nh * dv), BF16),
                   jax.ShapeDtypeStruct(c0.shape, F32),
                   jax.ShapeDtypeStruct((nh, B, dqk), F32),
                   jax.ShapeDtypeStruct((nh, B, GATE_LANES), F32)],
        compiler_params=_params(("parallel", "parallel"), 6 * nb * dv * dqk * 4 + (8 << 20)),
        name="mlstm_sample",
    )(proj, proj, proj, proj, proj_z, gates, g_head.reshape(1, nh * dv), c0, n0)


def _finish_casting(x, h_mix, h_att, w_out, layer, g_post, g_next):
    y, w16 = matmul_cast([h_mix, h_att], w_out, layer, BF16, tm=512, tn=256)
    return post_norm_residual(x, y, g_post, g_next) + (w16,)


def _finish(x, h_mix, h_att, w16, g_post, g_next):
    y = matmul([h_mix, h_att], w16[None], BF16, tm=512, tn=1024)
    return post_norm_residual(x, y, g_post, g_next)


def kernel(x_prompt, x_sample, mem_prompt, state_pool, state_mlstm_C, state_mlstm_n, state_mlstm_m,
           cache_mem_k, cache_mem_v, norm_pre, norm_post, norm_mem, w_mem_kv, w_out,
           w_in_pool, w_pool_group, pool_scale, w_in_mlstm, b_igate, b_fgate, mlstm_head_norm):
    B, T, D = x_prompt.shape
    Bs, Ts, _ = x_sample.shape
    S = mem_prompt.shape[1]
    depth = w_out.shape[0]
    nh = MLSTM_HEADS
    dx = D // 2
    dv = D // nh
    dqk = dv // 2

    xp = x_prompt.reshape(B * T, D)
    xs = x_sample.reshape(Bs * Ts, D)
    mem = mem_prompt.reshape(B * S, D)
    hp = rmsnorm_cast(xp, norm_pre[:1])[0]
    hs = rmsnorm_cast(xs, norm_pre[:1])[0]

    h_mem = rmsnorm_cast(mem, norm_mem)
    mem_k, mem_k_heads = mem_proj(h_mem, w_mem_kv, 0, B, XATT_HEADS)
    mem_v, mem_v_heads = mem_proj(h_mem, w_mem_kv, dx, B, XATT_HEADS)
    w_mlstm_t = jnp.swapaxes(w_in_mlstm, 1, 2)

    pool_p_l, pool_s_l = [], []
    cp_l, np_l, mp_l, cs_l, ns_l, ms_l = [], [], [], [], [], []
    for layer in range(depth):
        j = layer // 2
        g_next = norm_pre[layer + 1] if layer + 1 < depth else None
        if layer % 2 == 0:
            w_g = w_pool_group[j].astype(BF16)
            z_mix, qb, zb = D + dx, D // dx, (2 * D + dx) // dx
            proj, w_pool16 = matmul_cast([hs], w_in_pool, j, BF16)
            u_t = proj[:, :D].reshape(Bs, Ts, D).transpose(1, 0, 2)
            pooled, hist_t = pool_sample(state_pool[j].transpose(1, 0, 2), u_t, PAST_LEN)
            pooled = pooled.transpose(1, 0, 2).reshape(Bs * Ts, D)
            mix = group_mix_gate(pooled, w_g, pool_scale[j], proj, z_mix)
            att = attend_sample(proj, qb, proj, zb, cache_mem_k, cache_mem_v, layer, Ts)
            xs, hs, w_o16 = _finish_casting(xs, mix, att, w_out, layer, norm_post[layer], g_next)
            pool_s_l.append(hist_t.transpose(1, 0, 2))
            proj = matmul([hp], w_pool16[None], BF16)
            mix = pool_mix_prompt(proj, jnp.zeros((B, POOL_HIST, D), F32), w_g, pool_scale[j], B, 0, z_mix)
            att = attend_prompt(proj, qb, proj, zb, mem_k, mem_v, layer, S)
            xp, hp = _finish(xp, mix, att, w_o16, norm_post[layer], g_next)
            pool_p_l.append(proj.reshape(B, T, -1)[:, T - POOL_HIST:, :D].astype(F32))
        else:
            bias = jnp.pad(jnp.concatenate([b_igate[j], b_fgate[j]]), (0, GATE_LANES - 2 * nh))
            bias = bias.reshape(1, GATE_LANES)
            qb, zb = 0, (dx + D) // dx
            g0 = 3 * D
            w_gates = jnp.pad(w_mlstm_t[j, g0:g0 + 2 * nh, :], ((0, GATE_LANES - 2 * nh), (0, 0))).astype(BF16)
            proj, w_main = matmul_cast([hs], w_mlstm_t, j, BF16, 0, g0, transposed=True)
            proj_z, w_z = matmul_cast([hs], w_mlstm_t, j, BF16, g0 + 2 * nh, None, transposed=True)
            gates = mlstm_gates(hs, w_gates, bias)
            m_rows = jnp.repeat(state_mlstm_m[j], Ts, axis=0)
            gates = jnp.concatenate([gates[:, :2 * nh], m_rows,
                                     jnp.zeros((Bs * Ts, GATE_LANES - 3 * nh), F32)], axis=1)
            mix, c1, n1, m1 = mlstm_sample(proj, proj_z, gates, mlstm_head_norm[j], state_mlstm_C[j],
                                           state_mlstm_n[j].transpose(1, 0, 2), Ts, dqk, dv)
            att = attend_sample(proj_z, qb, proj_z, zb, cache_mem_k, cache_mem_v, layer, Ts)
            xs, hs, w_o16 = _finish_casting(xs, mix, att, w_out, layer, norm_post[layer], g_next)
            cs_l.append(c1); ns_l.append(n1.transpose(1, 0, 2)); ms_l.append(m1[:, :, 0].T)
            proj = matmul_nt(hp, w_main, BF16)
            proj_z = matmul_nt(hp, w_z, BF16)
            gates = mlstm_gates(hp, w_gates, bias)
            mix, c1, n1, m1 = mlstm_prompt(proj, proj_z, gates, mlstm_head_norm[j], B, dqk, dv)
            att = attend_prompt(proj_z, qb, proj_z, zb, mem_k, mem_v, layer, S)
            xp, hp = _finish(xp, mix, att, w_o16, norm_post[layer], g_next)
            cp_l.append(c1); np_l.append(n1); mp_l.append(m1[:, :, 0])

    return (xp.reshape(B, T, D), xs.reshape(Bs, Ts, D), jnp.stack(pool_p_l),
            mem_k_heads, mem_v_heads,
            jnp.stack(cp_l), jnp.stack(np_l), jnp.stack(mp_l),
            jnp.stack(pool_s_l), jnp.stack(cs_l), jnp.stack(ns_l), jnp.stack(ms_l))
```

```python
import functools

import jax
import jax.numpy as jnp
from jax import lax
from jax.experimental import pallas as pl
from jax.experimental.pallas import tpu as pltpu

F32 = jnp.float32
BF16 = jnp.bfloat16

NORM_EPS = 1e-6
GATE_SOFTCAP = 15.0
POOL_WINDOWS = (2, 4, 8, 16)
POOL_HIST = max(POOL_WINDOWS) - 1
HALO = POOL_HIST + 1
XATT_HEADS = 4
MLSTM_HEADS = 8
PAST_LEN = 16384
GATE_LANES = 128
MLSTM_CHUNK = 256
MLSTM_HEAD_GROUP = 2
SAMPLE_BATCH_BLOCK = 16

V7X_VMEM_BYTES = 64 * 1024 * 1024
VMEM_CAP = V7X_VMEM_BYTES - 8 * 1024 * 1024


def _params(semantics, vmem_bytes):
    limit = int(min(max(vmem_bytes, 16 * 1024 * 1024), VMEM_CAP))
    return pltpu.CompilerParams(dimension_semantics=semantics, vmem_limit_bytes=limit)


LOG2E = 1.4426950408889634
LN2 = 0.6931471805599453
NEG_LOG2E = -LOG2E


def _sigmoid(x):
    return 1.0 / (1.0 + jnp.exp2(x * NEG_LOG2E))


def _silu(x):
    return x * _sigmoid(x)


def _rms(x, g):
    r = lax.rsqrt(jnp.mean(x * x, axis=-1, keepdims=True) + NORM_EPS)
    return x * r * g


def _rmsnorm_kernel(x_ref, g_ref, o_ref):
    o_ref[...] = _rms(x_ref[...], g_ref[...]).astype(o_ref.dtype)


def rmsnorm_cast(x, g, tm=512):
    M, D = x.shape
    L = g.shape[0]
    tm = min(tm, M)
    return pl.pallas_call(
        _rmsnorm_kernel,
        grid=(L, M // tm),
        in_specs=[pl.BlockSpec((tm, D), lambda l, i: (i, 0)),
                  pl.BlockSpec((None, 1, D), lambda l, i: (l, 0, 0))],
        out_specs=pl.BlockSpec((None, tm, D), lambda l, i: (l, i, 0)),
        out_shape=jax.ShapeDtypeStruct((L, M, D), BF16),
        compiler_params=_params(("parallel", "parallel"), 6 * tm * D * 4),
        name="rmsnorm_cast",
    )(x, g.reshape(L, 1, D))


def _post_kernel(*refs, with_next):
    x_ref, y_ref, g_ref = refs[:3]
    out = x_ref[...] + _rms(y_ref[...].astype(F32), g_ref[...])
    if with_next:
        gn_ref, o_ref, h_ref = refs[3:]
        h_ref[...] = _rms(out, gn_ref[...]).astype(h_ref.dtype)
    else:
        o_ref = refs[3]
    o_ref[...] = out


def post_norm_residual(x, y, g_post, g_next=None, tm=256):
    M, D = x.shape
    tm = min(tm, M)
    row = pl.BlockSpec((tm, D), lambda i: (i, 0))
    vec = pl.BlockSpec((1, D), lambda i: (0, 0))
    with_next = g_next is not None
    args = [x, y, g_post.reshape(1, D)] + ([g_next.reshape(1, D)] if with_next else [])
    out_shape = [jax.ShapeDtypeStruct((M, D), F32)] + ([jax.ShapeDtypeStruct((M, D), BF16)] if with_next else [])
    res = pl.pallas_call(
        functools.partial(_post_kernel, with_next=with_next),
        grid=(M // tm,),
        in_specs=[row, row, vec] + ([vec] if with_next else []),
        out_specs=[row] * len(out_shape),
        out_shape=out_shape,
        compiler_params=_params(("parallel",), 12 * tm * D * 4),
        name="post_norm_residual",
    )(*args)
    return (res[0], res[1]) if with_next else (res[0], None)


BF16_ROW_TILE = 16


def _mm_kernel(*refs, ksizes):
    n_a = len(ksizes)
    a_refs, w_ref, o_ref = refs[:n_a], refs[n_a], refs[n_a + 1]
    acc = None
    off = 0
    for a_ref, ks in zip(a_refs, ksizes):
        part = jnp.dot(a_ref[...], w_ref[off:off + ks, :], preferred_element_type=F32)
        acc = part if acc is None else acc + part
        off += ks
    o_ref[...] = acc.astype(o_ref.dtype)


def matmul(a_list, w, out_dtype, layer=0, tm=1024, tn=1024):
    M = a_list[0].shape[0]
    _, K, N = w.shape
    ksizes = tuple(a.shape[1] for a in a_list)
    assert sum(ksizes) == K
    tm, tn = min(tm, M), min(tn, N)
    osz = jnp.dtype(out_dtype).itemsize
    vmem = 2 * (tm * K * 2 + K * tn * 2 + tm * tn * osz) + 2 * tm * tn * 4
    return pl.pallas_call(
        functools.partial(_mm_kernel, ksizes=ksizes),
        grid=(N // tn, M // tm),
        in_specs=[pl.BlockSpec((tm, ks), lambda j, i: (i, 0)) for ks in ksizes]
        + [pl.BlockSpec((None, K, tn), lambda j, i: (layer, 0, j))],
        out_specs=pl.BlockSpec((tm, tn), lambda j, i: (i, j)),
        out_shape=jax.ShapeDtypeStruct((M, N), out_dtype),
        compiler_params=_params(("parallel", "parallel"), vmem + (4 << 20)),
        name="matmul",
    )(*a_list, w)


def _mm_nt_kernel(a_ref, wt_ref, o_ref):
    o_ref[...] = lax.dot_general(a_ref[...], wt_ref[...], (((1,), (1,)), ((), ())),
                                 preferred_element_type=F32).astype(o_ref.dtype)


def matmul_nt(a, wt, out_dtype, tm=1024, tn=1024):
    M, K = a.shape
    N, _ = wt.shape
    tm, tn = min(tm, M), min(tn, N)
    osz = jnp.dtype(out_dtype).itemsize
    vmem = 2 * (tm * K * 2 + K * tn * 2 + tm * tn * osz) + 2 * tm * tn * 4
    return pl.pallas_call(
        _mm_nt_kernel,
        grid=(N // tn, M // tm),
        in_specs=[pl.BlockSpec((tm, K), lambda j, i: (i, 0)),
                  pl.BlockSpec((tn, K), lambda j, i: (j, 0))],
        out_specs=pl.BlockSpec((tm, tn), lambda j, i: (i, j)),
        out_shape=jax.ShapeDtypeStruct((M, N), out_dtype),
        compiler_params=_params(("parallel", "parallel"), vmem + (4 << 20)),
        name="matmul_nt",
    )(a, wt)


def _mm_castw_kernel(*refs, ksizes, shift, transposed):
    n_a = len(ksizes)
    a_refs = refs[:n_a]
    if shift:
        w_ref, e_ref, o_ref, wbf_ref = refs[n_a:]
    else:
        w_ref, o_ref, wbf_ref = refs[n_a:]

    @pl.when(pl.program_id(1) == 0)
    def _():
        if shift:
            keep = w_ref.shape[0] - shift
            wbf_ref[0:keep, :] = w_ref[shift:, :].astype(wbf_ref.dtype)
            wbf_ref[keep:, :] = e_ref[...].astype(wbf_ref.dtype)
        else:
            wbf_ref[...] = w_ref[...].astype(wbf_ref.dtype)

    if transposed:
        acc = lax.dot_general(a_refs[0][...], wbf_ref[...], (((1,), (1,)), ((), ())), preferred_element_type=F32)
    else:
        acc, off = None, 0
        for a_ref, ks in zip(a_refs, ksizes):
            part = jnp.dot(a_ref[...], wbf_ref[off:off + ks, :], preferred_element_type=F32)
            acc = part if acc is None else acc + part
            off += ks
    o_ref[...] = acc.astype(o_ref.dtype)


def matmul_cast(a_list, w, layer, out_dtype, n0=0, n=None, transposed=False, tm=1024, tn=512):
    M = a_list[0].shape[0]
    ksizes = tuple(a.shape[1] for a in a_list)
    K = sum(ksizes)
    assert not transposed or len(a_list) == 1
    n_all = w.shape[1] if transposed else w.shape[2]
    n = n_all - n0 if n is None else n
    tm, tn = min(tm, M), min(tn, n)
    shift = n0 % tn
    base = n0 - shift
    assert n % tn == 0 and M % tm == 0
    osz = jnp.dtype(out_dtype).itemsize
    if transposed:
        w_specs = [pl.BlockSpec((None, tn, K), lambda j, i: (layer, base // tn + j, 0))]
        if shift:
            assert shift % BF16_ROW_TILE == 0 and tn % shift == 0 and base % shift == 0
            w_specs.append(pl.BlockSpec((None, shift, K), lambda j, i: (layer, (base + (j + 1) * tn) // shift, 0)))
        wbf_spec = pl.BlockSpec((tn, K), lambda j, i: (j, 0))
        wbf_shape = (n, K)
    else:
        assert shift == 0
        w_specs = [pl.BlockSpec((None, K, tn), lambda j, i: (layer, 0, base // tn + j))]
        wbf_spec = pl.BlockSpec((K, tn), lambda j, i: (0, j))
        wbf_shape = (K, n)
    vmem = 2 * (tm * K * 2 + K * tn * 4 + K * tn * 2 + tm * tn * osz) + 2 * tm * tn * 4 + 2 * K * tn * 2
    return pl.pallas_call(
        functools.partial(_mm_castw_kernel, ksizes=ksizes, shift=shift, transposed=transposed),
        grid=(n // tn, M // tm),
        in_specs=[pl.BlockSpec((tm, ks), lambda j, i: (i, 0)) for ks in ksizes] + w_specs,
        out_specs=[pl.BlockSpec((tm, tn), lambda j, i: (i, j)), wbf_spec],
        out_shape=[jax.ShapeDtypeStruct((M, n), out_dtype), jax.ShapeDtypeStruct(wbf_shape, BF16)],
        compiler_params=_params(("parallel", "arbitrary"), vmem + (2 << 20)),
        name="matmul_cast",
    )(*a_list, *([w] * len(w_specs)))


def _mem_proj_kernel(a_ref, w_ref, o_ref, heads_hbm, wbf_scr, stage, sem, *, nheads, nb):
    jj, i = pl.program_id(0), pl.program_id(1)
    step = jj * pl.num_programs(1) + i
    last = pl.num_programs(0) * pl.num_programs(1) - 1
    slot = step % 2

    def head_copy(s):
        return pltpu.make_async_copy(stage.at[s], heads_hbm.at[jj // nheads, pl.ds(i * nb, nb), :, jj % nheads, :],
                                     sem.at[s])

    @pl.when(i == 0)
    def _():
        wbf_scr[...] = w_ref[...].astype(wbf_scr.dtype)

    kv = jnp.dot(a_ref[...], wbf_scr[...], preferred_element_type=F32)
    o_ref[...] = kv

    @pl.when(step >= 2)
    def _():
        head_copy(slot).wait()

    stage[slot] = kv.reshape(stage.shape[1:])
    head_copy(slot).start()

    @pl.when(step == last)
    def _():
        head_copy(slot).wait()

        @pl.when(step >= 1)
        def _():
            head_copy(1 - slot).wait()


def mem_proj(h, w_kv, n0, batch, nheads, tm=1024):
    depth, M, D = h.shape
    S = M // batch
    hd = (w_kv.shape[2] // 2) // nheads
    n = nheads * hd
    tm = min(tm, M)
    assert n0 % hd == 0 and M % tm == 0 and tm % S == 0
    nb = tm // S
    vmem = 2 * (tm * D * 2 + D * hd * 4 + tm * hd * 4) + 4 * tm * hd * 4 + 3 * D * hd * 2
    return pl.pallas_call(
        functools.partial(_mem_proj_kernel, nheads=nheads, nb=nb),
        grid=(depth * nheads, M // tm),
        in_specs=[pl.BlockSpec((None, tm, D), lambda jj, i: (jj // nheads, i, 0)),
                  pl.BlockSpec((None, D, hd), lambda jj, i: (jj // nheads, 0, n0 // hd + jj % nheads))],
        out_specs=[pl.BlockSpec((None, tm, hd), lambda jj, i: (jj // nheads, i, jj % nheads)),
                   pl.BlockSpec(memory_space=pl.ANY)],
        out_shape=[jax.ShapeDtypeStruct((depth, M, n), F32),
                   jax.ShapeDtypeStruct((depth, batch, S, nheads, hd), F32)],
        scratch_shapes=[pltpu.VMEM((D, hd), BF16), pltpu.VMEM((2, nb, S, hd), F32),
                        pltpu.SemaphoreType.DMA((2,))],
        compiler_params=_params(("arbitrary", "arbitrary"), vmem + (2 << 20)),
        name="mem_proj",
    )(h, w_kv)


def _gates_kernel(h_ref, w_ref, b_ref, o_ref):
    nh = MLSTM_HEADS
    pre = lax.dot_general(h_ref[...], w_ref[...], (((1,), (1,)), ((), ())),
                          preferred_element_type=F32) + b_ref[...]
    capped = GATE_SOFTCAP * jnp.tanh(pre / GATE_SOFTCAP)
    log_sig = jnp.minimum(capped, 0.0) - jnp.log(1.0 + jnp.exp(-jnp.abs(capped)))
    lane = lax.broadcasted_iota(jnp.int32, capped.shape, 1)
    o_ref[...] = jnp.where(lane < nh, capped, log_sig)


def mlstm_gates(h, w_gates, bias, tm=512):
    M, D = h.shape
    tm = min(tm, M)
    return pl.pallas_call(
        _gates_kernel,
        grid=(M // tm,),
        in_specs=[pl.BlockSpec((tm, D), lambda i: (i, 0)),
                  pl.BlockSpec((GATE_LANES, D), lambda i: (0, 0)),
                  pl.BlockSpec((1, GATE_LANES), lambda i: (0, 0))],
        out_specs=pl.BlockSpec((tm, GATE_LANES), lambda i: (i, 0)),
        out_shape=jax.ShapeDtypeStruct((M, GATE_LANES), F32),
        compiler_params=_params(("parallel",), 4 * tm * D * 2 + 4 * D * GATE_LANES * 2),
        name="mlstm_gates",
    )(h, w_gates, bias)


def _group_mm_kernel(a_ref, w_ref, s_ref, z_ref, o_ref):
    mixed = jnp.dot(a_ref[...], w_ref[...], preferred_element_type=F32) * s_ref[...]
    o_ref[...] = (mixed * _silu(z_ref[...].astype(F32))).astype(o_ref.dtype)


def group_mix_gate(pooled, w_group, scale, proj, z_off, tm=512):
    M, D = pooled.shape
    G, Gs, _ = w_group.shape
    tm = min(tm, M)
    zb = z_off // Gs
    return pl.pallas_call(
        _group_mm_kernel,
        grid=(G, M // tm),
        in_specs=[pl.BlockSpec((tm, Gs), lambda g, i: (i, g)),
                  pl.BlockSpec((None, Gs, Gs), lambda g, i: (g, 0, 0)),
                  pl.BlockSpec((1, Gs), lambda g, i: (0, g)),
                  pl.BlockSpec((tm, Gs), lambda g, i: (i, zb + g))],
        out_specs=pl.BlockSpec((tm, Gs), lambda g, i: (i, g)),
        out_shape=jax.ShapeDtypeStruct((M, D), BF16),
        compiler_params=_params(("parallel", "parallel"), 6 * tm * Gs * 4 + 4 * Gs * Gs * 2),
        name="group_mix_gate",
    )(pooled, w_group, scale.reshape(1, D), proj)


def _window_sums(ext, levels):
    sums = []
    cur = ext
    for lv in range(levels):
        cur = cur + pltpu.roll(cur, 1 << lv, 0)
        sums.append(cur)
    return sums


def _pool_mix_prompt_kernel(u_ref, halo_ref, hist_ref, zlo_ref, zhi_ref, w_ref, s_ref, o_ref, *, pos0, tt):
    i = pl.program_id(1)
    ng = len(POOL_WINDOWS)
    gs = u_ref.shape[1] // ng
    row = lax.broadcasted_iota(jnp.int32, (tt, 1), 0) + i * tt
    pos1 = (row + (pos0 + 1)).astype(F32)
    first = i == 0
    for g, w in enumerate(POOL_WINDOWS):
        lo, hi = g * gs, (g + 1) * gs
        u = u_ref[:, lo:hi].astype(F32)
        halo = jnp.where(first, hist_ref[:, lo:hi], halo_ref[:, lo:hi].astype(F32))
        ext = jnp.concatenate([halo, u], axis=0)
        win = _window_sums(ext, g + 1)[-1][HALO:, :]
        inv_cnt = 1.0 / jnp.minimum(float(w), pos1)
        pooled = (win * inv_cnt - u).astype(BF16)
        mixed = jnp.dot(pooled, w_ref[g], preferred_element_type=F32) * s_ref[:, lo:hi]
        z_ref, zg = (zlo_ref, g) if g < ng // 2 else (zhi_ref, g - ng // 2)
        z = z_ref[:, zg * gs:(zg + 1) * gs].astype(F32)
        o_ref[:, lo:hi] = (mixed * _silu(z)).astype(o_ref.dtype)


def pool_mix_prompt(proj, hist, w_group, scale, batch, pos0, z_off, tt=512):
    D = hist.shape[-1]
    G, Gs, _ = w_group.shape
    T = proj.shape[0] // batch
    tt = min(tt, T)
    nt = T // tt
    dh = D // 2
    assert z_off % dh == 0 and G == len(POOL_WINDOWS)
    zb = z_off // dh
    hist16 = jnp.concatenate([jnp.zeros((batch, 1, D), F32), hist.astype(F32)], axis=1)
    vmem = 2 * (3 * tt * D * 2 + G * Gs * Gs * 2) + 12 * (tt + HALO) * Gs * 4
    return pl.pallas_call(
        functools.partial(_pool_mix_prompt_kernel, pos0=pos0, tt=tt),
        grid=(batch, nt),
        in_specs=[pl.BlockSpec((tt, D), lambda b, i: (b * nt + i, 0)),
                  pl.BlockSpec((HALO, D),
                               lambda b, i: (jnp.maximum((b * nt + i) * (tt // HALO) - 1, 0), 0)),
                  pl.BlockSpec((None, HALO, D), lambda b, i: (b, 0, 0)),
                  pl.BlockSpec((tt, dh), lambda b, i: (b * nt + i, zb)),
                  pl.BlockSpec((tt, dh), lambda b, i: (b * nt + i, zb + 1)),
                  pl.BlockSpec((G, Gs, Gs), lambda b, i: (0, 0, 0)),
                  pl.BlockSpec((1, D), lambda b, i: (0, 0))],
        out_specs=pl.BlockSpec((tt, D), lambda b, i: (b * nt + i, 0)),
        out_shape=jax.ShapeDtypeStruct((batch * T, D), BF16),
        compiler_params=_params(("parallel", "parallel"), vmem + (4 << 20)),
        name="pool_mix_prompt",
    )(proj, proj, hist16, proj, proj, w_group, scale.reshape(1, D))


def _pool_sample_kernel(hist_ref, u_ref, o_ref, new_hist_ref, *, pos0):
    steps = u_ref.shape[0]
    gs = u_ref.shape[2] // len(POOL_WINDOWS)

    def row(r, lo, hi):
        if r < POOL_HIST:
            return hist_ref[r, :, lo:hi]
        return u_ref[r - POOL_HIST, :, lo:hi].astype(F32)

    for g, w in enumerate(POOL_WINDOWS):
        lo, hi = g * gs, (g + 1) * gs
        for t in range(steps):
            cur = row(POOL_HIST + t, lo, hi)
            win = cur
            for r in range(1, w):
                win = win + row(POOL_HIST + t - r, lo, hi)
            cnt = float(min(w, pos0 + t + 1))
            o_ref[t, :, lo:hi] = (win / cnt - cur).astype(o_ref.dtype)
    for r in range(POOL_HIST):
        new_hist_ref[r] = row(r + steps, 0, u_ref.shape[2])


def pool_sample(hist_t, u_t, pos0, bt=32):
    T, B, D = u_t.shape
    bt = min(bt, B)
    return pl.pallas_call(
        functools.partial(_pool_sample_kernel, pos0=pos0),
        grid=(B // bt,),
        in_specs=[pl.BlockSpec((POOL_HIST, bt, D), lambda i: (0, i, 0)),
                  pl.BlockSpec((T, bt, D), lambda i: (0, i, 0))],
        out_specs=[pl.BlockSpec((T, bt, D), lambda i: (0, i, 0)),
                   pl.BlockSpec((POOL_HIST, bt, D), lambda i: (0, i, 0))],
        out_shape=[jax.ShapeDtypeStruct((T, B, D), BF16), jax.ShapeDtypeStruct((POOL_HIST, B, D), F32)],
        compiler_params=_params(("parallel",), 5 * (POOL_HIST + T) * bt * D * 4),
        name="pool_sample",
    )(hist_t, u_t)


def _softmax_rows(qk, scale):
    m = jnp.max(qk, axis=-1, keepdims=True)
    p = jnp.exp2((qk - m) * (scale * LOG2E))
    return p, jnp.sum(p, axis=-1, keepdims=True)


def _attn_prompt_kernel(q_ref, k_ref, v_ref, z_ref, o_ref):
    hd = q_ref.shape[1] // XATT_HEADS
    scale = hd ** -0.5
    cols = [(h * hd, (h + 1) * hd) for h in range(XATT_HEADS)]
    scores = [lax.dot_general(q_ref[:, lo:hi], k_ref[:, lo:hi].astype(BF16), (((1,), (1,)), ((), ())),
                              preferred_element_type=F32) for lo, hi in cols]
    probs = [_softmax_rows(s, scale) for s in scores]
    outs = [jnp.dot(p.astype(BF16), v_ref[:, lo:hi].astype(BF16), preferred_element_type=F32) / l
            for (lo, hi), (p, l) in zip(cols, probs)]
    for (lo, hi), o in zip(cols, outs):
        o_ref[:, lo:hi] = (o * _silu(z_ref[:, lo:hi].astype(F32))).astype(o_ref.dtype)


def attend_prompt(q_arr, qb, z_arr, zb, k, v, layer, S, tq=512):
    _, BS, DX = k.shape
    B = BS // S
    T = q_arr.shape[0] // B
    tq = min(tq, T)
    nq = T // tq
    return pl.pallas_call(
        _attn_prompt_kernel,
        grid=(B, nq),
        in_specs=[pl.BlockSpec((tq, DX), lambda b, i: (b * nq + i, qb)),
                  pl.BlockSpec((None, S, DX), lambda b, i: (layer, b, 0)),
                  pl.BlockSpec((None, S, DX), lambda b, i: (layer, b, 0)),
                  pl.BlockSpec((tq, DX), lambda b, i: (b * nq + i, zb))],
        out_specs=pl.BlockSpec((tq, DX), lambda b, i: (b * nq + i, 0)),
        out_shape=jax.ShapeDtypeStruct((B * T, DX), BF16),
        compiler_params=_params(("parallel", "parallel"), 8 * tq * DX * 4 + 8 * S * DX * 4),
        name="attend_prompt",
    )(q_arr, k, v, z_arr)


def _attn_sample_kernel(q_ref, k_hbm, v_hbm, z_ref, o_ref, kbuf, vbuf, sem, *, steps, layer, bb):
    i = pl.program_id(0)
    rows, dx = q_ref.shape
    hd = dx // XATT_HEADS
    scale = hd ** -0.5

    def head_copies(step, slot):
        out = []
        for t, (src, dst) in enumerate(((k_hbm, kbuf), (v_hbm, vbuf))):
            for h in range(XATT_HEADS):
                out.append(pltpu.make_async_copy(src.at[layer, pl.ds(step * bb, bb), :, h, :],
                                                 dst.at[slot, h], sem.at[t, slot, h]))
        return out

    @pl.when(i == 0)
    def _():
        for cp in head_copies(0, 0):
            cp.start()

    @pl.when(i + 1 < pl.num_programs(0))
    def _():
        for cp in head_copies(i + 1, (i + 1) % 2):
            cp.start()

    slot = i % 2
    for cp in head_copies(i, slot):
        cp.wait()

    row_batch = lax.broadcasted_iota(jnp.int32, (rows, 1), 0) // steps
    pairs = [(h, j) for h in range(XATT_HEADS) for j in range(bb)]
    scores = [lax.dot_general(q_ref[:, h * hd:(h + 1) * hd], kbuf[slot, h, j].astype(BF16),
                              (((1,), (1,)), ((), ())), preferred_element_type=F32)
              for h, j in pairs]
    probs = [_softmax_rows(s, scale) for s in scores]
    outs = [jnp.dot(p.astype(BF16), vbuf[slot, h, j].astype(BF16), preferred_element_type=F32) / l
            for (h, j), (p, l) in zip(pairs, probs)]
    for h in range(XATT_HEADS):
        lo, hi = h * hd, (h + 1) * hd
        acc = jnp.zeros((rows, hd), F32)
        for j in range(bb):
            acc = jnp.where(row_batch == j, outs[h * bb + j], acc)
        o_ref[:, lo:hi] = (acc * _silu(z_ref[:, lo:hi].astype(F32))).astype(o_ref.dtype)


def attend_sample(q_arr, qb, z_arr, zb, k, v, layer, steps, bb=4):
    _, B, S, nh, hd = k.shape
    assert nh == XATT_HEADS
    DX = nh * hd
    rows = bb * steps
    slot_bytes = nh * bb * S * hd * 4
    return pl.pallas_call(
        functools.partial(_attn_sample_kernel, steps=steps, layer=layer, bb=bb),
        grid=(B // bb,),
        in_specs=[pl.BlockSpec((rows, DX), lambda i: (i, qb)),
                  pl.BlockSpec(memory_space=pl.ANY), pl.BlockSpec(memory_space=pl.ANY),
                  pl.BlockSpec((rows, DX), lambda i: (i, zb))],
        out_specs=pl.BlockSpec((rows, DX), lambda i: (i, 0)),
        out_shape=jax.ShapeDtypeStruct((B * steps, DX), BF16),
        scratch_shapes=[pltpu.VMEM((2, nh, bb, S, hd), F32), pltpu.VMEM((2, nh, bb, S, hd), F32),
                        pltpu.SemaphoreType.DMA((2, 2, nh))],
        compiler_params=_params(("arbitrary",), 4 * slot_bytes + (12 << 20)),
        name="attend_sample",
    )(q_arr, k, v, z_arr)


def _head_output(num, inv_den, o, z, g):
    ms = jnp.mean(num * num, axis=-1, keepdims=True) * (inv_den * inv_den)
    row_scale = inv_den * lax.rsqrt(ms + NORM_EPS)
    return (num * row_scale) * (g * (_sigmoid(o) * _silu(z)))


def _mlstm_prompt_kernel(q_ref, k_ref, v_ref, o_ref, zlo_ref, zhi_ref, g_ref, gh_ref,
                         y_ref, c_out, n_out, m_out, ct_scr, n_scr, m_scr):
    c = pl.program_id(1)
    nh = MLSTM_HEADS
    L = q_ref.shape[0]
    dqk = q_ref.shape[1] // nh
    dv = v_ref.shape[1] // nh
    qscale = dqk ** -0.5

    @pl.when(c == 0)
    def _():
        ct_scr[...] = jnp.zeros_like(ct_scr)
        n_scr[...] = jnp.zeros_like(n_scr)
        m_scr[...] = jnp.zeros_like(m_scr)

    r_i = lax.broadcasted_iota(jnp.int32, (L, L), 0)
    c_i = lax.broadcasted_iota(jnp.int32, (L, L), 1)
    causal = c_i <= r_i
    gates = g_ref[...]

    tril16 = jnp.where(causal, 1.0, 0.0).astype(BF16)
    g_hi = gates.astype(BF16)
    rest = gates - g_hi.astype(F32)
    g_mid = rest.astype(BF16)
    g_lo = (rest - g_mid.astype(F32)).astype(BF16)
    cums = (jnp.dot(tril16, g_hi, preferred_element_type=F32)
            + jnp.dot(tril16, g_mid, preferred_element_type=F32)
            + jnp.dot(tril16, g_lo, preferred_element_type=F32))
    gates_t = gates.T
    cums_t = cums.T

    def decay_weights(h):
        m_prev = m_scr[h:h + 1, 0:1]
        b_col = cums[:, nh + h:nh + h + 1]
        rel_row = gates_t[h:h + 1, :] - cums_t[nh + h:nh + h + 1, :]
        b2 = b_col * LOG2E
        d2 = jnp.where(causal, b2 + rel_row * LOG2E, -jnp.inf)
        inter2 = b2 + m_prev * LOG2E
        m2 = jnp.maximum(inter2, jnp.max(d2, axis=1, keepdims=True))
        return dict(m_prev=m_prev, b_col=b_col, m2=m2, m_t=m2 * LN2,
                    w=jnp.exp2(d2 - m2), w_inter=jnp.exp2(inter2 - m2))

    def read_out(h, st):
        q = q_ref[:, h * dqk:(h + 1) * dqk].astype(F32) * qscale
        k = k_ref[:, h * dqk:(h + 1) * dqk]
        v = v_ref[:, h * dv:(h + 1) * dv]
        s = lax.dot_general(q.astype(BF16), k, (((1,), (1,)), ((), ())), preferred_element_type=F32) * st["w"]
        qw = q * st["w_inter"]
        num = (jnp.dot(qw.astype(BF16), ct_scr[h].astype(BF16), preferred_element_type=F32)
               + jnp.dot(s.astype(BF16), v, preferred_element_type=F32))
        den = (jnp.sum(s, axis=1, keepdims=True)
               + jnp.sum(qw * n_scr[h:h + 1, :], axis=1, keepdims=True))
        return num, 1.0 / jnp.maximum(jnp.abs(den), jnp.exp2(-st["m2"]))

    def update_state(h, st):
        k = k_ref[:, h * dqk:(h + 1) * dqk]
        v = v_ref[:, h * dv:(h + 1) * dv]
        b_col, m_prev = st["b_col"], st["m_prev"]
        b_last = b_col[L - 1:L, :]
        m_new = st["m_t"][L - 1:L, :]
        w_state = jnp.exp(b_last - b_col + gates[:, h:h + 1] - m_new)
        decay = jnp.exp(b_last + m_prev - m_new)
        kw = k.astype(F32) * w_state
        ct_scr[h] = decay * ct_scr[h] + lax.dot_general(kw.astype(BF16), v, (((0,), (0,)), ((), ())),
                                                        preferred_element_type=F32)
        n_scr[h:h + 1, :] = decay * n_scr[h:h + 1, :] + jnp.sum(kw, axis=0, keepdims=True)
        m_scr[h:h + 1, :] = jnp.broadcast_to(m_new, (1, m_scr.shape[1]))

    def write_out(h, num, inv_den):
        lo, hi = h * dv, (h + 1) * dv
        z_ref, zh = (zlo_ref, h) if h < nh // 2 else (zhi_ref, h - nh // 2)
        y_ref[:, lo:hi] = _head_output(num, inv_den, o_ref[:, lo:hi].astype(F32),
                                       z_ref[:, zh * dv:(zh + 1) * dv].astype(F32),
                                       gh_ref[:, lo:hi]).astype(y_ref.dtype)

    for h0 in range(0, nh, MLSTM_HEAD_GROUP):
        group = range(h0, h0 + MLSTM_HEAD_GROUP)
        stats = {h: decay_weights(h) for h in group}
        outs = {h: read_out(h, stats[h]) for h in group}
        for h in group:
            update_state(h, stats[h])
        for h in group:
            write_out(h, *outs[h])

    @pl.when(c == pl.num_programs(1) - 1)
    def _():
        for h in range(nh):
            c_out[h] = ct_scr[h].T
        n_out[...] = n_scr[...]
        m_out[...] = m_scr[...]


def mlstm_prompt(proj, proj_z, gates, g_head, batch, dqk, dv):
    nh = MLSTM_HEADS
    T = proj.shape[0] // batch
    L = min(MLSTM_CHUNK, T)
    nc = T // L
    wq, wv = nh * dqk, nh * dv
    assert wv == 2 * wq
    rows = lambda b, c: b * nc + c
    return pl.pallas_call(
        _mlstm_prompt_kernel,
        grid=(batch, nc),
        in_specs=[pl.BlockSpec((L, wq), lambda b, c: (rows(b, c), 0)),
                  pl.BlockSpec((L, wq), lambda b, c: (rows(b, c), 1)),
                  pl.BlockSpec((L, wv), lambda b, c: (rows(b, c), 1)),
                  pl.BlockSpec((L, wv), lambda b, c: (rows(b, c), 2)),
                  pl.BlockSpec((L, wq), lambda b, c: (rows(b, c), 1)),
                  pl.BlockSpec((L, wq), lambda b, c: (rows(b, c), 2)),
                  pl.BlockSpec((L, GATE_LANES), lambda b, c: (rows(b, c), 0)),
                  pl.BlockSpec((1, wv), lambda b, c: (0, 0))],
        out_specs=[pl.BlockSpec((L, wv), lambda b, c: (rows(b, c), 0)),
                   pl.BlockSpec((None, nh, dv, dqk), lambda b, c: (b, 0, 0, 0)),
                   pl.BlockSpec((None, nh, dqk), lambda b, c: (b, 0, 0)),
                   pl.BlockSpec((None, nh, GATE_LANES), lambda b, c: (b, 0, 0))],
        out_shape=[jax.ShapeDtypeStruct((batch * T, wv), BF16),
                   jax.ShapeDtypeStruct((batch, nh, dv, dqk), F32),
                   jax.ShapeDtypeStruct((batch, nh, dqk), F32),
                   jax.ShapeDtypeStruct((batch, nh, GATE_LANES), F32)],
        scratch_shapes=[pltpu.VMEM((nh, dqk, dv), F32),
                        pltpu.VMEM((nh, dqk), F32),
                        pltpu.VMEM((nh, GATE_LANES), F32)],
        compiler_params=_params(("parallel", "arbitrary"),
                                24 * L * wv + 5 * nh * dv * dqk * 4 + (16 << 20)),
        name="mlstm_prompt",
    )(proj, proj, proj, proj, proj_z, proj_z, gates, g_head.reshape(1, wv))


def _lane_pick(x, lane, idx):
    return jnp.sum(jnp.where(lane == idx, x, 0.0), axis=1, keepdims=True)


def _mlstm_sample_kernel(q_ref, k_ref, v_ref, o_ref, z_ref, g_ref, gh_ref, c_ref, n_ref,
                         y_ref, c_out, n_out, m_out, *, steps):
    head = pl.program_id(1)
    nh = MLSTM_HEADS
    R, dqk = q_ref.shape
    nb = R // steps
    qscale = dqk ** -0.5

    gates = g_ref[...]
    lane = lax.broadcasted_iota(jnp.int32, gates.shape, 1)
    ig = _lane_pick(gates, lane, head)
    lf = _lane_pick(gates, lane, nh + head)
    m_prev = _lane_pick(gates, lane, 2 * nh + head)

    r_i = lax.broadcasted_iota(jnp.int32, (R, R), 0)
    c_i = lax.broadcasted_iota(jnp.int32, (R, R), 1)
    same = (r_i // steps) == (c_i // steps)
    causal = same & (c_i <= r_i)
    eye = (c_i == r_i).astype(F32)
    lower = causal.astype(F32)
    upper = (same & (r_i <= c_i)).astype(F32)
    last_of_row_batch = (c_i == (r_i // steps) * steps + (steps - 1)).astype(F32)
    row_batch = lax.broadcasted_iota(jnp.int32, (R, 1), 0) // steps

    q = q_ref[...].astype(F32) * qscale
    qb = q.astype(BF16)
    k = k_ref[...]
    v = v_ref[...]

    lf_row = jnp.sum(lf * eye, axis=0, keepdims=True)
    ig_row = jnp.sum(ig * eye, axis=0, keepdims=True)
    b_col = jnp.sum(lower * lf_row, axis=1, keepdims=True)
    b_row = jnp.sum(upper * lf, axis=0, keepdims=True)
    d = jnp.where(causal, b_col - b_row + ig_row, -jnp.inf)
    inter = b_col + m_prev
    m_t = jnp.maximum(inter, jnp.max(d, axis=1, keepdims=True))
    w = jnp.exp(d - m_t)
    w_inter = jnp.exp(inter - m_t)
    s = lax.dot_general(qb, k, (((1,), (1,)), ((), ())), preferred_element_type=F32) * w

    qc = jnp.zeros((R, v_ref.shape[1]), F32)
    n_rows = jnp.zeros((R, dqk), F32)
    for j in range(nb):
        mine = row_batch == j
        qcj = lax.dot_general(qb, c_ref[j].astype(BF16), (((1,), (1,)), ((), ())),
                              preferred_element_type=F32)
        qc = jnp.where(mine, qcj, qc)
        n_rows = jnp.where(mine, n_ref[j:j + 1, :], n_rows)

    num = jnp.dot(s.astype(BF16), v, preferred_element_type=F32) + w_inter * qc
    den = jnp.sum(s, axis=1, keepdims=True) + w_inter * jnp.sum(q * n_rows, axis=1, keepdims=True)
    inv_den = 1.0 / jnp.maximum(jnp.abs(den), jnp.exp(-m_t))
    y_ref[...] = _head_output(num, inv_den, o_ref[...].astype(F32), z_ref[...].astype(F32),
                              gh_ref[...]).astype(y_ref.dtype)

    m_row = jnp.sum(m_t * eye, axis=0, keepdims=True)
    b_last = jnp.sum(last_of_row_batch * b_row, axis=1, keepdims=True)
    m_new = jnp.sum(last_of_row_batch * m_row, axis=1, keepdims=True)
    w_state = jnp.exp(b_last - b_col + ig - m_new)
    decay = jnp.exp(b_last + m_prev - m_new)
    vw = v.astype(F32) * w_state
    kw = k.astype(F32) * w_state
    for j in range(nb):
        mine = row_batch == j
        r0 = j * steps
        dj = decay[r0:r0 + 1, :]
        upd = lax.dot_general(jnp.where(mine, vw, 0.0).astype(BF16), k, (((0,), (0,)), ((), ())),
                              preferred_element_type=F32)
        c_out[j] = dj * c_ref[j] + upd
        n_out[j:j + 1, :] = dj * n_ref[j:j + 1, :] + jnp.sum(jnp.where(mine, kw, 0.0), axis=0, keepdims=True)
        m_out[j:j + 1, :] = jnp.broadcast_to(m_new[r0:r0 + 1, :], (1, m_out.shape[1]))


def mlstm_sample(proj, proj_z, gates, g_head, c0, n0, steps, dqk, dv):
    nh = MLSTM_HEADS
    B = c0.shape[0]
    nb = SAMPLE_BATCH_BLOCK
    R = nb * steps
    kq = nh
    kv = (2 * nh * dqk) // dv
    ko = kv + nh
    kz = (nh * dqk) // dv
    return pl.pallas_call(
        functools.partial(_mlstm_sample_kernel, steps=steps),
        grid=(B // nb, nh),
        in_specs=[pl.BlockSpec((R, dqk), lambda i, h: (i, h)),
                  pl.BlockSpec((R, dqk), lambda i, h: (i, kq + h)),
                  pl.BlockSpec((R, dv), lambda i, h: (i, kv + h)),
                  pl.BlockSpec((R, dv), lambda i, h: (i, ko + h)),
                  pl.BlockSpec((R, dv), lambda i, h: (i, kz + h)),
                  pl.BlockSpec((R, GATE_LANES), lambda i, h: (i, 0)),
                  pl.BlockSpec((1, dv), lambda i, h: (0, h)),
                  pl.BlockSpec((nb, None, dv, dqk), lambda i, h: (i, h, 0, 0)),
                  pl.BlockSpec((None, nb, dqk), lambda i, h: (h, i, 0))],
        out_specs=[pl.BlockSpec((R, dv), lambda i, h: (i, h)),
                   pl.BlockSpec((nb, None, dv, dqk), lambda i, h: (i, h, 0, 0)),
                   pl.BlockSpec((None, nb, dqk), lambda i, h: (h, i, 0)),
                   pl.BlockSpec((None, nb, GATE_LANES), lambda i, h: (h, i, 0))],
        out_shape=[jax.ShapeDtypeStruct((B * steps, nh * dv), BF16),
                   jax.ShapeDtypeStruct(c0.shape, F32),
                   jax.ShapeDtypeStruct((nh, B, dqk), F32),
                   jax.ShapeDtypeStruct((nh, B, GATE_LANES), F32)],
        compiler_params=_params(("parallel", "parallel"), 6 * nb * dv * dqk * 4 + (8 << 20)),
        name="mlstm_sample",
    )(proj, proj, proj, proj, proj_z, gates, g_head.reshape(1, nh * dv), c0, n0)


def _finish_casting(x, h_mix, h_att, w_out, layer, g_post, g_next):
    y, w16 = matmul_cast([h_mix, h_att], w_out, layer, BF16, tm=512, tn=256)
    return post_norm_residual(x, y, g_post, g_next) + (w16,)


def _finish(x, h_mix, h_att, w16, g_post, g_next):
    y = matmul([h_mix, h_att], w16[None], BF16, tm=512, tn=1024)
    return post_norm_residual(x, y, g_post, g_next)


def kernel(x_prompt, x_sample, mem_prompt, state_pool, state_mlstm_C, state_mlstm_n, state_mlstm_m,
           cache_mem_k, cache_mem_v, norm_pre, norm_post, norm_mem, w_mem_kv, w_out,
           w_in_pool, w_pool_group, pool_scale, w_in_mlstm, b_igate, b_fgate, mlstm_head_norm):
    B, T, D = x_prompt.shape
    Bs, Ts, _ = x_sample.shape
    S = mem_prompt.shape[1]
    depth = w_out.shape[0]
    nh = MLSTM_HEADS
    dx = D // 2
    dv = D // nh
    dqk = dv // 2

    xp = x_prompt.reshape(B * T, D)
    xs = x_sample.reshape(Bs * Ts, D)
    mem = mem_prompt.reshape(B * S, D)
    hp = rmsnorm_cast(xp, norm_pre[:1])[0]
    hs = rmsnorm_cast(xs, norm_pre[:1])[0]

    h_mem = rmsnorm_cast(mem, norm_mem)
    mem_k, mem_k_heads = mem_proj(h_mem, w_mem_kv, 0, B, XATT_HEADS)
    mem_v, mem_v_heads = mem_proj(h_mem, w_mem_kv, dx, B, XATT_HEADS)
    w_mlstm_t = jnp.swapaxes(w_in_mlstm, 1, 2)

    pool_p_l, pool_s_l = [], []
    cp_l, np_l, mp_l, cs_l, ns_l, ms_l = [], [], [], [], [], []
    for layer in range(depth):
        j = layer // 2
        g_next = norm_pre[layer + 1] if layer + 1 < depth else None
        if layer % 2 == 0:
            w_g = w_pool_group[j].astype(BF16)
            z_mix, qb, zb = D + dx, D // dx, (2 * D + dx) // dx
            proj, w_pool16 = matmul_cast([hs], w_in_pool, j, BF16)
            u_t = proj[:, :D].reshape(Bs, Ts, D).transpose(1, 0, 2)
            pooled, hist_t = pool_sample(state_pool[j].transpose(1, 0, 2), u_t, PAST_LEN)
            pooled = pooled.transpose(1, 0, 2).reshape(Bs * Ts, D)
            mix = group_mix_gate(pooled, w_g, pool_scale[j], proj, z_mix)
            att = attend_sample(proj, qb, proj, zb, cache_mem_k, cache_mem_v, layer, Ts)
            xs, hs, w_o16 = _finish_casting(xs, mix, att, w_out, layer, norm_post[layer], g_next)
            pool_s_l.append(hist_t.transpose(1, 0, 2))
            proj = matmul([hp], w_pool16[None], BF16)
            mix = pool_mix_prompt(proj, jnp.zeros((B, POOL_HIST, D), F32), w_g, pool_scale[j], B, 0, z_mix)
            att = attend_prompt(proj, qb, proj, zb, mem_k, mem_v, layer, S)
            xp, hp = _finish(xp, mix, att, w_o16, norm_post[layer], g_next)
            pool_p_l.append(proj.reshape(B, T, -1)[:, T - POOL_HIST:, :D].astype(F32))
        else:
            bias = jnp.pad(jnp.concatenate([b_igate[j], b_fgate[j]]), (0, GATE_LANES - 2 * nh))
            bias = bias.reshape(1, GATE_LANES)
            qb, zb = 0, (dx + D) // dx
            g0 = 3 * D
            w_gates = jnp.pad(w_mlstm_t[j, g0:g0 + 2 * nh, :], ((0, GATE_LANES - 2 * nh), (0, 0))).astype(BF16)
            proj, w_main = matmul_cast([hs], w_mlstm_t, j, BF16, 0, g0, transposed=True)
            proj_z, w_z = matmul_cast([hs], w_mlstm_t, j, BF16, g0 + 2 * nh, None, transposed=True)
            gates = mlstm_gates(hs, w_gates, bias)
            m_rows = jnp.repeat(state_mlstm_m[j], Ts, axis=0)
            gates = jnp.concatenate([gates[:, :2 * nh], m_rows,
                                     jnp.zeros((Bs * Ts, GATE_LANES - 3 * nh), F32)], axis=1)
            mix, c1, n1, m1 = mlstm_sample(proj, proj_z, gates, mlstm_head_norm[j], state_mlstm_C[j],
                                           state_mlstm_n[j].transpose(1, 0, 2), Ts, dqk, dv)
            att = attend_sample(proj_z, qb, proj_z, zb, cache_mem_k, cache_mem_v, layer, Ts)
            xs, hs, w_o16 = _finish_casting(xs, mix, att, w_out, layer, norm_post[layer], g_next)
            cs_l.append(c1); ns_l.append(n1.transpose(1, 0, 2)); ms_l.append(m1[:, :, 0].T)
            proj = matmul_nt(hp, w_main, BF16)
            proj_z = matmul_nt(hp, w_z, BF16)
            gates = mlstm_gates(hp, w_gates, bias)
            mix, c1, n1, m1 = mlstm_prompt(proj, proj_z, gates, mlstm_head_norm[j], B, dqk, dv)
            att = attend_prompt(proj_z, qb, proj_z, zb, mem_k, mem_v, layer, S)
            xp, hp = _finish(xp, mix, att, w_o16, norm_post[layer], g_next)
            cp_l.append(c1); np_l.append(n1); mp_l.append(m1[:, :, 0])

    return (xp.reshape(B, T, D), xs.reshape(Bs, Ts, D), jnp.stack(pool_p_l),
            mem_k_heads, mem_v_heads,
            jnp.stack(cp_l), jnp.stack(np_l), jnp.stack(mp_l),
            jnp.stack(pool_s_l), jnp.stack(cs_l), jnp.stack(ns_l), jnp.stack(ms_l))
```

```python
import functools

import jax
import jax.numpy as jnp
from jax import lax
from jax.experimental import pallas as pl
from jax.experimental.pallas import tpu as pltpu

F32 = jnp.float32
BF16 = jnp.bfloat16

NORM_EPS = 1e-6
GATE_SOFTCAP = 15.0
POOL_WINDOWS = (2, 4, 8, 16)
POOL_HIST = max(POOL_WINDOWS) - 1
HALO = POOL_HIST + 1
XATT_HEADS = 4
MLSTM_HEADS = 8
PAST_LEN = 16384
GATE_LANES = 128
MLSTM_CHUNK = 256
MLSTM_HEAD_GROUP = 2
SAMPLE_BATCH_BLOCK = 16

V7X_VMEM_BYTES = 64 * 1024 * 1024
VMEM_CAP = V7X_VMEM_BYTES - 8 * 1024 * 1024


def _params(semantics, vmem_bytes):
    limit = int(min(max(vmem_bytes, 16 * 1024 * 1024), VMEM_CAP))
    return pltpu.CompilerParams(dimension_semantics=semantics, vmem_limit_bytes=limit)


LOG2E = 1.4426950408889634
LN2 = 0.6931471805599453
NEG_LOG2E = -LOG2E


def _sigmoid(x):
    return 1.0 / (1.0 + jnp.exp2(x * NEG_LOG2E))


def _silu(x):
    return x * _sigmoid(x)


def _rms(x, g):
    r = lax.rsqrt(jnp.mean(x * x, axis=-1, keepdims=True) + NORM_EPS)
    return x * r * g


def _rmsnorm_kernel(x_ref, g_ref, o_ref):
    o_ref[...] = _rms(x_ref[...], g_ref[...]).astype(o_ref.dtype)


ROWWISE_MIN_STEPS = 8


def rmsnorm_cast(x, g, tm=512):
    M, D = x.shape
    L = g.shape[0]
    tm = max(min(tm, M // ROWWISE_MIN_STEPS), BF16_ROW_TILE)
    return pl.pallas_call(
        _rmsnorm_kernel,
        grid=(L, M // tm),
        in_specs=[pl.BlockSpec((tm, D), lambda l, i: (i, 0)),
                  pl.BlockSpec((None, 1, D), lambda l, i: (l, 0, 0))],
        out_specs=pl.BlockSpec((None, tm, D), lambda l, i: (l, i, 0)),
        out_shape=jax.ShapeDtypeStruct((L, M, D), BF16),
        compiler_params=_params(("parallel", "parallel"), 6 * tm * D * 4),
        name="rmsnorm_cast",
    )(x, g.reshape(L, 1, D))


def _post_kernel(*refs, with_next):
    x_ref, y_ref, g_ref = refs[:3]
    out = x_ref[...] + _rms(y_ref[...].astype(F32), g_ref[...])
    if with_next:
        gn_ref, o_ref, h_ref = refs[3:]
        h_ref[...] = _rms(out, gn_ref[...]).astype(h_ref.dtype)
    else:
        o_ref = refs[3]
    o_ref[...] = out


def post_norm_residual(x, y, g_post, g_next=None, tm=256):
    M, D = x.shape
    tm = min(tm, M)
    row = pl.BlockSpec((tm, D), lambda i: (i, 0))
    vec = pl.BlockSpec((1, D), lambda i: (0, 0))
    with_next = g_next is not None
    args = [x, y, g_post.reshape(1, D)] + ([g_next.reshape(1, D)] if with_next else [])
    out_shape = [jax.ShapeDtypeStruct((M, D), F32)] + ([jax.ShapeDtypeStruct((M, D), BF16)] if with_next else [])
    res = pl.pallas_call(
        functools.partial(_post_kernel, with_next=with_next),
        grid=(M // tm,),
        in_specs=[row, row, vec] + ([vec] if with_next else []),
        out_specs=[row] * len(out_shape),
        out_shape=out_shape,
        compiler_params=_params(("parallel",), 12 * tm * D * 4),
        name="post_norm_residual",
    )(*args)
    return (res[0], res[1]) if with_next else (res[0], None)


BF16_ROW_TILE = 16


def _mm_kernel(*refs, ksizes):
    n_a = len(ksizes)
    a_refs, w_ref, o_ref = refs[:n_a], refs[n_a], refs[n_a + 1]
    acc = None
    off = 0
    for a_ref, ks in zip(a_refs, ksizes):
        part = jnp.dot(a_ref[...], w_ref[off:off + ks, :], preferred_element_type=F32)
        acc = part if acc is None else acc + part
        off += ks
    o_ref[...] = acc.astype(o_ref.dtype)


def matmul(a_list, w, out_dtype, layer=0, tm=1024, tn=1024):
    M = a_list[0].shape[0]
    _, K, N = w.shape
    ksizes = tuple(a.shape[1] for a in a_list)
    assert sum(ksizes) == K
    tm, tn = min(tm, M), min(tn, N)
    osz = jnp.dtype(out_dtype).itemsize
    vmem = 2 * (tm * K * 2 + K * tn * 2 + tm * tn * osz) + 2 * tm * tn * 4
    return pl.pallas_call(
        functools.partial(_mm_kernel, ksizes=ksizes),
        grid=(N // tn, M // tm),
        in_specs=[pl.BlockSpec((tm, ks), lambda j, i: (i, 0)) for ks in ksizes]
        + [pl.BlockSpec((None, K, tn), lambda j, i: (layer, 0, j))],
        out_specs=pl.BlockSpec((tm, tn), lambda j, i: (i, j)),
        out_shape=jax.ShapeDtypeStruct((M, N), out_dtype),
        compiler_params=_params(("parallel", "parallel"), vmem + (4 << 20)),
        name="matmul",
    )(*a_list, w)


def _mm_nt_kernel(a_ref, wt_ref, o_ref):
    o_ref[...] = lax.dot_general(a_ref[...], wt_ref[...], (((1,), (1,)), ((), ())),
                                 preferred_element_type=F32).astype(o_ref.dtype)


def matmul_nt(a, wt, out_dtype, tm=1024, tn=1024):
    M, K = a.shape
    N, _ = wt.shape
    tm, tn = min(tm, M), min(tn, N)
    osz = jnp.dtype(out_dtype).itemsize
    vmem = 2 * (tm * K * 2 + K * tn * 2 + tm * tn * osz) + 2 * tm * tn * 4
    return pl.pallas_call(
        _mm_nt_kernel,
        grid=(N // tn, M // tm),
        in_specs=[pl.BlockSpec((tm, K), lambda j, i: (i, 0)),
                  pl.BlockSpec((tn, K), lambda j, i: (j, 0))],
        out_specs=pl.BlockSpec((tm, tn), lambda j, i: (i, j)),
        out_shape=jax.ShapeDtypeStruct((M, N), out_dtype),
        compiler_params=_params(("parallel", "parallel"), vmem + (4 << 20)),
        name="matmul_nt",
    )(a, wt)


def _mm_castw_kernel(*refs, ksizes, shift, transposed):
    n_a = len(ksizes)
    a_refs = refs[:n_a]
    if shift:
        w_ref, e_ref, o_ref, wbf_ref = refs[n_a:]
    else:
        w_ref, o_ref, wbf_ref = refs[n_a:]

    @pl.when(pl.program_id(1) == 0)
    def _():
        if shift:
            keep = w_ref.shape[0] - shift
            wbf_ref[0:keep, :] = w_ref[shift:, :].astype(wbf_ref.dtype)
            wbf_ref[keep:, :] = e_ref[...].astype(wbf_ref.dtype)
        else:
            wbf_ref[...] = w_ref[...].astype(wbf_ref.dtype)

    if transposed:
        acc = lax.dot_general(a_refs[0][...], wbf_ref[...], (((1,), (1,)), ((), ())), preferred_element_type=F32)
    else:
        acc, off = None, 0
        for a_ref, ks in zip(a_refs, ksizes):
            part = jnp.dot(a_ref[...], wbf_ref[off:off + ks, :], preferred_element_type=F32)
            acc = part if acc is None else acc + part
            off += ks
    o_ref[...] = acc.astype(o_ref.dtype)


def matmul_cast(a_list, w, layer, out_dtype, n0=0, n=None, transposed=False, tm=1024, tn=512):
    M = a_list[0].shape[0]
    ksizes = tuple(a.shape[1] for a in a_list)
    K = sum(ksizes)
    assert not transposed or len(a_list) == 1
    n_all = w.shape[1] if transposed else w.shape[2]
    n = n_all - n0 if n is None else n
    tm, tn = min(tm, M), min(tn, n)
    shift = n0 % tn
    base = n0 - shift
    assert n % tn == 0 and M % tm == 0
    osz = jnp.dtype(out_dtype).itemsize
    if transposed:
        w_specs = [pl.BlockSpec((None, tn, K), lambda j, i: (layer, base // tn + j, 0))]
        if shift:
            assert shift % BF16_ROW_TILE == 0 and tn % shift == 0 and base % shift == 0
            w_specs.append(pl.BlockSpec((None, shift, K), lambda j, i: (layer, (base + (j + 1) * tn) // shift, 0)))
        wbf_spec = pl.BlockSpec((tn, K), lambda j, i: (j, 0))
        wbf_shape = (n, K)
    else:
        assert shift == 0
        w_specs = [pl.BlockSpec((None, K, tn), lambda j, i: (layer, 0, base // tn + j))]
        wbf_spec = pl.BlockSpec((K, tn), lambda j, i: (0, j))
        wbf_shape = (K, n)
    vmem = 2 * (tm * K * 2 + K * tn * 4 + K * tn * 2 + tm * tn * osz) + 2 * tm * tn * 4 + 2 * K * tn * 2
    return pl.pallas_call(
        functools.partial(_mm_castw_kernel, ksizes=ksizes, shift=shift, transposed=transposed),
        grid=(n // tn, M // tm),
        in_specs=[pl.BlockSpec((tm, ks), lambda j, i: (i, 0)) for ks in ksizes] + w_specs,
        out_specs=[pl.BlockSpec((tm, tn), lambda j, i: (i, j)), wbf_spec],
        out_shape=[jax.ShapeDtypeStruct((M, n), out_dtype), jax.ShapeDtypeStruct(wbf_shape, BF16)],
        compiler_params=_params(("parallel", "arbitrary"), vmem + (2 << 20)),
        name="matmul_cast",
    )(*a_list, *([w] * len(w_specs)))


def _mem_proj_kernel(a_ref, w_ref, o_ref, heads_hbm, wbf_scr, stage, sem, *, nheads, nb):
    jj, i = pl.program_id(0), pl.program_id(1)
    step = jj * pl.num_programs(1) + i
    last = pl.num_programs(0) * pl.num_programs(1) - 1
    slot = step % 2

    def head_copy(s):
        return pltpu.make_async_copy(stage.at[s], heads_hbm.at[jj // nheads, pl.ds(i * nb, nb), :, jj % nheads, :],
                                     sem.at[s])

    @pl.when(i == 0)
    def _():
        wbf_scr[...] = w_ref[...].astype(wbf_scr.dtype)

    kv = jnp.dot(a_ref[...], wbf_scr[...], preferred_element_type=F32)
    o_ref[...] = kv

    @pl.when(step >= 2)
    def _():
        head_copy(slot).wait()

    stage[slot] = kv.reshape(stage.shape[1:])
    head_copy(slot).start()

    @pl.when(step == last)
    def _():
        head_copy(slot).wait()

        @pl.when(step >= 1)
        def _():
            head_copy(1 - slot).wait()


def mem_proj(h, w_kv, n0, batch, nheads, tm=1024):
    depth, M, D = h.shape
    S = M // batch
    hd = (w_kv.shape[2] // 2) // nheads
    n = nheads * hd
    tm = min(tm, M)
    assert n0 % hd == 0 and M % tm == 0 and tm % S == 0
    nb = tm // S
    vmem = 2 * (tm * D * 2 + D * hd * 4 + tm * hd * 4) + 4 * tm * hd * 4 + 3 * D * hd * 2
    return pl.pallas_call(
        functools.partial(_mem_proj_kernel, nheads=nheads, nb=nb),
        grid=(depth * nheads, M // tm),
        in_specs=[pl.BlockSpec((None, tm, D), lambda jj, i: (jj // nheads, i, 0)),
                  pl.BlockSpec((None, D, hd), lambda jj, i: (jj // nheads, 0, n0 // hd + jj % nheads))],
        out_specs=[pl.BlockSpec((None, tm, hd), lambda jj, i: (jj // nheads, i, jj % nheads)),
                   pl.BlockSpec(memory_space=pl.ANY)],
        out_shape=[jax.ShapeDtypeStruct((depth, M, n), F32),
                   jax.ShapeDtypeStruct((depth, batch, S, nheads, hd), F32)],
        scratch_shapes=[pltpu.VMEM((D, hd), BF16), pltpu.VMEM((2, nb, S, hd), F32),
                        pltpu.SemaphoreType.DMA((2,))],
        compiler_params=_params(("arbitrary", "arbitrary"), vmem + (2 << 20)),
        name="mem_proj",
    )(h, w_kv)


def _gates_kernel(h_ref, w_ref, b_ref, o_ref):
    nh = MLSTM_HEADS
    pre = lax.dot_general(h_ref[...], w_ref[...], (((1,), (1,)), ((), ())),
                          preferred_element_type=F32) + b_ref[...]
    capped = GATE_SOFTCAP * jnp.tanh(pre / GATE_SOFTCAP)
    log_sig = jnp.minimum(capped, 0.0) - jnp.log(1.0 + jnp.exp(-jnp.abs(capped)))
    lane = lax.broadcasted_iota(jnp.int32, capped.shape, 1)
    o_ref[...] = jnp.where(lane < nh, capped, log_sig)


def mlstm_gates(h, w_gates, bias, tm=1024):
    M, D = h.shape
    tm = min(tm, M)
    return pl.pallas_call(
        _gates_kernel,
        grid=(M // tm,),
        in_specs=[pl.BlockSpec((tm, D), lambda i: (i, 0)),
                  pl.BlockSpec((GATE_LANES, D), lambda i: (0, 0)),
                  pl.BlockSpec((1, GATE_LANES), lambda i: (0, 0))],
        out_specs=pl.BlockSpec((tm, GATE_LANES), lambda i: (i, 0)),
        out_shape=jax.ShapeDtypeStruct((M, GATE_LANES), F32),
        compiler_params=_params(("parallel",), 4 * tm * D * 2 + 4 * D * GATE_LANES * 2),
        name="mlstm_gates",
    )(h, w_gates, bias)


def _group_mm_kernel(a_ref, w_ref, s_ref, z_ref, o_ref):
    mixed = jnp.dot(a_ref[...], w_ref[...], preferred_element_type=F32) * s_ref[...]
    o_ref[...] = (mixed * _silu(z_ref[...].astype(F32))).astype(o_ref.dtype)


def group_mix_gate(pooled, w_group, scale, proj, z_off, tm=512):
    M, D = pooled.shape
    G, Gs, _ = w_group.shape
    tm = min(tm, M)
    zb = z_off // Gs
    return pl.pallas_call(
        _group_mm_kernel,
        grid=(G, M // tm),
        in_specs=[pl.BlockSpec((tm, Gs), lambda g, i: (i, g)),
                  pl.BlockSpec((None, Gs, Gs), lambda g, i: (g, 0, 0)),
                  pl.BlockSpec((1, Gs), lambda g, i: (0, g)),
                  pl.BlockSpec((tm, Gs), lambda g, i: (i, zb + g))],
        out_specs=pl.BlockSpec((tm, Gs), lambda g, i: (i, g)),
        out_shape=jax.ShapeDtypeStruct((M, D), BF16),
        compiler_params=_params(("parallel", "parallel"), 6 * tm * Gs * 4 + 4 * Gs * Gs * 2),
        name="group_mix_gate",
    )(pooled, w_group, scale.reshape(1, D), proj)


def _window_sums(ext, levels):
    sums = []
    cur = ext
    for lv in range(levels):
        cur = cur + pltpu.roll(cur, 1 << lv, 0)
        sums.append(cur)
    return sums


def _pool_mix_prompt_kernel(u_ref, halo_ref, hist_ref, zlo_ref, zhi_ref, w_ref, s_ref, o_ref, *, pos0, tt):
    i = pl.program_id(1)
    ng = len(POOL_WINDOWS)
    gs = u_ref.shape[1] // ng
    row = lax.broadcasted_iota(jnp.int32, (tt, 1), 0) + i * tt
    pos1 = (row + (pos0 + 1)).astype(F32)
    first = i == 0
    for g, w in enumerate(POOL_WINDOWS):
        lo, hi = g * gs, (g + 1) * gs
        u = u_ref[:, lo:hi].astype(F32)
        halo = jnp.where(first, hist_ref[:, lo:hi], halo_ref[:, lo:hi].astype(F32))
        ext = jnp.concatenate([halo, u], axis=0)
        win = _window_sums(ext, g + 1)[-1][HALO:, :]
        inv_cnt = 1.0 / jnp.minimum(float(w), pos1)
        pooled = (win * inv_cnt - u).astype(BF16)
        mixed = jnp.dot(pooled, w_ref[g], preferred_element_type=F32) * s_ref[:, lo:hi]
        z_ref, zg = (zlo_ref, g) if g < ng // 2 else (zhi_ref, g - ng // 2)
        z = z_ref[:, zg * gs:(zg + 1) * gs].astype(F32)
        o_ref[:, lo:hi] = (mixed * _silu(z)).astype(o_ref.dtype)


def pool_mix_prompt(proj, hist, w_group, scale, batch, pos0, z_off, tt=512):
    D = hist.shape[-1]
    G, Gs, _ = w_group.shape
    T = proj.shape[0] // batch
    tt = min(tt, T)
    nt = T // tt
    dh = D // 2
    assert z_off % dh == 0 and G == len(POOL_WINDOWS)
    zb = z_off // dh
    hist16 = jnp.concatenate([jnp.zeros((batch, 1, D), F32), hist.astype(F32)], axis=1)
    vmem = 2 * (3 * tt * D * 2 + G * Gs * Gs * 2) + 12 * (tt + HALO) * Gs * 4
    return pl.pallas_call(
        functools.partial(_pool_mix_prompt_kernel, pos0=pos0, tt=tt),
        grid=(batch, nt),
        in_specs=[pl.BlockSpec((tt, D), lambda b, i: (b * nt + i, 0)),
                  pl.BlockSpec((HALO, D),
                               lambda b, i: (jnp.maximum((b * nt + i) * (tt // HALO) - 1, 0), 0)),
                  pl.BlockSpec((None, HALO, D), lambda b, i: (b, 0, 0)),
                  pl.BlockSpec((tt, dh), lambda b, i: (b * nt + i, zb)),
                  pl.BlockSpec((tt, dh), lambda b, i: (b * nt + i, zb + 1)),
                  pl.BlockSpec((G, Gs, Gs), lambda b, i: (0, 0, 0)),
                  pl.BlockSpec((1, D), lambda b, i: (0, 0))],
        out_specs=pl.BlockSpec((tt, D), lambda b, i: (b * nt + i, 0)),
        out_shape=jax.ShapeDtypeStruct((batch * T, D), BF16),
        compiler_params=_params(("parallel", "parallel"), vmem + (4 << 20)),
        name="pool_mix_prompt",
    )(proj, proj, hist16, proj, proj, w_group, scale.reshape(1, D))


def _pool_sample_kernel(hist_ref, u_ref, o_ref, new_hist_ref, *, pos0):
    steps = u_ref.shape[0]
    gs = u_ref.shape[2] // len(POOL_WINDOWS)

    def row(r, lo, hi):
        if r < POOL_HIST:
            return hist_ref[r, :, lo:hi]
        return u_ref[r - POOL_HIST, :, lo:hi].astype(F32)

    for g, w in enumerate(POOL_WINDOWS):
        lo, hi = g * gs, (g + 1) * gs
        for t in range(steps):
            cur = row(POOL_HIST + t, lo, hi)
            win = cur
            for r in range(1, w):
                win = win + row(POOL_HIST + t - r, lo, hi)
            cnt = float(min(w, pos0 + t + 1))
            o_ref[t, :, lo:hi] = (win / cnt - cur).astype(o_ref.dtype)
    for r in range(POOL_HIST):
        new_hist_ref[r] = row(r + steps, 0, u_ref.shape[2])


def pool_sample(hist_t, u_t, pos0, bt=32):
    T, B, D = u_t.shape
    bt = min(bt, B)
    return pl.pallas_call(
        functools.partial(_pool_sample_kernel, pos0=pos0),
        grid=(B // bt,),
        in_specs=[pl.BlockSpec((POOL_HIST, bt, D), lambda i: (0, i, 0)),
                  pl.BlockSpec((T, bt, D), lambda i: (0, i, 0))],
        out_specs=[pl.BlockSpec((T, bt, D), lambda i: (0, i, 0)),
                   pl.BlockSpec((POOL_HIST, bt, D), lambda i: (0, i, 0))],
        out_shape=[jax.ShapeDtypeStruct((T, B, D), BF16), jax.ShapeDtypeStruct((POOL_HIST, B, D), F32)],
        compiler_params=_params(("parallel",), 5 * (POOL_HIST + T) * bt * D * 4),
        name="pool_sample",
    )(hist_t, u_t)


def _softmax_rows(qk, scale):
    m = jnp.max(qk, axis=-1, keepdims=True)
    p = jnp.exp2((qk - m) * (scale * LOG2E))
    return p, jnp.sum(p, axis=-1, keepdims=True)


def _attn_prompt_kernel(q_ref, k_ref, v_ref, z_ref, o_ref):
    hd = q_ref.shape[1] // XATT_HEADS
    scale = hd ** -0.5
    cols = [(h * hd, (h + 1) * hd) for h in range(XATT_HEADS)]
    scores = [lax.dot_general(q_ref[:, lo:hi], k_ref[:, lo:hi].astype(BF16), (((1,), (1,)), ((), ())),
                              preferred_element_type=F32) for lo, hi in cols]
    probs = [_softmax_rows(s, scale) for s in scores]
    outs = [jnp.dot(p.astype(BF16), v_ref[:, lo:hi].astype(BF16), preferred_element_type=F32) / l
            for (lo, hi), (p, l) in zip(cols, probs)]
    for (lo, hi), o in zip(cols, outs):
        o_ref[:, lo:hi] = (o * _silu(z_ref[:, lo:hi].astype(F32))).astype(o_ref.dtype)


def attend_prompt(q_arr, qb, z_arr, zb, k, v, layer, S, tq=512):
    _, BS, DX = k.shape
    B = BS // S
    T = q_arr.shape[0] // B
    tq = min(tq, T)
    nq = T // tq
    return pl.pallas_call(
        _attn_prompt_kernel,
        grid=(B, nq),
        in_specs=[pl.BlockSpec((tq, DX), lambda b, i: (b * nq + i, qb)),
                  pl.BlockSpec((None, S, DX), lambda b, i: (layer, b, 0)),
                  pl.BlockSpec((None, S, DX), lambda b, i: (layer, b, 0)),
                  pl.BlockSpec((tq, DX), lambda b, i: (b * nq + i, zb))],
        out_specs=pl.BlockSpec((tq, DX), lambda b, i: (b * nq + i, 0)),
        out_shape=jax.ShapeDtypeStruct((B * T, DX), BF16),
        compiler_params=_params(("parallel", "parallel"), 8 * tq * DX * 4 + 8 * S * DX * 4),
        name="attend_prompt",
    )(q_arr, k, v, z_arr)


def _attn_sample_kernel(q_ref, k_hbm, v_hbm, z_ref, o_ref, kbuf, vbuf, sem, *, steps, layer, bb):
    i = pl.program_id(0)
    rows, dx = q_ref.shape
    hd = dx // XATT_HEADS
    scale = hd ** -0.5

    def head_copies(step, slot):
        out = []
        for t, (src, dst) in enumerate(((k_hbm, kbuf), (v_hbm, vbuf))):
            for h in range(XATT_HEADS):
                out.append(pltpu.make_async_copy(src.at[layer, pl.ds(step * bb, bb), :, h, :],
                                                 dst.at[slot, h], sem.at[t, slot, h]))
        return out

    @pl.when(i == 0)
    def _():
        for cp in head_copies(0, 0):
            cp.start()

    @pl.when(i + 1 < pl.num_programs(0))
    def _():
        for cp in head_copies(i + 1, (i + 1) % 2):
            cp.start()

    slot = i % 2
    for cp in head_copies(i, slot):
        cp.wait()

    row_batch = lax.broadcasted_iota(jnp.int32, (rows, 1), 0) // steps
    pairs = [(h, j) for h in range(XATT_HEADS) for j in range(bb)]
    scores = [lax.dot_general(q_ref[:, h * hd:(h + 1) * hd], kbuf[slot, h, j].astype(BF16),
                              (((1,), (1,)), ((), ())), preferred_element_type=F32)
              for h, j in pairs]
    probs = [_softmax_rows(s, scale) for s in scores]
    outs = [jnp.dot(p.astype(BF16), vbuf[slot, h, j].astype(BF16), preferred_element_type=F32) / l
            for (h, j), (p, l) in zip(pairs, probs)]
    for h in range(XATT_HEADS):
        lo, hi = h * hd, (h + 1) * hd
        acc = jnp.zeros((rows, hd), F32)
        for j in range(bb):
            acc = jnp.where(row_batch == j, outs[h * bb + j], acc)
        o_ref[:, lo:hi] = (acc * _silu(z_ref[:, lo:hi].astype(F32))).astype(o_ref.dtype)


def attend_sample(q_arr, qb, z_arr, zb, k, v, layer, steps, bb=4):
    _, B, S, nh, hd = k.shape
    assert nh == XATT_HEADS
    DX = nh * hd
    rows = bb * steps
    slot_bytes = nh * bb * S * hd * 4
    return pl.pallas_call(
        functools.partial(_attn_sample_kernel, steps=steps, layer=layer, bb=bb),
        grid=(B // bb,),
        in_specs=[pl.BlockSpec((rows, DX), lambda i: (i, qb)),
                  pl.BlockSpec(memory_space=pl.ANY), pl.BlockSpec(memory_space=pl.ANY),
                  pl.BlockSpec((rows, DX), lambda i: (i, zb))],
        out_specs=pl.BlockSpec((rows, DX), lambda i: (i, 0)),
        out_shape=jax.ShapeDtypeStruct((B * steps, DX), BF16),
        scratch_shapes=[pltpu.VMEM((2, nh, bb, S, hd), F32), pltpu.VMEM((2, nh, bb, S, hd), F32),
                        pltpu.SemaphoreType.DMA((2, 2, nh))],
        compiler_params=_params(("arbitrary",), 4 * slot_bytes + (12 << 20)),
        name="attend_sample",
    )(q_arr, k, v, z_arr)


def _head_output(num, inv_den, o, z, g):
    ms = jnp.mean(num * num, axis=-1, keepdims=True) * (inv_den * inv_den)
    row_scale = inv_den * lax.rsqrt(ms + NORM_EPS)
    return (num * row_scale) * (g * (_sigmoid(o) * _silu(z)))


def _mlstm_prompt_kernel(q_ref, k_ref, v_ref, o_ref, zlo_ref, zhi_ref, g_ref, gh_ref,
                         y_ref, c_out, n_out, m_out, ct_scr, n_scr, m_scr):
    c = pl.program_id(1)
    nh = MLSTM_HEADS
    L = q_ref.shape[0]
    dqk = q_ref.shape[1] // nh
    dv = v_ref.shape[1] // nh
    qscale = dqk ** -0.5

    @pl.when(c == 0)
    def _():
        ct_scr[...] = jnp.zeros_like(ct_scr)
        n_scr[...] = jnp.zeros_like(n_scr)
        m_scr[...] = jnp.zeros_like(m_scr)

    r_i = lax.broadcasted_iota(jnp.int32, (L, L), 0)
    c_i = lax.broadcasted_iota(jnp.int32, (L, L), 1)
    causal = c_i <= r_i
    gates = g_ref[...]

    tril16 = jnp.where(causal, 1.0, 0.0).astype(BF16)
    g_hi = gates.astype(BF16)
    rest = gates - g_hi.astype(F32)
    g_mid = rest.astype(BF16)
    g_lo = (rest - g_mid.astype(F32)).astype(BF16)
    cums = (jnp.dot(tril16, g_hi, preferred_element_type=F32)
            + jnp.dot(tril16, g_mid, preferred_element_type=F32)
            + jnp.dot(tril16, g_lo, preferred_element_type=F32))
    gates_t = gates.T
    cums_t = cums.T

    def decay_weights(h):
        m_prev = m_scr[h:h + 1, 0:1]
        b_col = cums[:, nh + h:nh + h + 1]
        rel_row = gates_t[h:h + 1, :] - cums_t[nh + h:nh + h + 1, :]
        b2 = b_col * LOG2E
        d2 = jnp.where(causal, b2 + rel_row * LOG2E, -jnp.inf)
        inter2 = b2 + m_prev * LOG2E
        m2 = jnp.maximum(inter2, jnp.max(d2, axis=1, keepdims=True))
        return dict(m_prev=m_prev, b_col=b_col, m2=m2, m_t=m2 * LN2,
                    w=jnp.exp2(d2 - m2), w_inter=jnp.exp2(inter2 - m2))

    def read_out(h, st):
        q = q_ref[:, h * dqk:(h + 1) * dqk].astype(F32) * qscale
        k = k_ref[:, h * dqk:(h + 1) * dqk]
        v = v_ref[:, h * dv:(h + 1) * dv]
        s = lax.dot_general(q.astype(BF16), k, (((1,), (1,)), ((), ())), preferred_element_type=F32) * st["w"]
        qw = q * st["w_inter"]
        num = (jnp.dot(qw.astype(BF16), ct_scr[h].astype(BF16), preferred_element_type=F32)
               + jnp.dot(s.astype(BF16), v, preferred_element_type=F32))
        den = (jnp.sum(s, axis=1, keepdims=True)
               + jnp.sum(qw * n_scr[h:h + 1, :], axis=1, keepdims=True))
        return num, 1.0 / jnp.maximum(jnp.abs(den), jnp.exp2(-st["m2"]))

    def update_state(h, st):
        k = k_ref[:, h * dqk:(h + 1) * dqk]
        v = v_ref[:, h * dv:(h + 1) * dv]
        b_col, m_prev = st["b_col"], st["m_prev"]
        b_last = b_col[L - 1:L, :]
        m_new = st["m_t"][L - 1:L, :]
        w_state = jnp.exp(b_last - b_col + gates[:, h:h + 1] - m_new)
        decay = jnp.exp(b_last + m_prev - m_new)
        kw = k.astype(F32) * w_state
        ct_scr[h] = decay * ct_scr[h] + lax.dot_general(kw.astype(BF16), v, (((0,), (0,)), ((), ())),
                                                        preferred_element_type=F32)
        n_scr[h:h + 1, :] = decay * n_scr[h:h + 1, :] + jnp.sum(kw, axis=0, keepdims=True)
        m_scr[h:h + 1, :] = jnp.broadcast_to(m_new, (1, m_scr.shape[1]))

    def write_out(h, num, inv_den):
        lo, hi = h * dv, (h + 1) * dv
        z_ref, zh = (zlo_ref, h) if h < nh // 2 else (zhi_ref, h - nh // 2)
        y_ref[:, lo:hi] = _head_output(num, inv_den, o_ref[:, lo:hi].astype(F32),
                                       z_ref[:, zh * dv:(zh + 1) * dv].astype(F32),
                                       gh_ref[:, lo:hi]).astype(y_ref.dtype)

    for h0 in range(0, nh, MLSTM_HEAD_GROUP):
        group = range(h0, h0 + MLSTM_HEAD_GROUP)
        stats = {h: decay_weights(h) for h in group}
        outs = {h: read_out(h, stats[h]) for h in group}
        for h in group:
            update_state(h, stats[h])
        for h in group:
            write_out(h, *outs[h])

    @pl.when(c == pl.num_programs(1) - 1)
    def _():
        for h in range(nh):
            c_out[h] = ct_scr[h].T
        n_out[...] = n_scr[...]
        m_out[...] = m_scr[...]


def mlstm_prompt(proj, proj_z, gates, g_head, batch, dqk, dv):
    nh = MLSTM_HEADS
    T = proj.shape[0] // batch
    L = min(MLSTM_CHUNK, T)
    nc = T // L
    wq, wv = nh * dqk, nh * dv
    assert wv == 2 * wq
    rows = lambda b, c: b * nc + c
    return pl.pallas_call(
        _mlstm_prompt_kernel,
        grid=(batch, nc),
        in_specs=[pl.BlockSpec((L, wq), lambda b, c: (rows(b, c), 0)),
                  pl.BlockSpec((L, wq), lambda b, c: (rows(b, c), 1)),
                  pl.BlockSpec((L, wv), lambda b, c: (rows(b, c), 1)),
                  pl.BlockSpec((L, wv), lambda b, c: (rows(b, c), 2)),
                  pl.BlockSpec((L, wq), lambda b, c: (rows(b, c), 1)),
                  pl.BlockSpec((L, wq), lambda b, c: (rows(b, c), 2)),
                  pl.BlockSpec((L, GATE_LANES), lambda b, c: (rows(b, c), 0)),
                  pl.BlockSpec((1, wv), lambda b, c: (0, 0))],
        out_specs=[pl.BlockSpec((L, wv), lambda b, c: (rows(b, c), 0)),
                   pl.BlockSpec((None, nh, dv, dqk), lambda b, c: (b, 0, 0, 0)),
                   pl.BlockSpec((None, nh, dqk), lambda b, c: (b, 0, 0)),
                   pl.BlockSpec((None, nh, GATE_LANES), lambda b, c: (b, 0, 0))],
        out_shape=[jax.ShapeDtypeStruct((batch * T, wv), BF16),
                   jax.ShapeDtypeStruct((batch, nh, dv, dqk), F32),
                   jax.ShapeDtypeStruct((batch, nh, dqk), F32),
                   jax.ShapeDtypeStruct((batch, nh, GATE_LANES), F32)],
        scratch_shapes=[pltpu.VMEM((nh, dqk, dv), F32),
                        pltpu.VMEM((nh, dqk), F32),
                        pltpu.VMEM((nh, GATE_LANES), F32)],
        compiler_params=_params(("parallel", "arbitrary"),
                                24 * L * wv + 5 * nh * dv * dqk * 4 + (16 << 20)),
        name="mlstm_prompt",
    )(proj, proj, proj, proj, proj_z, proj_z, gates, g_head.reshape(1, wv))


def _lane_pick(x, lane, idx):
    return jnp.sum(jnp.where(lane == idx, x, 0.0), axis=1, keepdims=True)


def _mlstm_sample_kernel(q_ref, k_ref, v_ref, o_ref, z_ref, g_ref, gh_ref, c_ref, n_ref,
                         y_ref, c_out, n_out, m_out, *, steps):
    head = pl.program_id(1)
    nh = MLSTM_HEADS
    R, dqk = q_ref.shape
    nb = R // steps
    qscale = dqk ** -0.5

    gates = g_ref[...]
    lane = lax.broadcasted_iota(jnp.int32, gates.shape, 1)
    ig = _lane_pick(gates, lane, head)
    lf = _lane_pick(gates, lane, nh + head)
    m_prev = _lane_pick(gates, lane, 2 * nh + head)

    r_i = lax.broadcasted_iota(jnp.int32, (R, R), 0)
    c_i = lax.broadcasted_iota(jnp.int32, (R, R), 1)
    same = (r_i // steps) == (c_i // steps)
    causal = same & (c_i <= r_i)
    eye = (c_i == r_i).astype(F32)
    lower = causal.astype(F32)
    upper = (same & (r_i <= c_i)).astype(F32)
    last_of_row_batch = (c_i == (r_i // steps) * steps + (steps - 1)).astype(F32)
    row_batch = lax.broadcasted_iota(jnp.int32, (R, 1), 0) // steps

    q = q_ref[...].astype(F32) * qscale
    qb = q.astype(BF16)
    k = k_ref[...]
    v = v_ref[...]

    lf_row = jnp.sum(lf * eye, axis=0, keepdims=True)
    ig_row = jnp.sum(ig * eye, axis=0, keepdims=True)
    b_col = jnp.sum(lower * lf_row, axis=1, keepdims=True)
    b_row = jnp.sum(upper * lf, axis=0, keepdims=True)
    d = jnp.where(causal, b_col - b_row + ig_row, -jnp.inf)
    inter = b_col + m_prev
    m_t = jnp.maximum(inter, jnp.max(d, axis=1, keepdims=True))
    w = jnp.exp(d - m_t)
    w_inter = jnp.exp(inter - m_t)
    s = lax.dot_general(qb, k, (((1,), (1,)), ((), ())), preferred_element_type=F32) * w

    qc = jnp.zeros((R, v_ref.shape[1]), F32)
    n_rows = jnp.zeros((R, dqk), F32)
    for j in range(nb):
        mine = row_batch == j
        qcj = lax.dot_general(qb, c_ref[j].astype(BF16), (((1,), (1,)), ((), ())),
                              preferred_element_type=F32)
        qc = jnp.where(mine, qcj, qc)
        n_rows = jnp.where(mine, n_ref[j:j + 1, :], n_rows)

    num = jnp.dot(s.astype(BF16), v, preferred_element_type=F32) + w_inter * qc
    den = jnp.sum(s, axis=1, keepdims=True) + w_inter * jnp.sum(q * n_rows, axis=1, keepdims=True)
    inv_den = 1.0 / jnp.maximum(jnp.abs(den), jnp.exp(-m_t))
    y_ref[...] = _head_output(num, inv_den, o_ref[...].astype(F32), z_ref[...].astype(F32),
                              gh_ref[...]).astype(y_ref.dtype)

    m_row = jnp.sum(m_t * eye, axis=0, keepdims=True)
    b_last = jnp.sum(last_of_row_batch * b_row, axis=1, keepdims=True)
    m_new = jnp.sum(last_of_row_batch * m_row, axis=1, keepdims=True)
    w_state = jnp.exp(b_last - b_col + ig - m_new)
    decay = jnp.exp(b_last + m_prev - m_new)
    vw = v.astype(F32) * w_state
    kw = k.astype(F32) * w_state
    for j in range(nb):
        mine = row_batch == j
        r0 = j * steps
        dj = decay[r0:r0 + 1, :]
        upd = lax.dot_general(jnp.where(mine, vw, 0.0).astype(BF16), k, (((0,), (0,)), ((), ())),
                              preferred_element_type=F32)
        c_out[j] = dj * c_ref[j] + upd
        n_out[j:j + 1, :] = dj * n_ref[j:j + 1, :] + jnp.sum(jnp.where(mine, kw, 0.0), axis=0, keepdims=True)
        m_out[j:j + 1, :] = jnp.broadcast_to(m_new[r0:r0 + 1, :], (1, m_out.shape[1]))


def mlstm_sample(proj, proj_z, gates, g_head, c0, n0, steps, dqk, dv):
    nh = MLSTM_HEADS
    B = c0.shape[0]
    nb = SAMPLE_BATCH_BLOCK
    R = nb * steps
    kq = nh
    kv = (2 * nh * dqk) // dv
    ko = kv + nh
    kz = (nh * dqk) // dv
    return pl.pallas_call(
        functools.partial(_mlstm_sample_kernel, steps=steps),
        grid=(B // nb, nh),
        in_specs=[pl.BlockSpec((R, dqk), lambda i, h: (i, h)),
                  pl.BlockSpec((R, dqk), lambda i, h: (i, kq + h)),
                  pl.BlockSpec((R, dv), lambda i, h: (i, kv + h)),
                  pl.BlockSpec((R, dv), lambda i, h: (i, ko + h)),
                  pl.BlockSpec((R, dv), lambda i, h: (i, kz + h)),
                  pl.BlockSpec((R, GATE_LANES), lambda i, h: (i, 0)),
                  pl.BlockSpec((1, dv), lambda i, h: (0, h)),
                  pl.BlockSpec((nb, None, dv, dqk), lambda i, h: (i, h, 0, 0)),
                  pl.BlockSpec((None, nb, dqk), lambda i, h: (h, i, 0))],
        out_specs=[pl.BlockSpec((R, dv), lambda i, h: (i, h)),
                   pl.BlockSpec((nb, None, dv, dqk), lambda i, h: (i, h, 0, 0)),
                   pl.BlockSpec((None, nb, dqk), lambda i, h: (h, i, 0)),
                   pl.BlockSpec((None, nb, GATE_LANES), lambda i, h: (h, i, 0))],
        out_shape=[jax.ShapeDtypeStruct((B * steps, nh * dv), BF16),
                   jax.ShapeDtypeStruct(c0.shape, F32),
                   jax.ShapeDtypeStruct((nh, B, dqk), F32),
                   jax.ShapeDtypeStruct((nh, B, GATE_LANES), F32)],
        compiler_params=_params(("parallel", "parallel"), 6 * nb * dv * dqk * 4 + (8 << 20)),
        name="mlstm_sample",
    )(proj, proj, proj, proj, proj_z, gates, g_head.reshape(1, nh * dv), c0, n0)


def _finish_casting(x, h_mix, h_att, w_out, layer, g_post, g_next):
    y, w16 = matmul_cast([h_mix, h_att], w_out, layer, BF16, tm=512, tn=256)
    return post_norm_residual(x, y, g_post, g_next) + (w16,)


def _finish(x, h_mix, h_att, w16, g_post, g_next):
    y = matmul([h_mix, h_att], w16[None], BF16, tm=512, tn=1024)
    return post_norm_residual(x, y, g_post, g_next)


def kernel(x_prompt, x_sample, mem_prompt, state_pool, state_mlstm_C, state_mlstm_n, state_mlstm_m,
           cache_mem_k, cache_mem_v, norm_pre, norm_post, norm_mem, w_mem_kv, w_out,
           w_in_pool, w_pool_group, pool_scale, w_in_mlstm, b_igate, b_fgate, mlstm_head_norm):
    B, T, D = x_prompt.shape
    Bs, Ts, _ = x_sample.shape
    S = mem_prompt.shape[1]
    depth = w_out.shape[0]
    nh = MLSTM_HEADS
    dx = D // 2
    dv = D // nh
    dqk = dv // 2

    xp = x_prompt.reshape(B * T, D)
    xs = x_sample.reshape(Bs * Ts, D)
    mem = mem_prompt.reshape(B * S, D)
    hp = rmsnorm_cast(xp, norm_pre[:1])[0]
    hs = rmsnorm_cast(xs, norm_pre[:1])[0]

    h_mem = rmsnorm_cast(mem, norm_mem)
    mem_k, mem_k_heads = mem_proj(h_mem, w_mem_kv, 0, B, XATT_HEADS)
    mem_v, mem_v_heads = mem_proj(h_mem, w_mem_kv, dx, B, XATT_HEADS)
    w_mlstm_t = jnp.swapaxes(w_in_mlstm, 1, 2)

    pool_p_l, pool_s_l = [], []
    cp_l, np_l, mp_l, cs_l, ns_l, ms_l = [], [], [], [], [], []
    for layer in range(depth):
        j = layer // 2
        g_next = norm_pre[layer + 1] if layer + 1 < depth else None
        if layer % 2 == 0:
            w_g = w_pool_group[j].astype(BF16)
            z_mix, qb, zb = D + dx, D // dx, (2 * D + dx) // dx
            proj, w_pool16 = matmul_cast([hs], w_in_pool, j, BF16)
            u_t = proj[:, :D].reshape(Bs, Ts, D).transpose(1, 0, 2)
            pooled, hist_t = pool_sample(state_pool[j].transpose(1, 0, 2), u_t, PAST_LEN)
            pooled = pooled.transpose(1, 0, 2).reshape(Bs * Ts, D)
            mix = group_mix_gate(pooled, w_g, pool_scale[j], proj, z_mix)
            att = attend_sample(proj, qb, proj, zb, cache_mem_k, cache_mem_v, layer, Ts)
            xs, hs, w_o16 = _finish_casting(xs, mix, att, w_out, layer, norm_post[layer], g_next)
            pool_s_l.append(hist_t.transpose(1, 0, 2))
            proj = matmul([hp], w_pool16[None], BF16)
            mix = pool_mix_prompt(proj, jnp.zeros((B, POOL_HIST, D), F32), w_g, pool_scale[j], B, 0, z_mix)
            att = attend_prompt(proj, qb, proj, zb, mem_k, mem_v, layer, S)
            xp, hp = _finish(xp, mix, att, w_o16, norm_post[layer], g_next)
            pool_p_l.append(proj.reshape(B, T, -1)[:, T - POOL_HIST:, :D].astype(F32))
        else:
            bias = jnp.pad(jnp.concatenate([b_igate[j], b_fgate[j]]), (0, GATE_LANES - 2 * nh))
            bias = bias.reshape(1, GATE_LANES)
            qb, zb = 0, (dx + D) // dx
            g0 = 3 * D
            w_gates = jnp.pad(w_mlstm_t[j, g0:g0 + 2 * nh, :], ((0, GATE_LANES - 2 * nh), (0, 0))).astype(BF16)
            proj, w_main = matmul_cast([hs], w_mlstm_t, j, BF16, 0, g0, transposed=True)
            proj_z, w_z = matmul_cast([hs], w_mlstm_t, j, BF16, g0 + 2 * nh, None, transposed=True)
            gates = mlstm_gates(hs, w_gates, bias)
            m_rows = jnp.repeat(state_mlstm_m[j], Ts, axis=0)
            gates = jnp.concatenate([gates[:, :2 * nh], m_rows,
                                     jnp.zeros((Bs * Ts, GATE_LANES - 3 * nh), F32)], axis=1)
            mix, c1, n1, m1 = mlstm_sample(proj, proj_z, gates, mlstm_head_norm[j], state_mlstm_C[j],
                                           state_mlstm_n[j].transpose(1, 0, 2), Ts, dqk, dv)
            att = attend_sample(proj_z, qb, proj_z, zb, cache_mem_k, cache_mem_v, layer, Ts)
            xs, hs, w_o16 = _finish_casting(xs, mix, att, w_out, layer, norm_post[layer], g_next)
            cs_l.append(c1); ns_l.append(n1.transpose(1, 0, 2)); ms_l.append(m1[:, :, 0].T)
            proj = matmul_nt(hp, w_main, BF16)
            proj_z = matmul_nt(hp, w_z, BF16)
            gates = mlstm_gates(hp, w_gates, bias)
            mix, c1, n1, m1 = mlstm_prompt(proj, proj_z, gates, mlstm_head_norm[j], B, dqk, dv)
            att = attend_prompt(proj_z, qb, proj_z, zb, mem_k, mem_v, layer, S)
            xp, hp = _finish(xp, mix, att, w_o16, norm_post[layer], g_next)
            cp_l.append(c1); np_l.append(n1); mp_l.append(m1[:, :, 0])

    return (xp.reshape(B, T, D), xs.reshape(Bs, Ts, D), jnp.stack(pool_p_l),
            mem_k_heads, mem_v_heads,
            jnp.stack(cp_l), jnp.stack(np_l), jnp.stack(mp_l),
            jnp.stack(pool_s_l), jnp.stack(cs_l), jnp.stack(ns_l), jnp.stack(ms_l))
```

```python
import functools

import jax
import jax.numpy as jnp
from jax import lax
from jax.experimental import pallas as pl
from jax.experimental.pallas import tpu as pltpu

F32 = jnp.float32
BF16 = jnp.bfloat16

NORM_EPS = 1e-6
GATE_SOFTCAP = 15.0
POOL_WINDOWS = (2, 4, 8, 16)
POOL_HIST = max(POOL_WINDOWS) - 1
HALO = POOL_HIST + 1
XATT_HEADS = 4
MLSTM_HEADS = 8
PAST_LEN = 16384
GATE_LANES = 128
MLSTM_CHUNK = 256
MLSTM_HEAD_GROUP = 2
SAMPLE_BATCH_BLOCK = 16

V7X_VMEM_BYTES = 64 * 1024 * 1024
VMEM_CAP = V7X_VMEM_BYTES - 8 * 1024 * 1024


def _params(semantics, vmem_bytes):
    limit = int(min(max(vmem_bytes, 16 * 1024 * 1024), VMEM_CAP))
    return pltpu.CompilerParams(dimension_semantics=semantics, vmem_limit_bytes=limit)


LOG2E = 1.4426950408889634
LN2 = 0.6931471805599453
NEG_LOG2E = -LOG2E


def _sigmoid(x):
    return 1.0 / (1.0 + jnp.exp2(x * NEG_LOG2E))


def _silu(x):
    return x * _sigmoid(x)


def _rms(x, g):
    r = lax.rsqrt(jnp.mean(x * x, axis=-1, keepdims=True) + NORM_EPS)
    return x * r * g


def _rmsnorm_kernel(x_ref, g_ref, o_ref):
    o_ref[...] = _rms(x_ref[...], g_ref[...]).astype(o_ref.dtype)


ROWWISE_MIN_STEPS = 8


def rmsnorm_cast(x, g, tm=512):
    M, D = x.shape
    L = g.shape[0]
    tm = max(min(tm, M // ROWWISE_MIN_STEPS), BF16_ROW_TILE)
    return pl.pallas_call(
        _rmsnorm_kernel,
        grid=(L, M // tm),
        in_specs=[pl.BlockSpec((tm, D), lambda l, i: (i, 0)),
                  pl.BlockSpec((None, 1, D), lambda l, i: (l, 0, 0))],
        out_specs=pl.BlockSpec((None, tm, D), lambda l, i: (l, i, 0)),
        out_shape=jax.ShapeDtypeStruct((L, M, D), BF16),
        compiler_params=_params(("parallel", "parallel"), 6 * tm * D * 4),
        name="rmsnorm_cast",
    )(x, g.reshape(L, 1, D))


def _post_kernel(*refs, with_next):
    x_ref, y_ref, g_ref = refs[:3]
    out = x_ref[...] + _rms(y_ref[...].astype(F32), g_ref[...])
    if with_next:
        gn_ref, o_ref, h_ref = refs[3:]
        h_ref[...] = _rms(out, gn_ref[...]).astype(h_ref.dtype)
    else:
        o_ref = refs[3]
    o_ref[...] = out


def post_norm_residual(x, y, g_post, g_next=None, tm=256):
    M, D = x.shape
    tm = min(tm, M)
    row = pl.BlockSpec((tm, D), lambda i: (i, 0))
    vec = pl.BlockSpec((1, D), lambda i: (0, 0))
    with_next = g_next is not None
    args = [x, y, g_post.reshape(1, D)] + ([g_next.reshape(1, D)] if with_next else [])
    out_shape = [jax.ShapeDtypeStruct((M, D), F32)] + ([jax.ShapeDtypeStruct((M, D), BF16)] if with_next else [])
    res = pl.pallas_call(
        functools.partial(_post_kernel, with_next=with_next),
        grid=(M // tm,),
        in_specs=[row, row, vec] + ([vec] if with_next else []),
        out_specs=[row] * len(out_shape),
        out_shape=out_shape,
        compiler_params=_params(("parallel",), 12 * tm * D * 4),
        name="post_norm_residual",
    )(*args)
    return (res[0], res[1]) if with_next else (res[0], None)


BF16_ROW_TILE = 16


def _mm_kernel(*refs, ksizes):
    n_a = len(ksizes)
    a_refs, w_ref, o_ref = refs[:n_a], refs[n_a], refs[n_a + 1]
    acc = None
    off = 0
    for a_ref, ks in zip(a_refs, ksizes):
        part = jnp.dot(a_ref[...], w_ref[off:off + ks, :], preferred_element_type=F32)
        acc = part if acc is None else acc + part
        off += ks
    o_ref[...] = acc.astype(o_ref.dtype)


def matmul(a_list, w, out_dtype, layer=0, tm=1024, tn=1024):
    M = a_list[0].shape[0]
    _, K, N = w.shape
    ksizes = tuple(a.shape[1] for a in a_list)
    assert sum(ksizes) == K
    tm, tn = min(tm, M), min(tn, N)
    osz = jnp.dtype(out_dtype).itemsize
    vmem = 2 * (tm * K * 2 + K * tn * 2 + tm * tn * osz) + 2 * tm * tn * 4
    return pl.pallas_call(
        functools.partial(_mm_kernel, ksizes=ksizes),
        grid=(N // tn, M // tm),
        in_specs=[pl.BlockSpec((tm, ks), lambda j, i: (i, 0)) for ks in ksizes]
        + [pl.BlockSpec((None, K, tn), lambda j, i: (layer, 0, j))],
        out_specs=pl.BlockSpec((tm, tn), lambda j, i: (i, j)),
        out_shape=jax.ShapeDtypeStruct((M, N), out_dtype),
        compiler_params=_params(("parallel", "parallel"), vmem + (4 << 20)),
        name="matmul",
    )(*a_list, w)


def _mm_nt_kernel(a_ref, wt_ref, o_ref):
    o_ref[...] = lax.dot_general(a_ref[...], wt_ref[...], (((1,), (1,)), ((), ())),
                                 preferred_element_type=F32).astype(o_ref.dtype)


def matmul_nt(a, wt, out_dtype, tm=1024, tn=1024):
    M, K = a.shape
    N, _ = wt.shape
    tm, tn = min(tm, M), min(tn, N)
    osz = jnp.dtype(out_dtype).itemsize
    vmem = 2 * (tm * K * 2 + K * tn * 2 + tm * tn * osz) + 2 * tm * tn * 4
    return pl.pallas_call(
        _mm_nt_kernel,
        grid=(N // tn, M // tm),
        in_specs=[pl.BlockSpec((tm, K), lambda j, i: (i, 0)),
                  pl.BlockSpec((tn, K), lambda j, i: (j, 0))],
        out_specs=pl.BlockSpec((tm, tn), lambda j, i: (i, j)),
        out_shape=jax.ShapeDtypeStruct((M, N), out_dtype),
        compiler_params=_params(("parallel", "parallel"), vmem + (4 << 20)),
        name="matmul_nt",
    )(a, wt)


def _mm_castw_kernel(*refs, ksizes, shift, transposed):
    n_a = len(ksizes)
    a_refs = refs[:n_a]
    if shift:
        w_ref, e_ref, o_ref, wbf_ref = refs[n_a:]
    else:
        w_ref, o_ref, wbf_ref = refs[n_a:]

    @pl.when(pl.program_id(1) == 0)
    def _():
        if shift:
            keep = w_ref.shape[0] - shift
            wbf_ref[0:keep, :] = w_ref[shift:, :].astype(wbf_ref.dtype)
            wbf_ref[keep:, :] = e_ref[...].astype(wbf_ref.dtype)
        else:
            wbf_ref[...] = w_ref[...].astype(wbf_ref.dtype)

    if transposed:
        acc = lax.dot_general(a_refs[0][...], wbf_ref[...], (((1,), (1,)), ((), ())), preferred_element_type=F32)
    else:
        acc, off = None, 0
        for a_ref, ks in zip(a_refs, ksizes):
            part = jnp.dot(a_ref[...], wbf_ref[off:off + ks, :], preferred_element_type=F32)
            acc = part if acc is None else acc + part
            off += ks
    o_ref[...] = acc.astype(o_ref.dtype)


def matmul_cast(a_list, w, layer, out_dtype, n0=0, n=None, transposed=False, tm=1024, tn=512):
    M = a_list[0].shape[0]
    ksizes = tuple(a.shape[1] for a in a_list)
    K = sum(ksizes)
    assert not transposed or len(a_list) == 1
    n_all = w.shape[1] if transposed else w.shape[2]
    n = n_all - n0 if n is None else n
    tm, tn = min(tm, M), min(tn, n)
    shift = n0 % tn
    base = n0 - shift
    assert n % tn == 0 and M % tm == 0
    osz = jnp.dtype(out_dtype).itemsize
    if transposed:
        w_specs = [pl.BlockSpec((None, tn, K), lambda j, i: (layer, base // tn + j, 0))]
        if shift:
            assert shift % BF16_ROW_TILE == 0 and tn % shift == 0 and base % shift == 0
            w_specs.append(pl.BlockSpec((None, shift, K), lambda j, i: (layer, (base + (j + 1) * tn) // shift, 0)))
        wbf_spec = pl.BlockSpec((tn, K), lambda j, i: (j, 0))
        wbf_shape = (n, K)
    else:
        assert shift == 0
        w_specs = [pl.BlockSpec((None, K, tn), lambda j, i: (layer, 0, base // tn + j))]
        wbf_spec = pl.BlockSpec((K, tn), lambda j, i: (0, j))
        wbf_shape = (K, n)
    vmem = 2 * (tm * K * 2 + K * tn * 4 + K * tn * 2 + tm * tn * osz) + 2 * tm * tn * 4 + 2 * K * tn * 2
    return pl.pallas_call(
        functools.partial(_mm_castw_kernel, ksizes=ksizes, shift=shift, transposed=transposed),
        grid=(n // tn, M // tm),
        in_specs=[pl.BlockSpec((tm, ks), lambda j, i: (i, 0)) for ks in ksizes] + w_specs,
        out_specs=[pl.BlockSpec((tm, tn), lambda j, i: (i, j)), wbf_spec],
        out_shape=[jax.ShapeDtypeStruct((M, n), out_dtype), jax.ShapeDtypeStruct(wbf_shape, BF16)],
        compiler_params=_params(("parallel", "arbitrary"), vmem + (2 << 20)),
        name="matmul_cast",
    )(*a_list, *([w] * len(w_specs)))


def _mem_proj_kernel(a_ref, w_ref, o_ref, heads_hbm, wbf_scr, stage, sem, *, nheads, nb):
    jj, i = pl.program_id(0), pl.program_id(1)
    step = jj * pl.num_programs(1) + i
    last = pl.num_programs(0) * pl.num_programs(1) - 1
    slot = step % 2

    def head_copy(s):
        return pltpu.make_async_copy(stage.at[s], heads_hbm.at[jj // nheads, pl.ds(i * nb, nb), :, jj % nheads, :],
                                     sem.at[s])

    @pl.when(i == 0)
    def _():
        wbf_scr[...] = w_ref[...].astype(wbf_scr.dtype)

    kv = jnp.dot(a_ref[...], wbf_scr[...], preferred_element_type=F32)
    o_ref[...] = kv

    @pl.when(step >= 2)
    def _():
        head_copy(slot).wait()

    stage[slot] = kv.reshape(stage.shape[1:])
    head_copy(slot).start()

    @pl.when(step == last)
    def _():
        head_copy(slot).wait()

        @pl.when(step >= 1)
        def _():
            head_copy(1 - slot).wait()


def mem_proj(h, w_kv, n0, batch, nheads, tm=1024):
    depth, M, D = h.shape
    S = M // batch
    hd = (w_kv.shape[2] // 2) // nheads
    n = nheads * hd
    tm = min(tm, M)
    assert n0 % hd == 0 and M % tm == 0 and tm % S == 0
    nb = tm // S
    vmem = 2 * (tm * D * 2 + D * hd * 4 + tm * hd * 4) + 4 * tm * hd * 4 + 3 * D * hd * 2
    return pl.pallas_call(
        functools.partial(_mem_proj_kernel, nheads=nheads, nb=nb),
        grid=(depth * nheads, M // tm),
        in_specs=[pl.BlockSpec((None, tm, D), lambda jj, i: (jj // nheads, i, 0)),
                  pl.BlockSpec((None, D, hd), lambda jj, i: (jj // nheads, 0, n0 // hd + jj % nheads))],
        out_specs=[pl.BlockSpec((None, tm, hd), lambda jj, i: (jj // nheads, i, jj % nheads)),
                   pl.BlockSpec(memory_space=pl.ANY)],
        out_shape=[jax.ShapeDtypeStruct((depth, M, n), F32),
                   jax.ShapeDtypeStruct((depth, batch, S, nheads, hd), F32)],
        scratch_shapes=[pltpu.VMEM((D, hd), BF16), pltpu.VMEM((2, nb, S, hd), F32),
                        pltpu.SemaphoreType.DMA((2,))],
        compiler_params=_params(("arbitrary", "arbitrary"), vmem + (2 << 20)),
        name="mem_proj",
    )(h, w_kv)


def _gates_kernel(h_ref, w_ref, b_ref, o_ref):
    nh = MLSTM_HEADS
    pre = lax.dot_general(h_ref[...], w_ref[...], (((1,), (1,)), ((), ())),
                          preferred_element_type=F32) + b_ref[...]
    capped = GATE_SOFTCAP * jnp.tanh(pre / GATE_SOFTCAP)
    log_sig = jnp.minimum(capped, 0.0) - jnp.log(1.0 + jnp.exp(-jnp.abs(capped)))
    lane = lax.broadcasted_iota(jnp.int32, capped.shape, 1)
    o_ref[...] = jnp.where(lane < nh, capped, log_sig)


def mlstm_gates(h, w_gates, bias, tm=1024):
    M, D = h.shape
    tm = min(tm, M)
    return pl.pallas_call(
        _gates_kernel,
        grid=(M // tm,),
        in_specs=[pl.BlockSpec((tm, D), lambda i: (i, 0)),
                  pl.BlockSpec((GATE_LANES, D), lambda i: (0, 0)),
                  pl.BlockSpec((1, GATE_LANES), lambda i: (0, 0))],
        out_specs=pl.BlockSpec((tm, GATE_LANES), lambda i: (i, 0)),
        out_shape=jax.ShapeDtypeStruct((M, GATE_LANES), F32),
        compiler_params=_params(("parallel",), 4 * tm * D * 2 + 4 * D * GATE_LANES * 2),
        name="mlstm_gates",
    )(h, w_gates, bias)


def _group_mm_kernel(a_ref, w_ref, s_ref, z_ref, o_ref):
    mixed = jnp.dot(a_ref[...], w_ref[...], preferred_element_type=F32) * s_ref[...]
    o_ref[...] = (mixed * _silu(z_ref[...].astype(F32))).astype(o_ref.dtype)


def group_mix_gate(pooled, w_group, scale, proj, z_off, tm=512):
    M, D = pooled.shape
    G, Gs, _ = w_group.shape
    tm = min(tm, M)
    zb = z_off // Gs
    return pl.pallas_call(
        _group_mm_kernel,
        grid=(G, M // tm),
        in_specs=[pl.BlockSpec((tm, Gs), lambda g, i: (i, g)),
                  pl.BlockSpec((None, Gs, Gs), lambda g, i: (g, 0, 0)),
                  pl.BlockSpec((1, Gs), lambda g, i: (0, g)),
                  pl.BlockSpec((tm, Gs), lambda g, i: (i, zb + g))],
        out_specs=pl.BlockSpec((tm, Gs), lambda g, i: (i, g)),
        out_shape=jax.ShapeDtypeStruct((M, D), BF16),
        compiler_params=_params(("parallel", "parallel"), 6 * tm * Gs * 4 + 4 * Gs * Gs * 2),
        name="group_mix_gate",
    )(pooled, w_group, scale.reshape(1, D), proj)


def _window_sums(ext, levels):
    sums = []
    cur = ext
    for lv in range(levels):
        cur = cur + pltpu.roll(cur, 1 << lv, 0)
        sums.append(cur)
    return sums


def _pool_mix_prompt_kernel(u_ref, halo_ref, hist_ref, zlo_ref, zhi_ref, w_ref, s_ref, o_ref, *, pos0, tt):
    i = pl.program_id(1)
    ng = len(POOL_WINDOWS)
    gs = u_ref.shape[1] // ng
    row = lax.broadcasted_iota(jnp.int32, (tt, 1), 0) + i * tt
    pos1 = (row + (pos0 + 1)).astype(F32)
    first = i == 0
    for g, w in enumerate(POOL_WINDOWS):
        lo, hi = g * gs, (g + 1) * gs
        u = u_ref[:, lo:hi].astype(F32)
        halo = jnp.where(first, hist_ref[:, lo:hi], halo_ref[:, lo:hi].astype(F32))
        ext = jnp.concatenate([halo, u], axis=0)
        win = _window_sums(ext, g + 1)[-1][HALO:, :]
        inv_cnt = 1.0 / jnp.minimum(float(w), pos1)
        pooled = (win * inv_cnt - u).astype(BF16)
        mixed = jnp.dot(pooled, w_ref[g], preferred_element_type=F32) * s_ref[:, lo:hi]
        z_ref, zg = (zlo_ref, g) if g < ng // 2 else (zhi_ref, g - ng // 2)
        z = z_ref[:, zg * gs:(zg + 1) * gs].astype(F32)
        o_ref[:, lo:hi] = (mixed * _silu(z)).astype(o_ref.dtype)


def pool_mix_prompt(proj, hist, w_group, scale, batch, pos0, z_off, tt=512):
    D = hist.shape[-1]
    G, Gs, _ = w_group.shape
    T = proj.shape[0] // batch
    tt = min(tt, T)
    nt = T // tt
    dh = D // 2
    assert z_off % dh == 0 and G == len(POOL_WINDOWS)
    zb = z_off // dh
    hist16 = jnp.concatenate([jnp.zeros((batch, 1, D), F32), hist.astype(F32)], axis=1)
    vmem = 2 * (3 * tt * D * 2 + G * Gs * Gs * 2) + 12 * (tt + HALO) * Gs * 4
    return pl.pallas_call(
        functools.partial(_pool_mix_prompt_kernel, pos0=pos0, tt=tt),
        grid=(batch, nt),
        in_specs=[pl.BlockSpec((tt, D), lambda b, i: (b * nt + i, 0)),
                  pl.BlockSpec((HALO, D),
                               lambda b, i: (jnp.maximum((b * nt + i) * (tt // HALO) - 1, 0), 0)),
                  pl.BlockSpec((None, HALO, D), lambda b, i: (b, 0, 0)),
                  pl.BlockSpec((tt, dh), lambda b, i: (b * nt + i, zb)),
                  pl.BlockSpec((tt, dh), lambda b, i: (b * nt + i, zb + 1)),
                  pl.BlockSpec((G, Gs, Gs), lambda b, i: (0, 0, 0)),
                  pl.BlockSpec((1, D), lambda b, i: (0, 0))],
        out_specs=pl.BlockSpec((tt, D), lambda b, i: (b * nt + i, 0)),
        out_shape=jax.ShapeDtypeStruct((batch * T, D), BF16),
        compiler_params=_params(("parallel", "parallel"), vmem + (4 << 20)),
        name="pool_mix_prompt",
    )(proj, proj, hist16, proj, proj, w_group, scale.reshape(1, D))


def _pool_sample_kernel(hist_ref, u_ref, o_ref, new_hist_ref, *, pos0):
    steps = u_ref.shape[0]
    gs = u_ref.shape[2] // len(POOL_WINDOWS)

    def row(r, lo, hi):
        if r < POOL_HIST:
            return hist_ref[r, :, lo:hi]
        return u_ref[r - POOL_HIST, :, lo:hi].astype(F32)

    for g, w in enumerate(POOL_WINDOWS):
        lo, hi = g * gs, (g + 1) * gs
        for t in range(steps):
            cur = row(POOL_HIST + t, lo, hi)
            win = cur
            for r in range(1, w):
                win = win + row(POOL_HIST + t - r, lo, hi)
            cnt = float(min(w, pos0 + t + 1))
            o_ref[t, :, lo:hi] = (win / cnt - cur).astype(o_ref.dtype)
    for r in range(POOL_HIST):
        new_hist_ref[r] = row(r + steps, 0, u_ref.shape[2])


def pool_sample(hist_t, u_t, pos0, bt=32):
    T, B, D = u_t.shape
    bt = min(bt, B)
    return pl.pallas_call(
        functools.partial(_pool_sample_kernel, pos0=pos0),
        grid=(B // bt,),
        in_specs=[pl.BlockSpec((POOL_HIST, bt, D), lambda i: (0, i, 0)),
                  pl.BlockSpec((T, bt, D), lambda i: (0, i, 0))],
        out_specs=[pl.BlockSpec((T, bt, D), lambda i: (0, i, 0)),
                   pl.BlockSpec((POOL_HIST, bt, D), lambda i: (0, i, 0))],
        out_shape=[jax.ShapeDtypeStruct((T, B, D), BF16), jax.ShapeDtypeStruct((POOL_HIST, B, D), F32)],
        compiler_params=_params(("parallel",), 5 * (POOL_HIST + T) * bt * D * 4),
        name="pool_sample",
    )(hist_t, u_t)


def _softmax_rows(qk, scale):
    m = jnp.max(qk, axis=-1, keepdims=True)
    p = jnp.exp2((qk - m) * (scale * LOG2E))
    return p, jnp.sum(p, axis=-1, keepdims=True)


def _attn_prompt_kernel(q_ref, k_ref, v_ref, z_ref, o_ref):
    hd = q_ref.shape[1] // XATT_HEADS
    scale = hd ** -0.5
    cols = [(h * hd, (h + 1) * hd) for h in range(XATT_HEADS)]
    scores = [lax.dot_general(q_ref[:, lo:hi], k_ref[:, lo:hi].astype(BF16), (((1,), (1,)), ((), ())),
                              preferred_element_type=F32) for lo, hi in cols]
    probs = [_softmax_rows(s, scale) for s in scores]
    outs = [jnp.dot(p.astype(BF16), v_ref[:, lo:hi].astype(BF16), preferred_element_type=F32) / l
            for (lo, hi), (p, l) in zip(cols, probs)]
    for (lo, hi), o in zip(cols, outs):
        o_ref[:, lo:hi] = (o * _silu(z_ref[:, lo:hi].astype(F32))).astype(o_ref.dtype)


def attend_prompt(q_arr, qb, z_arr, zb, k, v, layer, S, tq=512):
    _, BS, DX = k.shape
    B = BS // S
    T = q_arr.shape[0] // B
    tq = min(tq, T)
    nq = T // tq
    return pl.pallas_call(
        _attn_prompt_kernel,
        grid=(B, nq),
        in_specs=[pl.BlockSpec((tq, DX), lambda b, i: (b * nq + i, qb)),
                  pl.BlockSpec((None, S, DX), lambda b, i: (layer, b, 0)),
                  pl.BlockSpec((None, S, DX), lambda b, i: (layer, b, 0)),
                  pl.BlockSpec((tq, DX), lambda b, i: (b * nq + i, zb))],
        out_specs=pl.BlockSpec((tq, DX), lambda b, i: (b * nq + i, 0)),
        out_shape=jax.ShapeDtypeStruct((B * T, DX), BF16),
        compiler_params=_params(("parallel", "parallel"), 8 * tq * DX * 4 + 8 * S * DX * 4),
        name="attend_prompt",
    )(q_arr, k, v, z_arr)


def _attn_sample_kernel(q_ref, k_hbm, v_hbm, z_ref, o_ref, kbuf, vbuf, sem, *, steps, layer, bb):
    i = pl.program_id(0)
    rows, dx = q_ref.shape
    hd = dx // XATT_HEADS
    scale = hd ** -0.5

    def head_copies(step, slot):
        out = []
        for t, (src, dst) in enumerate(((k_hbm, kbuf), (v_hbm, vbuf))):
            for h in range(XATT_HEADS):
                out.append(pltpu.make_async_copy(src.at[layer, pl.ds(step * bb, bb), :, h, :],
                                                 dst.at[slot, h], sem.at[t, slot, h]))
        return out

    @pl.when(i == 0)
    def _():
        for cp in head_copies(0, 0):
            cp.start()

    @pl.when(i + 1 < pl.num_programs(0))
    def _():
        for cp in head_copies(i + 1, (i + 1) % 2):
            cp.start()

    slot = i % 2
    for cp in head_copies(i, slot):
        cp.wait()

    row_batch = lax.broadcasted_iota(jnp.int32, (rows, 1), 0) // steps
    pairs = [(h, j) for h in range(XATT_HEADS) for j in range(bb)]
    scores = [lax.dot_general(q_ref[:, h * hd:(h + 1) * hd], kbuf[slot, h, j].astype(BF16),
                              (((1,), (1,)), ((), ())), preferred_element_type=F32)
              for h, j in pairs]
    probs = [_softmax_rows(s, scale) for s in scores]
    outs = [jnp.dot(p.astype(BF16), vbuf[slot, h, j].astype(BF16), preferred_element_type=F32) / l
            for (h, j), (p, l) in zip(pairs, probs)]
    for h in range(XATT_HEADS):
        lo, hi = h * hd, (h + 1) * hd
        acc = jnp.zeros((rows, hd), F32)
        for j in range(bb):
            acc = jnp.where(row_batch == j, outs[h * bb + j], acc)
        o_ref[:, lo:hi] = (acc * _silu(z_ref[:, lo:hi].astype(F32))).astype(o_ref.dtype)


def attend_sample(q_arr, qb, z_arr, zb, k, v, layer, steps, bb=4):
    _, B, S, nh, hd = k.shape
    assert nh == XATT_HEADS
    DX = nh * hd
    rows = bb * steps
    slot_bytes = nh * bb * S * hd * 4
    return pl.pallas_call(
        functools.partial(_attn_sample_kernel, steps=steps, layer=layer, bb=bb),
        grid=(B // bb,),
        in_specs=[pl.BlockSpec((rows, DX), lambda i: (i, qb)),
                  pl.BlockSpec(memory_space=pl.ANY), pl.BlockSpec(memory_space=pl.ANY),
                  pl.BlockSpec((rows, DX), lambda i: (i, zb))],
        out_specs=pl.BlockSpec((rows, DX), lambda i: (i, 0)),
        out_shape=jax.ShapeDtypeStruct((B * steps, DX), BF16),
        scratch_shapes=[pltpu.VMEM((2, nh, bb, S, hd), F32), pltpu.VMEM((2, nh, bb, S, hd), F32),
                        pltpu.SemaphoreType.DMA((2, 2, nh))],
        compiler_params=_params(("arbitrary",), 4 * slot_bytes + (12 << 20)),
        name="attend_sample",
    )(q_arr, k, v, z_arr)


def _head_output(num, inv_den, o, z, g):
    ms = jnp.mean(num * num, axis=-1, keepdims=True) * (inv_den * inv_den)
    row_scale = inv_den * lax.rsqrt(ms + NORM_EPS)
    return (num * row_scale) * (g * (_sigmoid(o) * _silu(z)))


def _mlstm_prompt_kernel(q_ref, k_ref, v_ref, o_ref, zlo_ref, zhi_ref, g_ref, gh_ref,
                         y_ref, c_out, n_out, m_out, ct_scr, n_scr, m_scr):
    c = pl.program_id(1)
    nh = MLSTM_HEADS
    L = q_ref.shape[0]
    dqk = q_ref.shape[1] // nh
    dv = v_ref.shape[1] // nh
    qscale = dqk ** -0.5

    @pl.when(c == 0)
    def _():
        ct_scr[...] = jnp.zeros_like(ct_scr)
        n_scr[...] = jnp.zeros_like(n_scr)
        m_scr[...] = jnp.zeros_like(m_scr)

    r_i = lax.broadcasted_iota(jnp.int32, (L, L), 0)
    c_i = lax.broadcasted_iota(jnp.int32, (L, L), 1)
    causal = c_i <= r_i
    gates = g_ref[...]

    tril16 = jnp.where(causal, 1.0, 0.0).astype(BF16)
    g_hi = gates.astype(BF16)
    rest = gates - g_hi.astype(F32)
    g_mid = rest.astype(BF16)
    g_lo = (rest - g_mid.astype(F32)).astype(BF16)
    cums = (jnp.dot(tril16, g_hi, preferred_element_type=F32)
            + jnp.dot(tril16, g_mid, preferred_element_type=F32)
            + jnp.dot(tril16, g_lo, preferred_element_type=F32))
    gates_t = gates.T
    cums_t = cums.T

    def decay_weights(h):
        m_prev = m_scr[h:h + 1, 0:1]
        b_col = cums[:, nh + h:nh + h + 1]
        rel_row = gates_t[h:h + 1, :] - cums_t[nh + h:nh + h + 1, :]
        b2 = b_col * LOG2E
        d2 = jnp.where(causal, b2 + rel_row * LOG2E, -jnp.inf)
        inter2 = b2 + m_prev * LOG2E
        m2 = jnp.maximum(inter2, jnp.max(d2, axis=1, keepdims=True))
        return dict(m_prev=m_prev, b_col=b_col, m2=m2, m_t=m2 * LN2,
                    w=jnp.exp2(d2 - m2), w_inter=jnp.exp2(inter2 - m2))

    def scores(h):
        q = q_ref[:, h * dqk:(h + 1) * dqk].astype(F32) * qscale
        k = k_ref[:, h * dqk:(h + 1) * dqk]
        return q, lax.dot_general(q.astype(BF16), k, (((1,), (1,)), ((), ())), preferred_element_type=F32)

    def read_out(h, st, q, qk):
        v = v_ref[:, h * dv:(h + 1) * dv]
        s = qk * st["w"]
        qw = q * st["w_inter"]
        num = (jnp.dot(qw.astype(BF16), ct_scr[h].astype(BF16), preferred_element_type=F32)
               + jnp.dot(s.astype(BF16), v, preferred_element_type=F32))
        den = (jnp.sum(s, axis=1, keepdims=True)
               + jnp.sum(qw * n_scr[h:h + 1, :], axis=1, keepdims=True))
        return num, 1.0 / jnp.maximum(jnp.abs(den), jnp.exp2(-st["m2"]))

    def update_state(h, st):
        k = k_ref[:, h * dqk:(h + 1) * dqk]
        v = v_ref[:, h * dv:(h + 1) * dv]
        b_col, m_prev = st["b_col"], st["m_prev"]
        b_last = b_col[L - 1:L, :]
        m_new = st["m_t"][L - 1:L, :]
        w_state = jnp.exp(b_last - b_col + gates[:, h:h + 1] - m_new)
        decay = jnp.exp(b_last + m_prev - m_new)
        kw = k.astype(F32) * w_state
        ct_scr[h] = decay * ct_scr[h] + lax.dot_general(kw.astype(BF16), v, (((0,), (0,)), ((), ())),
                                                        preferred_element_type=F32)
        n_scr[h:h + 1, :] = decay * n_scr[h:h + 1, :] + jnp.sum(kw, axis=0, keepdims=True)
        m_scr[h:h + 1, :] = jnp.broadcast_to(m_new, (1, m_scr.shape[1]))

    def write_out(h, num, inv_den):
        lo, hi = h * dv, (h + 1) * dv
        z_ref, zh = (zlo_ref, h) if h < nh // 2 else (zhi_ref, h - nh // 2)
        y_ref[:, lo:hi] = _head_output(num, inv_den, o_ref[:, lo:hi].astype(F32),
                                       z_ref[:, zh * dv:(zh + 1) * dv].astype(F32),
                                       gh_ref[:, lo:hi]).astype(y_ref.dtype)

    for h0 in range(0, nh, MLSTM_HEAD_GROUP):
        group = range(h0, h0 + MLSTM_HEAD_GROUP)
        qks = {h: scores(h) for h in group}
        stats = {h: decay_weights(h) for h in group}
        outs = {h: read_out(h, stats[h], *qks[h]) for h in group}
        for h in group:
            update_state(h, stats[h])
        for h in group:
            write_out(h, *outs[h])

    @pl.when(c == pl.num_programs(1) - 1)
    def _():
        for h in range(nh):
            c_out[h] = ct_scr[h].T
        n_out[...] = n_scr[...]
        m_out[...] = m_scr[...]


def mlstm_prompt(proj, proj_z, gates, g_head, batch, dqk, dv):
    nh = MLSTM_HEADS
    T = proj.shape[0] // batch
    L = min(MLSTM_CHUNK, T)
    nc = T // L
    wq, wv = nh * dqk, nh * dv
    assert wv == 2 * wq
    rows = lambda b, c: b * nc + c
    return pl.pallas_call(
        _mlstm_prompt_kernel,
        grid=(batch, nc),
        in_specs=[pl.BlockSpec((L, wq), lambda b, c: (rows(b, c), 0)),
                  pl.BlockSpec((L, wq), lambda b, c: (rows(b, c), 1)),
                  pl.BlockSpec((L, wv), lambda b, c: (rows(b, c), 1)),
                  pl.BlockSpec((L, wv), lambda b, c: (rows(b, c), 2)),
                  pl.BlockSpec((L, wq), lambda b, c: (rows(b, c), 1)),
                  pl.BlockSpec((L, wq), lambda b, c: (rows(b, c), 2)),
                  pl.BlockSpec((L, GATE_LANES), lambda b, c: (rows(b, c), 0)),
                  pl.BlockSpec((1, wv), lambda b, c: (0, 0))],
        out_specs=[pl.BlockSpec((L, wv), lambda b, c: (rows(b, c), 0)),
                   pl.BlockSpec((None, nh, dv, dqk), lambda b, c: (b, 0, 0, 0)),
                   pl.BlockSpec((None, nh, dqk), lambda b, c: (b, 0, 0)),
                   pl.BlockSpec((None, nh, GATE_LANES), lambda b, c: (b, 0, 0))],
        out_shape=[jax.ShapeDtypeStruct((batch * T, wv), BF16),
                   jax.ShapeDtypeStruct((batch, nh, dv, dqk), F32),
                   jax.ShapeDtypeStruct((batch, nh, dqk), F32),
                   jax.ShapeDtypeStruct((batch, nh, GATE_LANES), F32)],
        scratch_shapes=[pltpu.VMEM((nh, dqk, dv), F32),
                        pltpu.VMEM((nh, dqk), F32),
                        pltpu.VMEM((nh, GATE_LANES), F32)],
        compiler_params=_params(("parallel", "arbitrary"),
                                24 * L * wv + 5 * nh * dv * dqk * 4 + (16 << 20)),
        name="mlstm_prompt",
    )(proj, proj, proj, proj, proj_z, proj_z, gates, g_head.reshape(1, wv))


def _lane_pick(x, lane, idx):
    return jnp.sum(jnp.where(lane == idx, x, 0.0), axis=1, keepdims=True)


def _mlstm_sample_kernel(q_ref, k_ref, v_ref, o_ref, z_ref, g_ref, gh_ref, c_ref, n_ref,
                         y_ref, c_out, n_out, m_out, *, steps):
    head = pl.program_id(1)
    nh = MLSTM_HEADS
    R, dqk = q_ref.shape
    nb = R // steps
    qscale = dqk ** -0.5

    gates = g_ref[...]
    lane = lax.broadcasted_iota(jnp.int32, gates.shape, 1)
    ig = _lane_pick(gates, lane, head)
    lf = _lane_pick(gates, lane, nh + head)
    m_prev = _lane_pick(gates, lane, 2 * nh + head)

    r_i = lax.broadcasted_iota(jnp.int32, (R, R), 0)
    c_i = lax.broadcasted_iota(jnp.int32, (R, R), 1)
    same = (r_i // steps) == (c_i // steps)
    causal = same & (c_i <= r_i)
    eye = (c_i == r_i).astype(F32)
    lower = causal.astype(F32)
    upper = (same & (r_i <= c_i)).astype(F32)
    last_of_row_batch = (c_i == (r_i // steps) * steps + (steps - 1)).astype(F32)
    row_batch = lax.broadcasted_iota(jnp.int32, (R, 1), 0) // steps

    q = q_ref[...].astype(F32) * qscale
    qb = q.astype(BF16)
    k = k_ref[...]
    v = v_ref[...]

    lf_row = jnp.sum(lf * eye, axis=0, keepdims=True)
    ig_row = jnp.sum(ig * eye, axis=0, keepdims=True)
    b_col = jnp.sum(lower * lf_row, axis=1, keepdims=True)
    b_row = jnp.sum(upper * lf, axis=0, keepdims=True)
    d = jnp.where(causal, b_col - b_row + ig_row, -jnp.inf)
    inter = b_col + m_prev
    m_t = jnp.maximum(inter, jnp.max(d, axis=1, keepdims=True))
    w = jnp.exp(d - m_t)
    w_inter = jnp.exp(inter - m_t)
    s = lax.dot_general(qb, k, (((1,), (1,)), ((), ())), preferred_element_type=F32) * w

    qc = jnp.zeros((R, v_ref.shape[1]), F32)
    n_rows = jnp.zeros((R, dqk), F32)
    for j in range(nb):
        mine = row_batch == j
        qcj = lax.dot_general(qb, c_ref[j].astype(BF16), (((1,), (1,)), ((), ())),
                              preferred_element_type=F32)
        qc = jnp.where(mine, qcj, qc)
        n_rows = jnp.where(mine, n_ref[j:j + 1, :], n_rows)

    num = jnp.dot(s.astype(BF16), v, preferred_element_type=F32) + w_inter * qc
    den = jnp.sum(s, axis=1, keepdims=True) + w_inter * jnp.sum(q * n_rows, axis=1, keepdims=True)
    inv_den = 1.0 / jnp.maximum(jnp.abs(den), jnp.exp(-m_t))
    y_ref[...] = _head_output(num, inv_den, o_ref[...].astype(F32), z_ref[...].astype(F32),
                              gh_ref[...]).astype(y_ref.dtype)

    m_row = jnp.sum(m_t * eye, axis=0, keepdims=True)
    b_last = jnp.sum(last_of_row_batch * b_row, axis=1, keepdims=True)
    m_new = jnp.sum(last_of_row_batch * m_row, axis=1, keepdims=True)
    w_state = jnp.exp(b_last - b_col + ig - m_new)
    decay = jnp.exp(b_last + m_prev - m_new)
    vw = v.astype(F32) * w_state
    kw = k.astype(F32) * w_state
    for j in range(nb):
        mine = row_batch == j
        r0 = j * steps
        dj = decay[r0:r0 + 1, :]
        upd = lax.dot_general(jnp.where(mine, vw, 0.0).astype(BF16), k, (((0,), (0,)), ((), ())),
                              preferred_element_type=F32)
        c_out[j] = dj * c_ref[j] + upd
        n_out[j:j + 1, :] = dj * n_ref[j:j + 1, :] + jnp.sum(jnp.where(mine, kw, 0.0), axis=0, keepdims=True)
        m_out[j:j + 1, :] = jnp.broadcast_to(m_new[r0:r0 + 1, :], (1, m_out.shape[1]))


def mlstm_sample(proj, proj_z, gates, g_head, c0, n0, steps, dqk, dv):
    nh = MLSTM_HEADS
    B = c0.shape[0]
    nb = SAMPLE_BATCH_BLOCK
    R = nb * steps
    kq = nh
    kv = (2 * nh * dqk) // dv
    ko = kv + nh
    kz = (nh * dqk) // dv
    return pl.pallas_call(
        functools.partial(_mlstm_sample_kernel, steps=steps),
        grid=(B // nb, nh),
        in_specs=[pl.BlockSpec((R, dqk), lambda i, h: (i, h)),
                  pl.BlockSpec((R, dqk), lambda i, h: (i, kq + h)),
                  pl.BlockSpec((R, dv), lambda i, h: (i, kv + h)),
                  pl.BlockSpec((R, dv), lambda i, h: (i, ko + h)),
                  pl.BlockSpec((R, dv), lambda i, h: (i, kz + h)),
                  pl.BlockSpec((R, GATE_LANES), lambda i, h: (i, 0)),
                  pl.BlockSpec((1, dv), lambda i, h: (0, h)),
                  pl.BlockSpec((nb, None, dv, dqk), lambda i, h: (i, h, 0, 0)),
                  pl.BlockSpec((None, nb, dqk), lambda i, h: (h, i, 0))],
        out_specs=[pl.BlockSpec((R, dv), lambda i, h: (i, h)),
                   pl.BlockSpec((nb, None, dv, dqk), lambda i, h: (i, h, 0, 0)),
                   pl.BlockSpec((None, nb, dqk), lambda i, h: (h, i, 0)),
                   pl.BlockSpec((None, nb, GATE_LANES), lambda i, h: (h, i, 0))],
        out_shape=[jax.ShapeDtypeStruct((B * steps, nh * dv), BF16),
                   jax.ShapeDtypeStruct(c0.shape, F32),
                   jax.ShapeDtypeStruct((nh, B, dqk), F32),
                   jax.ShapeDtypeStruct((nh, B, GATE_LANES), F32)],
        compiler_params=_params(("parallel", "parallel"), 6 * nb * dv * dqk * 4 + (8 << 20)),
        name="mlstm_sample",
    )(proj, proj, proj, proj, proj_z, gates, g_head.reshape(1, nh * dv), c0, n0)


def _finish_casting(x, h_mix, h_att, w_out, layer, g_post, g_next):
    y, w16 = matmul_cast([h_mix, h_att], w_out, layer, BF16, tm=512, tn=256)
    return post_norm_residual(x, y, g_post, g_next) + (w16,)


def _finish(x, h_mix, h_att, w16, g_post, g_next):
    y = matmul([h_mix, h_att], w16[None], BF16, tm=512, tn=1024)
    return post_norm_residual(x, y, g_post, g_next)


def kernel(x_prompt, x_sample, mem_prompt, state_pool, state_mlstm_C, state_mlstm_n, state_mlstm_m,
           cache_mem_k, cache_mem_v, norm_pre, norm_post, norm_mem, w_mem_kv, w_out,
           w_in_pool, w_pool_group, pool_scale, w_in_mlstm, b_igate, b_fgate, mlstm_head_norm):
    B, T, D = x_prompt.shape
    Bs, Ts, _ = x_sample.shape
    S = mem_prompt.shape[1]
    depth = w_out.shape[0]
    nh = MLSTM_HEADS
    dx = D // 2
    dv = D // nh
    dqk = dv // 2

    xp = x_prompt.reshape(B * T, D)
    xs = x_sample.reshape(Bs * Ts, D)
    mem = mem_prompt.reshape(B * S, D)
    hp = rmsnorm_cast(xp, norm_pre[:1])[0]
    hs = rmsnorm_cast(xs, norm_pre[:1])[0]

    h_mem = rmsnorm_cast(mem, norm_mem)
    mem_k, mem_k_heads = mem_proj(h_mem, w_mem_kv, 0, B, XATT_HEADS)
    mem_v, mem_v_heads = mem_proj(h_mem, w_mem_kv, dx, B, XATT_HEADS)
    w_mlstm_t = jnp.swapaxes(w_in_mlstm, 1, 2)

    pool_p_l, pool_s_l = [], []
    cp_l, np_l, mp_l, cs_l, ns_l, ms_l = [], [], [], [], [], []
    for layer in range(depth):
        j = layer // 2
        g_next = norm_pre[layer + 1] if layer + 1 < depth else None
        if layer % 2 == 0:
            w_g = w_pool_group[j].astype(BF16)
            z_mix, qb, zb = D + dx, D // dx, (2 * D + dx) // dx
            proj, w_pool16 = matmul_cast([hs], w_in_pool, j, BF16)
            u_t = proj[:, :D].reshape(Bs, Ts, D).transpose(1, 0, 2)
            pooled, hist_t = pool_sample(state_pool[j].transpose(1, 0, 2), u_t, PAST_LEN)
            pooled = pooled.transpose(1, 0, 2).reshape(Bs * Ts, D)
            mix = group_mix_gate(pooled, w_g, pool_scale[j], proj, z_mix)
            att = attend_sample(proj, qb, proj, zb, cache_mem_k, cache_mem_v, layer, Ts)
            xs, hs, w_o16 = _finish_casting(xs, mix, att, w_out, layer, norm_post[layer], g_next)
            pool_s_l.append(hist_t.transpose(1, 0, 2))
            proj = matmul([hp], w_pool16[None], BF16)
            mix = pool_mix_prompt(proj, jnp.zeros((B, POOL_HIST, D), F32), w_g, pool_scale[j], B, 0, z_mix)
            att = attend_prompt(proj, qb, proj, zb, mem_k, mem_v, layer, S)
            xp, hp = _finish(xp, mix, att, w_o16, norm_post[layer], g_next)
            pool_p_l.append(proj.reshape(B, T, -1)[:, T - POOL_HIST:, :D].astype(F32))
        else:
            bias = jnp.pad(jnp.concatenate([b_igate[j], b_fgate[j]]), (0, GATE_LANES - 2 * nh))
            bias = bias.reshape(1, GATE_LANES)
            qb, zb = 0, (dx + D) // dx
            g0 = 3 * D
            w_gates = jnp.pad(w_mlstm_t[j, g0:g0 + 2 * nh, :], ((0, GATE_LANES - 2 * nh), (0, 0))).astype(BF16)
            proj, w_main = matmul_cast([hs], w_mlstm_t, j, BF16, 0, g0, transposed=True)
            proj_z, w_z = matmul_cast([hs], w_mlstm_t, j, BF16, g0 + 2 * nh, None, transposed=True)
            gates = mlstm_gates(hs, w_gates, bias)
            m_rows = jnp.repeat(state_mlstm_m[j], Ts, axis=0)
            gates = jnp.concatenate([gates[:, :2 * nh], m_rows,
                                     jnp.zeros((Bs * Ts, GATE_LANES - 3 * nh), F32)], axis=1)
            mix, c1, n1, m1 = mlstm_sample(proj, proj_z, gates, mlstm_head_norm[j], state_mlstm_C[j],
                                           state_mlstm_n[j].transpose(1, 0, 2), Ts, dqk, dv)
            att = attend_sample(proj_z, qb, proj_z, zb, cache_mem_k, cache_mem_v, layer, Ts)
            xs, hs, w_o16 = _finish_casting(xs, mix, att, w_out, layer, norm_post[layer], g_next)
            cs_l.append(c1); ns_l.append(n1.transpose(1, 0, 2)); ms_l.append(m1[:, :, 0].T)
            proj = matmul_nt(hp, w_main, BF16)
            proj_z = matmul_nt(hp, w_z, BF16)
            gates = mlstm_gates(hp, w_gates, bias)
            mix, c1, n1, m1 = mlstm_prompt(proj, proj_z, gates, mlstm_head_norm[j], B, dqk, dv)
            att = attend_prompt(proj_z, qb, proj_z, zb, mem_k, mem_v, layer, S)
            xp, hp = _finish(xp, mix, att, w_o16, norm_post[layer], g_next)
            cp_l.append(c1); np_l.append(n1); mp_l.append(m1[:, :, 0])

    return (xp.reshape(B, T, D), xs.reshape(Bs, Ts, D), jnp.stack(pool_p_l),
            mem_k_heads, mem_v_heads,
            jnp.stack(cp_l), jnp.stack(np_l), jnp.stack(mp_l),
            jnp.stack(pool_s_l), jnp.stack(cs_l), jnp.stack(ns_l), jnp.stack(ms_l))
```

```python
import functools

import jax
import jax.numpy as jnp
from jax import lax
from jax.experimental import pallas as pl
from jax.experimental.pallas import tpu as pltpu

F32 = jnp.float32
BF16 = jnp.bfloat16

NORM_EPS = 1e-6
GATE_SOFTCAP = 15.0
POOL_WINDOWS = (2, 4, 8, 16)
POOL_HIST = max(POOL_WINDOWS) - 1
HALO = POOL_HIST + 1
XATT_HEADS = 4
MLSTM_HEADS = 8
PAST_LEN = 16384
GATE_LANES = 128
MLSTM_CHUNK = 256
MLSTM_HEAD_GROUP = 2
SAMPLE_BATCH_BLOCK = 16

V7X_VMEM_BYTES = 64 * 1024 * 1024
VMEM_CAP = V7X_VMEM_BYTES - 8 * 1024 * 1024


def _params(semantics, vmem_bytes):
    limit = int(min(max(vmem_bytes, 16 * 1024 * 1024), VMEM_CAP))
    return pltpu.CompilerParams(dimension_semantics=semantics, vmem_limit_bytes=limit)


LOG2E = 1.4426950408889634
LN2 = 0.6931471805599453
NEG_LOG2E = -LOG2E


def _sigmoid(x):
    return 1.0 / (1.0 + jnp.exp2(x * NEG_LOG2E))


def _silu(x):
    return x * _sigmoid(x)


def _rms(x, g):
    r = lax.rsqrt(jnp.mean(x * x, axis=-1, keepdims=True) + NORM_EPS)
    return x * r * g


def _rmsnorm_kernel(x_ref, g_ref, o_ref):
    o_ref[...] = _rms(x_ref[...], g_ref[...]).astype(o_ref.dtype)


ROWWISE_MIN_STEPS = 8


def rmsnorm_cast(x, g, tm=512):
    M, D = x.shape
    L = g.shape[0]
    tm = max(min(tm, M // ROWWISE_MIN_STEPS), BF16_ROW_TILE)
    return pl.pallas_call(
        _rmsnorm_kernel,
        grid=(L, M // tm),
        in_specs=[pl.BlockSpec((tm, D), lambda l, i: (i, 0)),
                  pl.BlockSpec((None, 1, D), lambda l, i: (l, 0, 0))],
        out_specs=pl.BlockSpec((None, tm, D), lambda l, i: (l, i, 0)),
        out_shape=jax.ShapeDtypeStruct((L, M, D), BF16),
        compiler_params=_params(("parallel", "parallel"), 6 * tm * D * 4),
        name="rmsnorm_cast",
    )(x, g.reshape(L, 1, D))


def _post_kernel(*refs, with_next):
    x_ref, y_ref, g_ref = refs[:3]
    out = x_ref[...] + _rms(y_ref[...].astype(F32), g_ref[...])
    if with_next:
        gn_ref, o_ref, h_ref = refs[3:]
        h_ref[...] = _rms(out, gn_ref[...]).astype(h_ref.dtype)
    else:
        o_ref = refs[3]
    o_ref[...] = out


def post_norm_residual(x, y, g_post, g_next=None, tm=256):
    M, D = x.shape
    tm = min(tm, M)
    row = pl.BlockSpec((tm, D), lambda i: (i, 0))
    vec = pl.BlockSpec((1, D), lambda i: (0, 0))
    with_next = g_next is not None
    args = [x, y, g_post.reshape(1, D)] + ([g_next.reshape(1, D)] if with_next else [])
    out_shape = [jax.ShapeDtypeStruct((M, D), F32)] + ([jax.ShapeDtypeStruct((M, D), BF16)] if with_next else [])
    res = pl.pallas_call(
        functools.partial(_post_kernel, with_next=with_next),
        grid=(M // tm,),
        in_specs=[row, row, vec] + ([vec] if with_next else []),
        out_specs=[row] * len(out_shape),
        out_shape=out_shape,
        compiler_params=_params(("parallel",), 12 * tm * D * 4),
        name="post_norm_residual",
    )(*args)
    return (res[0], res[1]) if with_next else (res[0], None)


BF16_ROW_TILE = 16


def _mm_kernel(*refs, ksizes):
    n_a = len(ksizes)
    a_refs, w_ref, o_ref = refs[:n_a], refs[n_a], refs[n_a + 1]
    acc = None
    off = 0
    for a_ref, ks in zip(a_refs, ksizes):
        part = jnp.dot(a_ref[...], w_ref[off:off + ks, :], preferred_element_type=F32)
        acc = part if acc is None else acc + part
        off += ks
    o_ref[...] = acc.astype(o_ref.dtype)


def matmul(a_list, w, out_dtype, layer=0, tm=1024, tn=1024):
    M = a_list[0].shape[0]
    _, K, N = w.shape
    ksizes = tuple(a.shape[1] for a in a_list)
    assert sum(ksizes) == K
    tm, tn = min(tm, M), min(tn, N)
    osz = jnp.dtype(out_dtype).itemsize
    vmem = 2 * (tm * K * 2 + K * tn * 2 + tm * tn * osz) + 2 * tm * tn * 4
    return pl.pallas_call(
        functools.partial(_mm_kernel, ksizes=ksizes),
        grid=(N // tn, M // tm),
        in_specs=[pl.BlockSpec((tm, ks), lambda j, i: (i, 0)) for ks in ksizes]
        + [pl.BlockSpec((None, K, tn), lambda j, i: (layer, 0, j))],
        out_specs=pl.BlockSpec((tm, tn), lambda j, i: (i, j)),
        out_shape=jax.ShapeDtypeStruct((M, N), out_dtype),
        compiler_params=_params(("parallel", "parallel"), vmem + (4 << 20)),
        name="matmul",
    )(*a_list, w)


def _mm_nt_kernel(a_ref, wt_ref, o_ref):
    o_ref[...] = lax.dot_general(a_ref[...], wt_ref[...], (((1,), (1,)), ((), ())),
                                 preferred_element_type=F32).astype(o_ref.dtype)


def matmul_nt(a, wt, out_dtype, tm=1024, tn=1024):
    M, K = a.shape
    N, _ = wt.shape
    tm, tn = min(tm, M), min(tn, N)
    osz = jnp.dtype(out_dtype).itemsize
    vmem = 2 * (tm * K * 2 + K * tn * 2 + tm * tn * osz) + 2 * tm * tn * 4
    return pl.pallas_call(
        _mm_nt_kernel,
        grid=(N // tn, M // tm),
        in_specs=[pl.BlockSpec((tm, K), lambda j, i: (i, 0)),
                  pl.BlockSpec((tn, K), lambda j, i: (j, 0))],
        out_specs=pl.BlockSpec((tm, tn), lambda j, i: (i, j)),
        out_shape=jax.ShapeDtypeStruct((M, N), out_dtype),
        compiler_params=_params(("parallel", "parallel"), vmem + (4 << 20)),
        name="matmul_nt",
    )(a, wt)


def _mm_castw_kernel(*refs, ksizes, shift, transposed):
    n_a = len(ksizes)
    a_refs = refs[:n_a]
    if shift:
        w_ref, e_ref, o_ref, wbf_ref = refs[n_a:]
    else:
        w_ref, o_ref, wbf_ref = refs[n_a:]

    @pl.when(pl.program_id(1) == 0)
    def _():
        if shift:
            keep = w_ref.shape[0] - shift
            wbf_ref[0:keep, :] = w_ref[shift:, :].astype(wbf_ref.dtype)
            wbf_ref[keep:, :] = e_ref[...].astype(wbf_ref.dtype)
        else:
            wbf_ref[...] = w_ref[...].astype(wbf_ref.dtype)

    if transposed:
        acc = lax.dot_general(a_refs[0][...], wbf_ref[...], (((1,), (1,)), ((), ())), preferred_element_type=F32)
    else:
        acc, off = None, 0
        for a_ref, ks in zip(a_refs, ksizes):
            part = jnp.dot(a_ref[...], wbf_ref[off:off + ks, :], preferred_element_type=F32)
            acc = part if acc is None else acc + part
            off += ks
    o_ref[...] = acc.astype(o_ref.dtype)


def matmul_cast(a_list, w, layer, out_dtype, n0=0, n=None, transposed=False, tm=1024, tn=512):
    M = a_list[0].shape[0]
    ksizes = tuple(a.shape[1] for a in a_list)
    K = sum(ksizes)
    assert not transposed or len(a_list) == 1
    n_all = w.shape[1] if transposed else w.shape[2]
    n = n_all - n0 if n is None else n
    tm, tn = min(tm, M), min(tn, n)
    shift = n0 % tn
    base = n0 - shift
    assert n % tn == 0 and M % tm == 0
    osz = jnp.dtype(out_dtype).itemsize
    if transposed:
        w_specs = [pl.BlockSpec((None, tn, K), lambda j, i: (layer, base // tn + j, 0))]
        if shift:
            assert shift % BF16_ROW_TILE == 0 and tn % shift == 0 and base % shift == 0
            w_specs.append(pl.BlockSpec((None, shift, K), lambda j, i: (layer, (base + (j + 1) * tn) // shift, 0)))
        wbf_spec = pl.BlockSpec((tn, K), lambda j, i: (j, 0))
        wbf_shape = (n, K)
    else:
        assert shift == 0
        w_specs = [pl.BlockSpec((None, K, tn), lambda j, i: (layer, 0, base // tn + j))]
        wbf_spec = pl.BlockSpec((K, tn), lambda j, i: (0, j))
        wbf_shape = (K, n)
    vmem = 2 * (tm * K * 2 + K * tn * 4 + K * tn * 2 + tm * tn * osz) + 2 * tm * tn * 4 + 2 * K * tn * 2
    return pl.pallas_call(
        functools.partial(_mm_castw_kernel, ksizes=ksizes, shift=shift, transposed=transposed),
        grid=(n // tn, M // tm),
        in_specs=[pl.BlockSpec((tm, ks), lambda j, i: (i, 0)) for ks in ksizes] + w_specs,
        out_specs=[pl.BlockSpec((tm, tn), lambda j, i: (i, j)), wbf_spec],
        out_shape=[jax.ShapeDtypeStruct((M, n), out_dtype), jax.ShapeDtypeStruct(wbf_shape, BF16)],
        compiler_params=_params(("parallel", "arbitrary"), vmem + (2 << 20)),
        name="matmul_cast",
    )(*a_list, *([w] * len(w_specs)))


def _mem_proj_kernel(a_ref, w_ref, o_ref, heads_hbm, wbf_scr, stage, sem, *, nheads, nb):
    jj, i = pl.program_id(0), pl.program_id(1)
    step = jj * pl.num_programs(1) + i
    last = pl.num_programs(0) * pl.num_programs(1) - 1
    slot = step % 2

    def head_copy(s):
        return pltpu.make_async_copy(stage.at[s], heads_hbm.at[jj // nheads, pl.ds(i * nb, nb), :, jj % nheads, :],
                                     sem.at[s])

    @pl.when(i == 0)
    def _():
        wbf_scr[...] = w_ref[...].astype(wbf_scr.dtype)

    kv = jnp.dot(a_ref[...], wbf_scr[...], preferred_element_type=F32)
    o_ref[...] = kv

    @pl.when(step >= 2)
    def _():
        head_copy(slot).wait()

    stage[slot] = kv.reshape(stage.shape[1:])
    head_copy(slot).start()

    @pl.when(step == last)
    def _():
        head_copy(slot).wait()

        @pl.when(step >= 1)
        def _():
            head_copy(1 - slot).wait()


def mem_proj(h, w_kv, n0, batch, nheads, tm=1024):
    depth, M, D = h.shape
    S = M // batch
    hd = (w_kv.shape[2] // 2) // nheads
    n = nheads * hd
    tm = min(tm, M)
    assert n0 % hd == 0 and M % tm == 0 and tm % S == 0
    nb = tm // S
    vmem = 2 * (tm * D * 2 + D * hd * 4 + tm * hd * 4) + 4 * tm * hd * 4 + 3 * D * hd * 2
    return pl.pallas_call(
        functools.partial(_mem_proj_kernel, nheads=nheads, nb=nb),
        grid=(depth * nheads, M // tm),
        in_specs=[pl.BlockSpec((None, tm, D), lambda jj, i: (jj // nheads, i, 0)),
                  pl.BlockSpec((None, D, hd), lambda jj, i: (jj // nheads, 0, n0 // hd + jj % nheads))],
        out_specs=[pl.BlockSpec((None, tm, hd), lambda jj, i: (jj // nheads, i, jj % nheads)),
                   pl.BlockSpec(memory_space=pl.ANY)],
        out_shape=[jax.ShapeDtypeStruct((depth, M, n), F32),
                   jax.ShapeDtypeStruct((depth, batch, S, nheads, hd), F32)],
        scratch_shapes=[pltpu.VMEM((D, hd), BF16), pltpu.VMEM((2, nb, S, hd), F32),
                        pltpu.SemaphoreType.DMA((2,))],
        compiler_params=_params(("arbitrary", "arbitrary"), vmem + (2 << 20)),
        name="mem_proj",
    )(h, w_kv)


def _gates_kernel(h_ref, w_ref, b_ref, o_ref):
    nh = MLSTM_HEADS
    pre = lax.dot_general(h_ref[...], w_ref[...], (((1,), (1,)), ((), ())),
                          preferred_element_type=F32) + b_ref[...]
    capped = GATE_SOFTCAP * jnp.tanh(pre / GATE_SOFTCAP)
    log_sig = jnp.minimum(capped, 0.0) - jnp.log(1.0 + jnp.exp(-jnp.abs(capped)))
    lane = lax.broadcasted_iota(jnp.int32, capped.shape, 1)
    o_ref[...] = jnp.where(lane < nh, capped, log_sig)


def mlstm_gates(h, w_gates, bias, tm=1024):
    M, D = h.shape
    tm = min(tm, M)
    return pl.pallas_call(
        _gates_kernel,
        grid=(M // tm,),
        in_specs=[pl.BlockSpec((tm, D), lambda i: (i, 0)),
                  pl.BlockSpec((GATE_LANES, D), lambda i: (0, 0)),
                  pl.BlockSpec((1, GATE_LANES), lambda i: (0, 0))],
        out_specs=pl.BlockSpec((tm, GATE_LANES), lambda i: (i, 0)),
        out_shape=jax.ShapeDtypeStruct((M, GATE_LANES), F32),
        compiler_params=_params(("parallel",), 4 * tm * D * 2 + 4 * D * GATE_LANES * 2),
        name="mlstm_gates",
    )(h, w_gates, bias)


def _group_mm_kernel(a_ref, w_ref, s_ref, z_ref, o_ref):
    mixed = jnp.dot(a_ref[...], w_ref[...], preferred_element_type=F32) * s_ref[...]
    o_ref[...] = (mixed * _silu(z_ref[...].astype(F32))).astype(o_ref.dtype)


def group_mix_gate(pooled, w_group, scale, proj, z_off, tm=512):
    M, D = pooled.shape
    G, Gs, _ = w_group.shape
    tm = min(tm, M)
    zb = z_off // Gs
    return pl.pallas_call(
        _group_mm_kernel,
        grid=(G, M // tm),
        in_specs=[pl.BlockSpec((tm, Gs), lambda g, i: (i, g)),
                  pl.BlockSpec((None, Gs, Gs), lambda g, i: (g, 0, 0)),
                  pl.BlockSpec((1, Gs), lambda g, i: (0, g)),
                  pl.BlockSpec((tm, Gs), lambda g, i: (i, zb + g))],
        out_specs=pl.BlockSpec((tm, Gs), lambda g, i: (i, g)),
        out_shape=jax.ShapeDtypeStruct((M, D), BF16),
        compiler_params=_params(("parallel", "parallel"), 6 * tm * Gs * 4 + 4 * Gs * Gs * 2),
        name="group_mix_gate",
    )(pooled, w_group, scale.reshape(1, D), proj)


def _window_sums(ext, levels):
    sums = []
    cur = ext
    for lv in range(levels):
        cur = cur + pltpu.roll(cur, 1 << lv, 0)
        sums.append(cur)
    return sums


def _pool_mix_prompt_kernel(u_ref, halo_ref, hist_ref, zlo_ref, zhi_ref, w_ref, s_ref, o_ref, *, pos0, tt):
    i = pl.program_id(1)
    ng = len(POOL_WINDOWS)
    gs = u_ref.shape[1] // ng
    row = lax.broadcasted_iota(jnp.int32, (tt, 1), 0) + i * tt
    pos1 = (row + (pos0 + 1)).astype(F32)
    first = i == 0
    for g, w in enumerate(POOL_WINDOWS):
        lo, hi = g * gs, (g + 1) * gs
        u = u_ref[:, lo:hi].astype(F32)
        halo = jnp.where(first, hist_ref[:, lo:hi], halo_ref[:, lo:hi].astype(F32))
        ext = jnp.concatenate([halo, u], axis=0)
        win = _window_sums(ext, g + 1)[-1][HALO:, :]
        inv_cnt = 1.0 / jnp.minimum(float(w), pos1)
        pooled = (win * inv_cnt - u).astype(BF16)
        mixed = jnp.dot(pooled, w_ref[g], preferred_element_type=F32) * s_ref[:, lo:hi]
        z_ref, zg = (zlo_ref, g) if g < ng // 2 else (zhi_ref, g - ng // 2)
        z = z_ref[:, zg * gs:(zg + 1) * gs].astype(F32)
        o_ref[:, lo:hi] = (mixed * _silu(z)).astype(o_ref.dtype)


def pool_mix_prompt(proj, hist, w_group, scale, batch, pos0, z_off, tt=512):
    D = hist.shape[-1]
    G, Gs, _ = w_group.shape
    T = proj.shape[0] // batch
    tt = min(tt, T)
    nt = T // tt
    dh = D // 2
    assert z_off % dh == 0 and G == len(POOL_WINDOWS)
    zb = z_off // dh
    hist16 = jnp.concatenate([jnp.zeros((batch, 1, D), F32), hist.astype(F32)], axis=1)
    vmem = 2 * (3 * tt * D * 2 + G * Gs * Gs * 2) + 12 * (tt + HALO) * Gs * 4
    return pl.pallas_call(
        functools.partial(_pool_mix_prompt_kernel, pos0=pos0, tt=tt),
        grid=(batch, nt),
        in_specs=[pl.BlockSpec((tt, D), lambda b, i: (b * nt + i, 0)),
                  pl.BlockSpec((HALO, D),
                               lambda b, i: (jnp.maximum((b * nt + i) * (tt // HALO) - 1, 0), 0)),
                  pl.BlockSpec((None, HALO, D), lambda b, i: (b, 0, 0)),
                  pl.BlockSpec((tt, dh), lambda b, i: (b * nt + i, zb)),
                  pl.BlockSpec((tt, dh), lambda b, i: (b * nt + i, zb + 1)),
                  pl.BlockSpec((G, Gs, Gs), lambda b, i: (0, 0, 0)),
                  pl.BlockSpec((1, D), lambda b, i: (0, 0))],
        out_specs=pl.BlockSpec((tt, D), lambda b, i: (b * nt + i, 0)),
        out_shape=jax.ShapeDtypeStruct((batch * T, D), BF16),
        compiler_params=_params(("parallel", "parallel"), vmem + (4 << 20)),
        name="pool_mix_prompt",
    )(proj, proj, hist16, proj, proj, w_group, scale.reshape(1, D))


def _pool_sample_kernel(hist_ref, u_ref, o_ref, new_hist_ref, *, pos0):
    steps = u_ref.shape[0]
    gs = u_ref.shape[2] // len(POOL_WINDOWS)

    def row(r, lo, hi):
        if r < POOL_HIST:
            return hist_ref[r, :, lo:hi]
        return u_ref[r - POOL_HIST, :, lo:hi].astype(F32)

    for g, w in enumerate(POOL_WINDOWS):
        lo, hi = g * gs, (g + 1) * gs
        for t in range(steps):
            cur = row(POOL_HIST + t, lo, hi)
            win = cur
            for r in range(1, w):
                win = win + row(POOL_HIST + t - r, lo, hi)
            cnt = float(min(w, pos0 + t + 1))
            o_ref[t, :, lo:hi] = (win / cnt - cur).astype(o_ref.dtype)
    for r in range(POOL_HIST):
        new_hist_ref[r] = row(r + steps, 0, u_ref.shape[2])


def pool_sample(hist_t, u_t, pos0, bt=32):
    T, B, D = u_t.shape
    bt = min(bt, B)
    return pl.pallas_call(
        functools.partial(_pool_sample_kernel, pos0=pos0),
        grid=(B // bt,),
        in_specs=[pl.BlockSpec((POOL_HIST, bt, D), lambda i: (0, i, 0)),
                  pl.BlockSpec((T, bt, D), lambda i: (0, i, 0))],
        out_specs=[pl.BlockSpec((T, bt, D), lambda i: (0, i, 0)),
                   pl.BlockSpec((POOL_HIST, bt, D), lambda i: (0, i, 0))],
        out_shape=[jax.ShapeDtypeStruct((T, B, D), BF16), jax.ShapeDtypeStruct((POOL_HIST, B, D), F32)],
        compiler_params=_params(("parallel",), 5 * (POOL_HIST + T) * bt * D * 4),
        name="pool_sample",
    )(hist_t, u_t)


def _softmax_rows(qk, scale):
    m = jnp.max(qk, axis=-1, keepdims=True)
    p = jnp.exp2((qk - m) * (scale * LOG2E))
    return p, jnp.sum(p, axis=-1, keepdims=True)


def _attn_prompt_kernel(q_ref, k_ref, v_ref, z_ref, o_ref):
    hd = q_ref.shape[1] // XATT_HEADS
    scale = hd ** -0.5
    cols = [(h * hd, (h + 1) * hd) for h in range(XATT_HEADS)]
    scores = [lax.dot_general(q_ref[:, lo:hi], k_ref[:, lo:hi].astype(BF16), (((1,), (1,)), ((), ())),
                              preferred_element_type=F32) for lo, hi in cols]
    probs = [_softmax_rows(s, scale) for s in scores]
    outs = [jnp.dot(p.astype(BF16), v_ref[:, lo:hi].astype(BF16), preferred_element_type=F32) / l
            for (lo, hi), (p, l) in zip(cols, probs)]
    for (lo, hi), o in zip(cols, outs):
        o_ref[:, lo:hi] = (o * _silu(z_ref[:, lo:hi].astype(F32))).astype(o_ref.dtype)


def attend_prompt(q_arr, qb, z_arr, zb, k, v, layer, S, tq=1024):
    _, BS, DX = k.shape
    B = BS // S
    T = q_arr.shape[0] // B
    tq = min(tq, T)
    nq = T // tq
    return pl.pallas_call(
        _attn_prompt_kernel,
        grid=(B, nq),
        in_specs=[pl.BlockSpec((tq, DX), lambda b, i: (b * nq + i, qb)),
                  pl.BlockSpec((None, S, DX), lambda b, i: (layer, b, 0)),
                  pl.BlockSpec((None, S, DX), lambda b, i: (layer, b, 0)),
                  pl.BlockSpec((tq, DX), lambda b, i: (b * nq + i, zb))],
        out_specs=pl.BlockSpec((tq, DX), lambda b, i: (b * nq + i, 0)),
        out_shape=jax.ShapeDtypeStruct((B * T, DX), BF16),
        compiler_params=_params(("parallel", "parallel"), 8 * tq * DX * 4 + 8 * S * DX * 4),
        name="attend_prompt",
    )(q_arr, k, v, z_arr)


def _attn_sample_kernel(q_ref, k_hbm, v_hbm, z_ref, o_ref, kbuf, vbuf, sem, *, steps, layer, bb):
    i = pl.program_id(0)
    rows, dx = q_ref.shape
    hd = dx // XATT_HEADS
    scale = hd ** -0.5

    def head_copies(step, slot):
        out = []
        for t, (src, dst) in enumerate(((k_hbm, kbuf), (v_hbm, vbuf))):
            for h in range(XATT_HEADS):
                out.append(pltpu.make_async_copy(src.at[layer, pl.ds(step * bb, bb), :, h, :],
                                                 dst.at[slot, h], sem.at[t, slot, h]))
        return out

    @pl.when(i == 0)
    def _():
        for cp in head_copies(0, 0):
            cp.start()

    @pl.when(i + 1 < pl.num_programs(0))
    def _():
        for cp in head_copies(i + 1, (i + 1) % 2):
            cp.start()

    slot = i % 2
    for cp in head_copies(i, slot):
        cp.wait()

    row_batch = lax.broadcasted_iota(jnp.int32, (rows, 1), 0) // steps
    pairs = [(h, j) for h in range(XATT_HEADS) for j in range(bb)]
    scores = [lax.dot_general(q_ref[:, h * hd:(h + 1) * hd], kbuf[slot, h, j].astype(BF16),
                              (((1,), (1,)), ((), ())), preferred_element_type=F32)
              for h, j in pairs]
    probs = [_softmax_rows(s, scale) for s in scores]
    outs = [jnp.dot(p.astype(BF16), vbuf[slot, h, j].astype(BF16), preferred_element_type=F32) / l
            for (h, j), (p, l) in zip(pairs, probs)]
    for h in range(XATT_HEADS):
        lo, hi = h * hd, (h + 1) * hd
        acc = jnp.zeros((rows, hd), F32)
        for j in range(bb):
            acc = jnp.where(row_batch == j, outs[h * bb + j], acc)
        o_ref[:, lo:hi] = (acc * _silu(z_ref[:, lo:hi].astype(F32))).astype(o_ref.dtype)


def attend_sample(q_arr, qb, z_arr, zb, k, v, layer, steps, bb=4):
    _, B, S, nh, hd = k.shape
    assert nh == XATT_HEADS
    DX = nh * hd
    rows = bb * steps
    slot_bytes = nh * bb * S * hd * 4
    return pl.pallas_call(
        functools.partial(_attn_sample_kernel, steps=steps, layer=layer, bb=bb),
        grid=(B // bb,),
        in_specs=[pl.BlockSpec((rows, DX), lambda i: (i, qb)),
                  pl.BlockSpec(memory_space=pl.ANY), pl.BlockSpec(memory_space=pl.ANY),
                  pl.BlockSpec((rows, DX), lambda i: (i, zb))],
        out_specs=pl.BlockSpec((rows, DX), lambda i: (i, 0)),
        out_shape=jax.ShapeDtypeStruct((B * steps, DX), BF16),
        scratch_shapes=[pltpu.VMEM((2, nh, bb, S, hd), F32), pltpu.VMEM((2, nh, bb, S, hd), F32),
                        pltpu.SemaphoreType.DMA((2, 2, nh))],
        compiler_params=_params(("arbitrary",), 4 * slot_bytes + (12 << 20)),
        name="attend_sample",
    )(q_arr, k, v, z_arr)


def _head_output(num, inv_den, o, z, g):
    ms = jnp.mean(num * num, axis=-1, keepdims=True) * (inv_den * inv_den)
    row_scale = inv_den * lax.rsqrt(ms + NORM_EPS)
    return (num * row_scale) * (g * (_sigmoid(o) * _silu(z)))


def _mlstm_prompt_kernel(q_ref, k_ref, v_ref, o_ref, zlo_ref, zhi_ref, g_ref, gh_ref,
                         y_ref, c_out, n_out, m_out, ct_scr, n_scr, m_scr):
    c = pl.program_id(1)
    nh = MLSTM_HEADS
    L = q_ref.shape[0]
    dqk = q_ref.shape[1] // nh
    dv = v_ref.shape[1] // nh
    qscale = dqk ** -0.5

    @pl.when(c == 0)
    def _():
        ct_scr[...] = jnp.zeros_like(ct_scr)
        n_scr[...] = jnp.zeros_like(n_scr)
        m_scr[...] = jnp.zeros_like(m_scr)

    r_i = lax.broadcasted_iota(jnp.int32, (L, L), 0)
    c_i = lax.broadcasted_iota(jnp.int32, (L, L), 1)
    causal = c_i <= r_i
    gates = g_ref[...]

    tril16 = jnp.where(causal, 1.0, 0.0).astype(BF16)
    g_hi = gates.astype(BF16)
    rest = gates - g_hi.astype(F32)
    g_mid = rest.astype(BF16)
    g_lo = (rest - g_mid.astype(F32)).astype(BF16)
    cums = (jnp.dot(tril16, g_hi, preferred_element_type=F32)
            + jnp.dot(tril16, g_mid, preferred_element_type=F32)
            + jnp.dot(tril16, g_lo, preferred_element_type=F32))
    gates_t = gates.T
    cums_t = cums.T

    def decay_weights(h):
        m_prev = m_scr[h:h + 1, 0:1]
        b_col = cums[:, nh + h:nh + h + 1]
        rel_row = gates_t[h:h + 1, :] - cums_t[nh + h:nh + h + 1, :]
        b2 = b_col * LOG2E
        d2 = jnp.where(causal, b2 + rel_row * LOG2E, -jnp.inf)
        inter2 = b2 + m_prev * LOG2E
        m2 = jnp.maximum(inter2, jnp.max(d2, axis=1, keepdims=True))
        return dict(m_prev=m_prev, b_col=b_col, m2=m2, m_t=m2 * LN2,
                    w=jnp.exp2(d2 - m2), w_inter=jnp.exp2(inter2 - m2))

    def scores(h):
        q = q_ref[:, h * dqk:(h + 1) * dqk].astype(F32) * qscale
        k = k_ref[:, h * dqk:(h + 1) * dqk]
        return q, lax.dot_general(q.astype(BF16), k, (((1,), (1,)), ((), ())), preferred_element_type=F32)

    def read_out(h, st, q, qk):
        v = v_ref[:, h * dv:(h + 1) * dv]
        s = qk * st["w"]
        qw = q * st["w_inter"]
        num = (jnp.dot(qw.astype(BF16), ct_scr[h].astype(BF16), preferred_element_type=F32)
               + jnp.dot(s.astype(BF16), v, preferred_element_type=F32))
        den = (jnp.sum(s, axis=1, keepdims=True)
               + jnp.sum(qw * n_scr[h:h + 1, :], axis=1, keepdims=True))
        return num, 1.0 / jnp.maximum(jnp.abs(den), jnp.exp2(-st["m2"]))

    def update_state(h, st):
        k = k_ref[:, h * dqk:(h + 1) * dqk]
        v = v_ref[:, h * dv:(h + 1) * dv]
        b_col, m_prev = st["b_col"], st["m_prev"]
        b_last = b_col[L - 1:L, :]
        m_new = st["m_t"][L - 1:L, :]
        w_state = jnp.exp(b_last - b_col + gates[:, h:h + 1] - m_new)
        decay = jnp.exp(b_last + m_prev - m_new)
        kw = k.astype(F32) * w_state
        ct_scr[h] = decay * ct_scr[h] + lax.dot_general(kw.astype(BF16), v, (((0,), (0,)), ((), ())),
                                                        preferred_element_type=F32)
        n_scr[h:h + 1, :] = decay * n_scr[h:h + 1, :] + jnp.sum(kw, axis=0, keepdims=True)
        m_scr[h:h + 1, :] = jnp.broadcast_to(m_new, (1, m_scr.shape[1]))

    def write_out(h, num, inv_den):
        lo, hi = h * dv, (h + 1) * dv
        z_ref, zh = (zlo_ref, h) if h < nh // 2 else (zhi_ref, h - nh // 2)
        y_ref[:, lo:hi] = _head_output(num, inv_den, o_ref[:, lo:hi].astype(F32),
                                       z_ref[:, zh * dv:(zh + 1) * dv].astype(F32),
                                       gh_ref[:, lo:hi]).astype(y_ref.dtype)

    for h0 in range(0, nh, MLSTM_HEAD_GROUP):
        group = range(h0, h0 + MLSTM_HEAD_GROUP)
        qks = {h: scores(h) for h in group}
        stats = {h: decay_weights(h) for h in group}
        outs = {h: read_out(h, stats[h], *qks[h]) for h in group}
        for h in group:
            update_state(h, stats[h])
        for h in group:
            write_out(h, *outs[h])

    @pl.when(c == pl.num_programs(1) - 1)
    def _():
        for h in range(nh):
            c_out[h] = ct_scr[h].T
        n_out[...] = n_scr[...]
        m_out[...] = m_scr[...]


def mlstm_prompt(proj, proj_z, gates, g_head, batch, dqk, dv):
    nh = MLSTM_HEADS
    T = proj.shape[0] // batch
    L = min(MLSTM_CHUNK, T)
    nc = T // L
    wq, wv = nh * dqk, nh * dv
    assert wv == 2 * wq
    rows = lambda b, c: b * nc + c
    return pl.pallas_call(
        _mlstm_prompt_kernel,
        grid=(batch, nc),
        in_specs=[pl.BlockSpec((L, wq), lambda b, c: (rows(b, c), 0)),
                  pl.BlockSpec((L, wq), lambda b, c: (rows(b, c), 1)),
                  pl.BlockSpec((L, wv), lambda b, c: (rows(b, c), 1)),
                  pl.BlockSpec((L, wv), lambda b, c: (rows(b, c), 2)),
                  pl.BlockSpec((L, wq), lambda b, c: (rows(b, c), 1)),
                  pl.BlockSpec((L, wq), lambda b, c: (rows(b, c), 2)),
                  pl.BlockSpec((L, GATE_LANES), lambda b, c: (rows(b, c), 0)),
                  pl.BlockSpec((1, wv), lambda b, c: (0, 0))],
        out_specs=[pl.BlockSpec((L, wv), lambda b, c: (rows(b, c), 0)),
                   pl.BlockSpec((None, nh, dv, dqk), lambda b, c: (b, 0, 0, 0)),
                   pl.BlockSpec((None, nh, dqk), lambda b, c: (b, 0, 0)),
                   pl.BlockSpec((None, nh, GATE_LANES), lambda b, c: (b, 0, 0))],
        out_shape=[jax.ShapeDtypeStruct((batch * T, wv), BF16),
                   jax.ShapeDtypeStruct((batch, nh, dv, dqk), F32),
                   jax.ShapeDtypeStruct((batch, nh, dqk), F32),
                   jax.ShapeDtypeStruct((batch, nh, GATE_LANES), F32)],
        scratch_shapes=[pltpu.VMEM((nh, dqk, dv), F32),
                        pltpu.VMEM((nh, dqk), F32),
                        pltpu.VMEM((nh, GATE_LANES), F32)],
        compiler_params=_params(("parallel", "arbitrary"),
                                24 * L * wv + 5 * nh * dv * dqk * 4 + (16 << 20)),
        name="mlstm_prompt",
    )(proj, proj, proj, proj, proj_z, proj_z, gates, g_head.reshape(1, wv))


def _lane_pick(x, lane, idx):
    return jnp.sum(jnp.where(lane == idx, x, 0.0), axis=1, keepdims=True)


def _mlstm_sample_kernel(q_ref, k_ref, v_ref, o_ref, z_ref, g_ref, gh_ref, c_ref, n_ref,
                         y_ref, c_out, n_out, m_out, *, steps):
    head = pl.program_id(1)
    nh = MLSTM_HEADS
    R, dqk = q_ref.shape
    nb = R // steps
    qscale = dqk ** -0.5

    gates = g_ref[...]
    lane = lax.broadcasted_iota(jnp.int32, gates.shape, 1)
    ig = _lane_pick(gates, lane, head)
    lf = _lane_pick(gates, lane, nh + head)
    m_prev = _lane_pick(gates, lane, 2 * nh + head)

    r_i = lax.broadcasted_iota(jnp.int32, (R, R), 0)
    c_i = lax.broadcasted_iota(jnp.int32, (R, R), 1)
    same = (r_i // steps) == (c_i // steps)
    causal = same & (c_i <= r_i)
    eye = (c_i == r_i).astype(F32)
    lower = causal.astype(F32)
    upper = (same & (r_i <= c_i)).astype(F32)
    last_of_row_batch = (c_i == (r_i // steps) * steps + (steps - 1)).astype(F32)
    row_batch = lax.broadcasted_iota(jnp.int32, (R, 1), 0) // steps

    q = q_ref[...].astype(F32) * qscale
    qb = q.astype(BF16)
    k = k_ref[...]
    v = v_ref[...]

    lf_row = jnp.sum(lf * eye, axis=0, keepdims=True)
    ig_row = jnp.sum(ig * eye, axis=0, keepdims=True)
    b_col = jnp.sum(lower * lf_row, axis=1, keepdims=True)
    b_row = jnp.sum(upper * lf, axis=0, keepdims=True)
    d = jnp.where(causal, b_col - b_row + ig_row, -jnp.inf)
    inter = b_col + m_prev
    m_t = jnp.maximum(inter, jnp.max(d, axis=1, keepdims=True))
    w = jnp.exp(d - m_t)
    w_inter = jnp.exp(inter - m_t)
    s = lax.dot_general(qb, k, (((1,), (1,)), ((), ())), preferred_element_type=F32) * w

    qc = jnp.zeros((R, v_ref.shape[1]), F32)
    n_rows = jnp.zeros((R, dqk), F32)
    for j in range(nb):
        mine = row_batch == j
        qcj = lax.dot_general(qb, c_ref[j].astype(BF16), (((1,), (1,)), ((), ())),
                              preferred_element_type=F32)
        qc = jnp.where(mine, qcj, qc)
        n_rows = jnp.where(mine, n_ref[j:j + 1, :], n_rows)

    num = jnp.dot(s.astype(BF16), v, preferred_element_type=F32) + w_inter * qc
    den = jnp.sum(s, axis=1, keepdims=True) + w_inter * jnp.sum(q * n_rows, axis=1, keepdims=True)
    inv_den = 1.0 / jnp.maximum(jnp.abs(den), jnp.exp(-m_t))
    y_ref[...] = _head_output(num, inv_den, o_ref[...].astype(F32), z_ref[...].astype(F32),
                              gh_ref[...]).astype(y_ref.dtype)

    m_row = jnp.sum(m_t * eye, axis=0, keepdims=True)
    b_last = jnp.sum(last_of_row_batch * b_row, axis=1, keepdims=True)
    m_new = jnp.sum(last_of_row_batch * m_row, axis=1, keepdims=True)
    w_state = jnp.exp(b_last - b_col + ig - m_new)
    decay = jnp.exp(b_last + m_prev - m_new)
    vw = v.astype(F32) * w_state
    kw = k.astype(F32) * w_state
    for j in range(nb):
        mine = row_batch == j
        r0 = j * steps
        dj = decay[r0:r0 + 1, :]
        upd = lax.dot_general(jnp.where(mine, vw, 0.0).astype(BF16), k, (((0,), (0,)), ((), ())),
                              preferred_element_type=F32)
        c_out[j] = dj * c_ref[j] + upd
        n_out[j:j + 1, :] = dj * n_ref[j:j + 1, :] + jnp.sum(jnp.where(mine, kw, 0.0), axis=0, keepdims=True)
        m_out[j:j + 1, :] = jnp.broadcast_to(m_new[r0:r0 + 1, :], (1, m_out.shape[1]))


def mlstm_sample(proj, proj_z, gates, g_head, c0, n0, steps, dqk, dv):
    nh = MLSTM_HEADS
    B = c0.shape[0]
    nb = SAMPLE_BATCH_BLOCK
    R = nb * steps
    kq = nh
    kv = (2 * nh * dqk) // dv
    ko = kv + nh
    kz = (nh * dqk) // dv
    return pl.pallas_call(
        functools.partial(_mlstm_sample_kernel, steps=steps),
        grid=(B // nb, nh),
        in_specs=[pl.BlockSpec((R, dqk), lambda i, h: (i, h)),
                  pl.BlockSpec((R, dqk), lambda i, h: (i, kq + h)),
                  pl.BlockSpec((R, dv), lambda i, h: (i, kv + h)),
                  pl.BlockSpec((R, dv), lambda i, h: (i, ko + h)),
                  pl.BlockSpec((R, dv), lambda i, h: (i, kz + h)),
                  pl.BlockSpec((R, GATE_LANES), lambda i, h: (i, 0)),
                  pl.BlockSpec((1, dv), lambda i, h: (0, h)),
                  pl.BlockSpec((nb, None, dv, dqk), lambda i, h: (i, h, 0, 0)),
                  pl.BlockSpec((None, nb, dqk), lambda i, h: (h, i, 0))],
        out_specs=[pl.BlockSpec((R, dv), lambda i, h: (i, h)),
                   pl.BlockSpec((nb, None, dv, dqk), lambda i, h: (i, h, 0, 0)),
                   pl.BlockSpec((None, nb, dqk), lambda i, h: (h, i, 0)),
                   pl.BlockSpec((None, nb, GATE_LANES), lambda i, h: (h, i, 0))],
        out_shape=[jax.ShapeDtypeStruct((B * steps, nh * dv), BF16),
                   jax.ShapeDtypeStruct(c0.shape, F32),
                   jax.ShapeDtypeStruct((nh, B, dqk), F32),
                   jax.ShapeDtypeStruct((nh, B, GATE_LANES), F32)],
        compiler_params=_params(("parallel", "parallel"), 6 * nb * dv * dqk * 4 + (8 << 20)),
        name="mlstm_sample",
    )(proj, proj, proj, proj, proj_z, gates, g_head.reshape(1, nh * dv), c0, n0)


def _finish_casting(x, h_mix, h_att, w_out, layer, g_post, g_next):
    y, w16 = matmul_cast([h_mix, h_att], w_out, layer, BF16, tm=512, tn=256)
    return post_norm_residual(x, y, g_post, g_next) + (w16,)


def _finish(x, h_mix, h_att, w16, g_post, g_next):
    y = matmul([h_mix, h_att], w16[None], BF16, tm=512, tn=1024)
    return post_norm_residual(x, y, g_post, g_next)


def kernel(x_prompt, x_sample, mem_prompt, state_pool, state_mlstm_C, state_mlstm_n, state_mlstm_m,
           cache_mem_k, cache_mem_v, norm_pre, norm_post, norm_mem, w_mem_kv, w_out,
           w_in_pool, w_pool_group, pool_scale, w_in_mlstm, b_igate, b_fgate, mlstm_head_norm):
    B, T, D = x_prompt.shape
    Bs, Ts, _ = x_sample.shape
    S = mem_prompt.shape[1]
    depth = w_out.shape[0]
    nh = MLSTM_HEADS
    dx = D // 2
    dv = D // nh
    dqk = dv // 2

    xp = x_prompt.reshape(B * T, D)
    xs = x_sample.reshape(Bs * Ts, D)
    mem = mem_prompt.reshape(B * S, D)
    hp = rmsnorm_cast(xp, norm_pre[:1])[0]
    hs = rmsnorm_cast(xs, norm_pre[:1])[0]

    h_mem = rmsnorm_cast(mem, norm_mem)
    mem_k, mem_k_heads = mem_proj(h_mem, w_mem_kv, 0, B, XATT_HEADS)
    mem_v, mem_v_heads = mem_proj(h_mem, w_mem_kv, dx, B, XATT_HEADS)
    w_mlstm_t = jnp.swapaxes(w_in_mlstm, 1, 2)

    pool_p_l, pool_s_l = [], []
    cp_l, np_l, mp_l, cs_l, ns_l, ms_l = [], [], [], [], [], []
    for layer in range(depth):
        j = layer // 2
        g_next = norm_pre[layer + 1] if layer + 1 < depth else None
        if layer % 2 == 0:
            w_g = w_pool_group[j].astype(BF16)
            z_mix, qb, zb = D + dx, D // dx, (2 * D + dx) // dx
            proj, w_pool16 = matmul_cast([hs], w_in_pool, j, BF16)
            u_t = proj[:, :D].reshape(Bs, Ts, D).transpose(1, 0, 2)
            pooled, hist_t = pool_sample(state_pool[j].transpose(1, 0, 2), u_t, PAST_LEN)
            pooled = pooled.transpose(1, 0, 2).reshape(Bs * Ts, D)
            mix = group_mix_gate(pooled, w_g, pool_scale[j], proj, z_mix)
            att = attend_sample(proj, qb, proj, zb, cache_mem_k, cache_mem_v, layer, Ts)
            xs, hs, w_o16 = _finish_casting(xs, mix, att, w_out, layer, norm_post[layer], g_next)
            pool_s_l.append(hist_t.transpose(1, 0, 2))
            proj = matmul([hp], w_pool16[None], BF16)
            mix = pool_mix_prompt(proj, jnp.zeros((B, POOL_HIST, D), F32), w_g, pool_scale[j], B, 0, z_mix)
            att = attend_prompt(proj, qb, proj, zb, mem_k, mem_v, layer, S)
            xp, hp = _finish(xp, mix, att, w_o16, norm_post[layer], g_next)
            pool_p_l.append(proj.reshape(B, T, -1)[:, T - POOL_HIST:, :D].astype(F32))
        else:
            bias = jnp.pad(jnp.concatenate([b_igate[j], b_fgate[j]]), (0, GATE_LANES - 2 * nh))
            bias = bias.reshape(1, GATE_LANES)
            qb, zb = 0, (dx + D) // dx
            g0 = 3 * D
            w_gates = jnp.pad(w_mlstm_t[j, g0:g0 + 2 * nh, :], ((0, GATE_LANES - 2 * nh), (0, 0))).astype(BF16)
            proj, w_main = matmul_cast([hs], w_mlstm_t, j, BF16, 0, g0, transposed=True)
            proj_z, w_z = matmul_cast([hs], w_mlstm_t, j, BF16, g0 + 2 * nh, None, transposed=True)
            gates = mlstm_gates(hs, w_gates, bias)
            m_rows = jnp.repeat(state_mlstm_m[j], Ts, axis=0)
            gates = jnp.concatenate([gates[:, :2 * nh], m_rows,
                                     jnp.zeros((Bs * Ts, GATE_LANES - 3 * nh), F32)], axis=1)
            mix, c1, n1, m1 = mlstm_sample(proj, proj_z, gates, mlstm_head_norm[j], state_mlstm_C[j],
                                           state_mlstm_n[j].transpose(1, 0, 2), Ts, dqk, dv)
            att = attend_sample(proj_z, qb, proj_z, zb, cache_mem_k, cache_mem_v, layer, Ts)
            xs, hs, w_o16 = _finish_casting(xs, mix, att, w_out, layer, norm_post[layer], g_next)
            cs_l.append(c1); ns_l.append(n1.transpose(1, 0, 2)); ms_l.append(m1[:, :, 0].T)
            proj = matmul_nt(hp, w_main, BF16)
            proj_z = matmul_nt(hp, w_z, BF16)
            gates = mlstm_gates(hp, w_gates, bias)
            mix, c1, n1, m1 = mlstm_prompt(proj, proj_z, gates, mlstm_head_norm[j], B, dqk, dv)
            att = attend_prompt(proj_z, qb, proj_z, zb, mem_k, mem_v, layer, S)
            xp, hp = _finish(xp, mix, att, w_o16, norm_post[layer], g_next)
            cp_l.append(c1); np_l.append(n1); mp_l.append(m1[:, :, 0])

    return (xp.reshape(B, T, D), xs.reshape(Bs, Ts, D), jnp.stack(pool_p_l),
            mem_k_heads, mem_v_heads,
            jnp.stack(cp_l), jnp.stack(np_l), jnp.stack(mp_l),
            jnp.stack(pool_s_l), jnp.stack(cs_l), jnp.stack(ns_l), jnp.stack(ms_l))
```
